```python
import math
import jax, jax.numpy as jnp
from jax import lax
import numpy as np

D_MODEL = 1024
BATCH = 2
SEQ = 16384
DEPTH = 2

HEAD_DIM = 64
DILATED_GROUPS = ((128, 1), (512, 4), (2048, 16))
N_DIL = len(DILATED_GROUPS)
HEADS_PER_GROUP = 8
N_ATTN_HEADS = N_DIL * HEADS_PER_GROUP
ATTN_WIDTH = N_ATTN_HEADS * HEAD_DIM
ATTN_OUT_WIDTH = HEADS_PER_GROUP * HEAD_DIM
ATTN_BLOCK = 128
N_REL_BUCKETS = 32
REL_MAX_DISTANCE = 2048

D_INNER = 2 * D_MODEL
SSM_HEAD_DIM = 64
N_SSM_HEADS = D_INNER // SSM_HEAD_DIM
N_SSM_GROUPS = 4
HEADS_PER_SSM_GROUP = N_SSM_HEADS // N_SSM_GROUPS
D_STATE = 128
CONV_WIDTH = 4
SSD_CHUNK = 128
XBC_WIDTH = D_INNER + 2 * N_SSM_GROUPS * D_STATE

N_BRANCHES = 2
D_FF = -(-8 * D_MODEL // (3 * 256)) * 256
IN_PROJ_WIDTH = 3 * ATTN_WIDTH + D_INNER + XBC_WIDTH + N_SSM_HEADS + N_BRANCHES * D_MODEL
EPS = 1e-6

kernel_name = "hybrid_dilated_attn_ssd_gated_block"


def rmsnorm(x, w):
    xf = x.astype(jnp.float32)
    y = xf * lax.rsqrt(jnp.mean(xf * xf, axis=-1, keepdims=True) + EPS)
    return (y * w.astype(jnp.float32)).astype(x.dtype)


def t5_causal_bucket(dist):
    max_exact = N_REL_BUCKETS // 2
    d_f = jnp.maximum(dist, 1).astype(jnp.float32)
    large = max_exact + (jnp.log(d_f / max_exact) / math.log(REL_MAX_DISTANCE / max_exact)
                         * (N_REL_BUCKETS - max_exact)).astype(jnp.int32)
    large = jnp.minimum(large, N_REL_BUCKETS - 1)
    return jnp.where(dist < max_exact, dist, large)


def rel_bias_block(rel_bias_g, dilation, n_steps):
    qi = jnp.arange(ATTN_BLOCK)[:, None]
    kj = jnp.arange(2 * ATTN_BLOCK)[None, :]
    steps = jnp.clip(qi + ATTN_BLOCK - kj, 0, n_steps)
    bucket = t5_causal_bucket(steps * dilation)
    return jnp.transpose(rel_bias_g[bucket], (2, 0, 1)).astype(jnp.float32)


def dilated_window_attention(q, k, v, bias, dilation, n_steps):
    b, s, h, dh = q.shape
    seg = s // dilation
    nb = -(-seg // ATTN_BLOCK)
    segp = nb * ATTN_BLOCK

    def to_blocks(t):
        t = t.reshape(b, seg, dilation, h, dh).transpose(0, 2, 1, 3, 4).reshape(b * dilation, seg, h, dh)
        t = jnp.pad(t, ((0, 0), (0, segp - seg), (0, 0), (0, 0)))
        return t.reshape(b * dilation, nb, ATTN_BLOCK, h, dh)

    def with_prev(t):
        prev = jnp.pad(t[:, :-1], ((0, 0), (1, 0), (0, 0), (0, 0), (0, 0)))
        return jnp.concatenate([prev, t], axis=2)

    qb = to_blocks(q)
    kw = with_prev(to_blocks(k))
    vw = with_prev(to_blocks(v))

    qi = jnp.arange(ATTN_BLOCK)[:, None]
    kj = jnp.arange(2 * ATTN_BLOCK)[None, :]
    steps = qi + ATTN_BLOCK - kj
    blk = jnp.arange(nb)[:, None, None]
    valid = (steps >= 0) & (steps <= n_steps) & (blk * ATTN_BLOCK - ATTN_BLOCK + kj >= 0)

    logits = jnp.einsum('znqhd,znkhd->znhqk', qb, kw).astype(jnp.float32) * (HEAD_DIM ** -0.5) + bias
    logits = jnp.where(valid[None, :, None], logits, -jnp.inf)
    m = jnp.max(logits, axis=-1, keepdims=True)
    p = jnp.exp(logits - m)
    den = jnp.sum(p, axis=-1, keepdims=True)
    o = jnp.einsum('znhqk,znkhd->znqhd', p, vw.astype(jnp.float32)) / jnp.swapaxes(den, 2, 3)
    lse = jnp.swapaxes((m + jnp.log(den))[..., 0], 2, 3)

    o = o.reshape(b, dilation, segp, h, dh)[:, :, :seg]
    o = jnp.swapaxes(o, 1, 2).reshape(b, s, h, dh)
    lse = lse.reshape(b, dilation, segp, h)[:, :, :seg]
    lse = jnp.swapaxes(lse, 1, 2).reshape(b, s, h)
    return o, lse


def causal_depthwise_conv(x, w, bias):
    c = x.shape[-1]
    y = lax.conv_general_dilated(x, w[:, None, :], window_strides=(1,),
                                 padding=((CONV_WIDTH - 1, 0),),
                                 dimension_numbers=('NWC', 'WIO', 'NWC'),
                                 feature_group_count=c)
    return y + bias


def ssd_chunked_scan(xh, dt, a, bm, cm):
    b, s = xh.shape[:2]
    nc = s // SSD_CHUNK

    def chunks(t):
        return jnp.swapaxes(t.reshape(b, nc, SSD_CHUNK, *t.shape[2:]), 0, 1)

    causal = jnp.tril(jnp.ones((SSD_CHUNK, SSD_CHUNK), dtype=bool))[None, :, :, None, None]

    def step(state, inp):
        x_, dt_, b_, c_ = inp
        la = jnp.cumsum(dt_ * a, axis=1)
        seg = la[:, :, None] - la[:, None, :]
        decay = jnp.exp(jnp.where(causal, seg, -jnp.inf))
        cb = jnp.einsum('bign,bjgn->bijg', c_, b_)
        xdt = x_ * dt_[..., None]
        y_intra = jnp.einsum('bijg,bijgh,bjghp->bighp', cb, decay, xdt)
        y_inter = jnp.einsum('bign,bghpn->bighp', c_, state) * jnp.exp(la)[..., None]
        to_end = jnp.exp(la[:, -1:] - la)
        new_state = (state * jnp.exp(la[:, -1])[..., None, None]
                     + jnp.einsum('bjgn,bjgh,bjghp->bghpn', b_, to_end, xdt))
        return new_state, y_intra + y_inter

    state0 = jnp.zeros((b, N_SSM_GROUPS, HEADS_PER_SSM_GROUP, SSM_HEAD_DIM, D_STATE), jnp.float32)
    _, ys = lax.scan(step, state0, (chunks(xh), chunks(dt), chunks(bm), chunks(cm)))
    return jnp.swapaxes(ys, 0, 1).reshape(b, s, N_SSM_GROUPS, HEADS_PER_SSM_GROUP, SSM_HEAD_DIM)


def hybrid_mixer(xn, w_in, conv_w, conv_b, dt_bias, a_log, d_skip, ssm_norm_w,
                 w_attn_branch, w_ssm_branch, w_out, attn_biases):
    b, s, _ = xn.shape
    proj = xn @ w_in
    q, k, v, z, xbc, dt_raw, gate_logits = jnp.split(
        proj, np.cumsum([ATTN_WIDTH, ATTN_WIDTH, ATTN_WIDTH, D_INNER, XBC_WIDTH, N_SSM_HEADS]).tolist(), axis=-1)

    q = q.reshape(b, s, N_DIL, HEADS_PER_GROUP, HEAD_DIM)
    k = k.reshape(b, s, N_DIL, HEADS_PER_GROUP, HEAD_DIM)
    v = v.reshape(b, s, N_DIL, HEADS_PER_GROUP, HEAD_DIM)
    outs, lses = [], []
    for g, (window, dil) in enumerate(DILATED_GROUPS):
        o_g, lse_g = dilated_window_attention(q[:, :, g], k[:, :, g], v[:, :, g],
                                              attn_biases[g], dil, window // dil)
        outs.append(o_g)
        lses.append(lse_g)
    o = jnp.stack(outs, axis=2)
    alpha = jax.nn.softmax(jnp.stack(lses, axis=2), axis=2)
    attn = jnp.sum(alpha[..., None] * o, axis=2).reshape(b, s, ATTN_OUT_WIDTH).astype(xn.dtype)

    xbc = jax.nn.silu(causal_depthwise_conv(xbc, conv_w, conv_b))
    xs, bm, cm = jnp.split(xbc, [D_INNER, D_INNER + N_SSM_GROUPS * D_STATE], axis=-1)
    xh = xs.reshape(b, s, N_SSM_GROUPS, HEADS_PER_SSM_GROUP, SSM_HEAD_DIM).astype(jnp.float32)
    bm = bm.reshape(b, s, N_SSM_GROUPS, D_STATE).astype(jnp.float32)
    cm = cm.reshape(b, s, N_SSM_GROUPS, D_STATE).astype(jnp.float32)
    dt = jax.nn.softplus(dt_raw.astype(jnp.float32) + dt_bias.astype(jnp.float32))
    dt = dt.reshape(b, s, N_SSM_GROUPS, HEADS_PER_SSM_GROUP)
    a = -jnp.exp(a_log.astype(jnp.float32)).reshape(N_SSM_GROUPS, HEADS_PER_SSM_GROUP)
    y = ssd_chunked_scan(xh, dt, a, bm, cm)
    y = y + xh * d_skip.astype(jnp.float32).reshape(N_SSM_GROUPS, HEADS_PER_SSM_GROUP)[..., None]
    yg = (y.reshape(b, s, D_INNER) * jax.nn.silu(z.astype(jnp.float32))).reshape(b, s, N_SSM_GROUPS, D_INNER // N_SSM_GROUPS)
    yg = yg * lax.rsqrt(jnp.mean(yg * yg, axis=-1, keepdims=True) + EPS)
    ssm = (yg.reshape(b, s, D_INNER) * ssm_norm_w.astype(jnp.float32)).astype(xn.dtype)

    gates = jax.nn.sigmoid(gate_logits.astype(jnp.float32)).reshape(b, s, N_BRANCHES, D_MODEL)
    merged = (gates[:, :, 0] * (attn @ w_attn_branch).astype(jnp.float32)
              + gates[:, :, 1] * (ssm @ w_ssm_branch).astype(jnp.float32)).astype(xn.dtype)
    return merged @ w_out


def swiglu(xn, w_ffn_in, w_ffn_out):
    gate, up = jnp.split(xn @ w_ffn_in, 2, axis=-1)
    return (jax.nn.silu(gate) * up) @ w_ffn_out


def setup_inputs(seed: int = 0) -> dict:
    key = jax.random.key(seed)
    ks = jax.random.split(key, 20)
    f32 = jnp.float32
    nrm = lambda k, shape, scale: jax.random.normal(k, shape, f32) * scale
    dt0 = jnp.exp(jax.random.uniform(ks[5], (DEPTH, N_SSM_HEADS), f32, math.log(1e-3), math.log(1e-1)))
    return {
        "x": nrm(ks[0], (BATCH, SEQ, D_MODEL), 1.0),
        "norm1_w": 1.0 + nrm(ks[1], (DEPTH, D_MODEL), 0.02),
        "w_in": nrm(ks[2], (DEPTH, D_MODEL, IN_PROJ_WIDTH), D_MODEL ** -0.5),
        "conv_w": nrm(ks[3], (DEPTH, CONV_WIDTH, XBC_WIDTH), CONV_WIDTH ** -0.5),
        "conv_b": nrm(ks[4], (DEPTH, XBC_WIDTH), 0.02),
        "dt_bias": dt0 + jnp.log(-jnp.expm1(-dt0)),
        "a_log": jnp.log(jax.random.uniform(ks[6], (DEPTH, N_SSM_HEADS), f32, 1.0, 16.0)),
        "d_skip": 1.0 + nrm(ks[7], (DEPTH, N_SSM_HEADS), 0.1),
        "ssm_norm_w": 1.0 + nrm(ks[8], (DEPTH, D_INNER), 0.02),
        "w_attn_branch": nrm(ks[9], (DEPTH, ATTN_OUT_WIDTH, D_MODEL), ATTN_OUT_WIDTH ** -0.5),
        "w_ssm_branch": nrm(ks[10], (DEPTH, D_INNER, D_MODEL), D_INNER ** -0.5),
        "w_out": nrm(ks[11], (DEPTH, D_MODEL, D_MODEL), D_MODEL ** -0.5),
        "norm2_w": 1.0 + nrm(ks[12], (DEPTH, D_MODEL), 0.02),
        "w_ffn_in": nrm(ks[13], (DEPTH, D_MODEL, 2 * D_FF), D_MODEL ** -0.5),
        "w_ffn_out": nrm(ks[14], (DEPTH, D_FF, D_MODEL), D_FF ** -0.5),
        "rel_bias": nrm(ks[15], (N_REL_BUCKETS, N_ATTN_HEADS), 0.5),
        "final_norm_w": 1.0 + nrm(ks[16], (D_MODEL,), 0.02),
    }


def reference(x, norm1_w, w_in, conv_w, conv_b, dt_bias, a_log, d_skip, ssm_norm_w,
              w_attn_branch, w_ssm_branch, w_out, norm2_w, w_ffn_in, w_ffn_out,
              rel_bias, final_norm_w):
    attn_biases = [rel_bias_block(rel_bias[:, g * HEADS_PER_GROUP:(g + 1) * HEADS_PER_GROUP], dil, window // dil)
                   for g, (window, dil) in enumerate(DILATED_GROUPS)]
    h = x
    for layer in range(DEPTH):
        h = h + hybrid_mixer(rmsnorm(h, norm1_w[layer]), w_in[layer], conv_w[layer], conv_b[layer],
                             dt_bias[layer], a_log[layer], d_skip[layer], ssm_norm_w[layer],
                             w_attn_branch[layer], w_ssm_branch[layer], w_out[layer], attn_biases)
        h = h + swiglu(rmsnorm(h, norm2_w[layer]), w_ffn_in[layer], w_ffn_out[layer])
    return rmsnorm(h, final_norm_w)
```

```python
import functools
import math

import jax
import jax.numpy as jnp
from jax import lax
from jax.experimental import pallas as pl
from jax.experimental.pallas import tpu as pltpu

HEAD_DIM = 64
DILATED_GROUPS = ((128, 1), (512, 4), (2048, 16))
N_DIL = len(DILATED_GROUPS)
HEADS_PER_GROUP = 8
GROUP_WIDTH = HEADS_PER_GROUP * HEAD_DIM
ATTN_WIDTH = N_DIL * GROUP_WIDTH
ATTN_BLOCK = 128
N_REL_BUCKETS = 32
REL_MAX_DISTANCE = 2048
SSM_HEAD_DIM = 64
N_SSM_GROUPS = 4
HEADS_PER_SSM_GROUP = 8
D_STATE = 128
CONV_WIDTH = 4
SSD_CHUNK = 128
EPS = 1e-6

LANES = 128
SUBLANES = 8
VMEM_LIMIT_BYTES = 48 * 1024 * 1024

BF16 = jnp.bfloat16
F32 = jnp.float32
NT_DIMS = (((1,), (1,)), ((), ()))


def _compiler_params(semantics):
    return pltpu.CompilerParams(dimension_semantics=semantics, vmem_limit_bytes=VMEM_LIMIT_BYTES)


def _rmsnorm(x, w):
    return x * lax.rsqrt(jnp.mean(x * x, axis=-1, keepdims=True) + EPS) * w


def _sigmoid(x):
    return 1.0 / (1.0 + jnp.exp(-x))


def _in_proj_kernel(x_ref, nw_ref, w_ref, wdt_ref, proj_ref, dt_ref, xn_ref):
    @pl.when(pl.program_id(1) == 0)
    def _():
        xn = _rmsnorm(x_ref[...], nw_ref[...]).astype(BF16)
        xn_ref[...] = xn
        dt_ref[...] = jnp.dot(xn, wdt_ref[...], preferred_element_type=F32)

    proj_ref[...] = jnp.dot(xn_ref[...], w_ref[...], preferred_element_type=F32).astype(proj_ref.dtype)


def _in_proj(h, norm_w, w_main, w_dt, *, tm, tn):
    t, d = h.shape
    n = w_main.shape[1]
    return pl.pallas_call(
        _in_proj_kernel,
        grid=(t // tm, n // tn),
        in_specs=[
            pl.BlockSpec((tm, d), lambda i, j: (i, 0)),
            pl.BlockSpec((1, d), lambda i, j: (0, 0)),
            pl.BlockSpec((d, tn), lambda i, j: (0, j)),
            pl.BlockSpec((d, LANES), lambda i, j: (0, 0)),
        ],
        out_specs=[
            pl.BlockSpec((tm, tn), lambda i, j: (i, j)),
            pl.BlockSpec((tm, LANES), lambda i, j: (i, 0)),
        ],
        out_shape=[
            jax.ShapeDtypeStruct((t, n), BF16),
            jax.ShapeDtypeStruct((t, LANES), F32),
        ],
        scratch_shapes=[pltpu.VMEM((tm, d), BF16)],
        compiler_params=_compiler_params(("parallel", "arbitrary")),
        name="in_proj",
    )(h, norm_w, w_main, w_dt)


def _attention_kernel(q_ref, kp_ref, kc_ref, vp_ref, vc_ref, bias_ref, o_ref, l_ref, kext_ref, vext_ref, *, tq):
    n = pl.program_id(2)
    kext_ref[0:ATTN_BLOCK] = kp_ref[0]
    kext_ref[ATTN_BLOCK:] = kc_ref[0]
    vext_ref[0:ATTN_BLOCK] = vp_ref[0]
    vext_ref[ATTN_BLOCK:] = vc_ref[0]
    low_half = lax.broadcasted_iota(jnp.int32, (1, LANES), 1) < HEAD_DIM

    def block(s, carry):
        r0 = pl.multiple_of(s * ATTN_BLOCK, ATTN_BLOCK)
        qs = q_ref[0, pl.ds(r0, ATTN_BLOCK), :] * (HEAD_DIM ** -0.5)
        ks = kext_ref[pl.ds(r0, 2 * ATTN_BLOCK), :]
        vs = vext_ref[pl.ds(r0, 2 * ATTN_BLOCK), :]
        bidx = jnp.where(jnp.logical_and(n == 0, s == 0), 0, 1)
        for p in range(GROUP_WIDTH // LANES):
            cols = slice(p * LANES, (p + 1) * LANES)
            qp, kp, vp = qs[:, cols], ks[:, cols], vs[:, cols]
            outs, lses = [], []
            for e in range(2):
                qe = jnp.where(low_half if e == 0 else jnp.logical_not(low_half), qp, jnp.zeros_like(qp))
                sc = lax.dot_general(qe, kp, NT_DIMS, preferred_element_type=F32) + bias_ref[bidx, 2 * p + e]
                m = jnp.max(sc, axis=1, keepdims=True)
                pe = jnp.exp(sc - m)
                den = jnp.sum(pe, axis=1, keepdims=True)
                num = jnp.dot(pe.astype(BF16), vp, preferred_element_type=F32)
                outs.append(num / den)
                lses.append(jnp.broadcast_to(m + jnp.log(den), (ATTN_BLOCK, LANES)))
            o_ref[0, pl.ds(r0, ATTN_BLOCK), cols] = jnp.where(low_half, outs[0], outs[1])
            l_ref[0, pl.ds(r0, ATTN_BLOCK), cols] = jnp.where(low_half, lses[0], lses[1])
        return carry

    lax.fori_loop(0, tq // ATTN_BLOCK, block, 0)


def _attention(proj3, bias_g, *, dilation, col_q, col_k, col_v, tq):
    b, seg, width = proj3.shape
    ncol = width // dilation // GROUP_WIDTH
    blocks_per_tile = tq // ATTN_BLOCK

    def cur(col):
        return pl.BlockSpec((1, tq, GROUP_WIDTH), lambda bi, r, n: (bi, n, r * ncol + col))

    def prev(col):
        return pl.BlockSpec((1, ATTN_BLOCK, GROUP_WIDTH),
                            lambda bi, r, n: (bi, jnp.maximum(n * blocks_per_tile - 1, 0), r * ncol + col))

    out_spec = pl.BlockSpec((1, tq, GROUP_WIDTH), lambda bi, r, n: (bi, n, r))
    out_sds = jax.ShapeDtypeStruct((b, seg, dilation * GROUP_WIDTH), F32)
    return pl.pallas_call(
        functools.partial(_attention_kernel, tq=tq),
        grid=(b, dilation, seg // tq),
        in_specs=[cur(col_q), prev(col_k), cur(col_k), prev(col_v), cur(col_v),
                  pl.BlockSpec(bias_g.shape, lambda bi, r, n: (0, 0, 0, 0))],
        out_specs=[out_spec, out_spec],
        out_shape=[out_sds, out_sds],
        scratch_shapes=[pltpu.VMEM((tq + ATTN_BLOCK, GROUP_WIDTH), BF16),
                        pltpu.VMEM((tq + ATTN_BLOCK, GROUP_WIDTH), BF16)],
        compiler_params=_compiler_params(("parallel", "parallel", "arbitrary")),
        name=f"attention_d{dilation}",
    )(proj3, proj3, proj3, proj3, proj3, bias_g)


def _t5_causal_bucket(dist):
    max_exact = N_REL_BUCKETS // 2
    d_f = jnp.maximum(dist, 1).astype(F32)
    large = max_exact + (jnp.log(d_f / max_exact) / math.log(REL_MAX_DISTANCE / max_exact)
                         * (N_REL_BUCKETS - max_exact)).astype(jnp.int32)
    large = jnp.minimum(large, N_REL_BUCKETS - 1)
    return jnp.where(dist < max_exact, dist, large)


def _attention_bias_tables(rel_bias):
    qi = jnp.arange(ATTN_BLOCK)[:, None]
    kj = jnp.arange(2 * ATTN_BLOCK)[None, :]
    steps = qi + ATTN_BLOCK - kj
    tables = []
    for g, (window, dil) in enumerate(DILATED_GROUPS):
        n_steps = window // dil
        bucket = _t5_causal_bucket(jnp.clip(steps, 0, n_steps) * dil)
        rel_g = rel_bias[:, g * HEADS_PER_GROUP:(g + 1) * HEADS_PER_GROUP]
        bias = jnp.transpose(rel_g[bucket], (2, 0, 1)).astype(F32)
        valid = (steps >= 0) & (steps <= n_steps)
        rest = jnp.where(valid[None], bias, -jnp.inf)
        first = jnp.where((valid & (kj >= ATTN_BLOCK))[None], bias, -jnp.inf)
        tables.append(jnp.stack([first, rest]))
    return jnp.stack(tables)


def _conv_silu(x, halo_ref, w_ref, b_ref):
    prev = halo_ref[...]
    q = x.shape[0]
    halo_ref[...] = x[q - SUBLANES:]
    row = lax.broadcasted_iota(jnp.int32, (SUBLANES, 1), 0)
    acc = x * w_ref[CONV_WIDTH - 1:CONV_WIDTH, :] + b_ref[...]
    for s in range(1, CONV_WIDTH):
        rolled = pltpu.roll(x, s, axis=0)
        top = jnp.where(row < s, pltpu.roll(prev, s, axis=0), rolled[:SUBLANES])
        shifted = jnp.concatenate([top, rolled[SUBLANES:]], axis=0)
        acc = acc + shifted * w_ref[CONV_WIDTH - 1 - s:CONV_WIDTH - s, :]
    return acc * _sigmoid(acc)


def _ssd_kernel(xs_ref, bc_ref, z_ref, dt_ref, cwx_ref, cwb_ref, cbx_ref, cbb_ref, dtb_ref, alog_ref,
                dskip_ref, nw_ref, out_ref, state_ref, hx_ref, hb_ref):
    q = SSD_CHUNK
    gw = HEADS_PER_SSM_GROUP * SSM_HEAD_DIM

    @pl.when(pl.program_id(1) == 0)
    def _():
        state_ref[...] = jnp.zeros_like(state_ref)
        hx_ref[...] = jnp.zeros_like(hx_ref)
        hb_ref[...] = jnp.zeros_like(hb_ref)

    x_dt = dt_ref[0] + dtb_ref[...]
    dt = jnp.maximum(x_dt, 0.0) + jnp.log1p(jnp.exp(-jnp.abs(x_dt)))
    d_a = dt * (-jnp.exp(alog_ref[...]))
    ri = lax.broadcasted_iota(jnp.int32, (q, q), 0)
    ci = lax.broadcasted_iota(jnp.int32, (q, q), 1)
    causal = ri >= ci
    tril = jnp.where(causal, 1.0, 0.0).astype(BF16)
    d1 = d_a.astype(BF16)
    r1 = d_a - d1.astype(F32)
    d2 = r1.astype(BF16)
    d3 = (r1 - d2.astype(F32)).astype(BF16)
    la = (jnp.dot(tril, d1, preferred_element_type=F32) + jnp.dot(tril, d2, preferred_element_type=F32)
          + jnp.dot(tril, d3, preferred_element_type=F32))
    la_t = la.T
    dt_t = dt.T
    last = jnp.broadcast_to(la_t[:, q - 1:q], (q, q))
    w_t = jnp.exp(last - la_t) * dt_t
    state_decay = jnp.exp(last)

    xs = _conv_silu(xs_ref[0].astype(F32), hx_ref, cwx_ref, cbx_ref)
    bc = _conv_silu(bc_ref[0].astype(F32), hb_ref, cwb_ref, cbb_ref)

    for g in range(N_SSM_GROUPS):
        bm = bc[:, g * D_STATE:(g + 1) * D_STATE]
        cm = bc[:, (N_SSM_GROUPS + g) * D_STATE:(N_SSM_GROUPS + g + 1) * D_STATE]
        bm16 = bm.astype(BF16)
        cb = lax.dot_general(cm.astype(BF16), bm16, NT_DIMS, preferred_element_type=F32)
        xs_g = xs[:, g * gw:(g + 1) * gw]
        xs_t = xs_g.T
        y_t = []
        for h in range(HEADS_PER_SSM_GROUP):
            hh = g * HEADS_PER_SSM_GROUP + h
            rows = slice(hh * SSM_HEAD_DIM, (hh + 1) * SSM_HEAD_DIM)
            la_i = jnp.broadcast_to(la[:, hh:hh + 1], (q, q))
            decay = jnp.exp(jnp.where(causal, la_i - la_t[hh:hh + 1, :], -jnp.inf))
            lhs = jnp.concatenate([(cb * decay).astype(BF16), (cm * jnp.exp(la_i)).astype(BF16)], axis=1)
            x_h = xs_t[h * SSM_HEAD_DIM:(h + 1) * SSM_HEAD_DIM]
            state = state_ref[rows, :]
            rhs_t = jnp.concatenate([(x_h * dt_t[hh:hh + 1, :]).astype(BF16), state.astype(BF16)], axis=1)
            y_t.append(lax.dot_general(rhs_t, lhs, NT_DIMS, preferred_element_type=F32))
            upd = jnp.dot((x_h * w_t[hh:hh + 1, :]).astype(BF16), bm16, preferred_element_type=F32)
            state_ref[rows, :] = state * state_decay[hh:hh + 1, :] + upd
        y = jnp.concatenate(y_t, axis=0).T
        cols = slice(g * gw, (g + 1) * gw)
        y = y + xs_g * dskip_ref[:, cols]
        z = z_ref[0, :, cols].astype(F32)
        yg = y * (z * _sigmoid(z))
        yg = yg * lax.rsqrt(jnp.mean(yg * yg, axis=-1, keepdims=True) + EPS)
        out_ref[0, :, cols] = (yg * nw_ref[:, cols]).astype(out_ref.dtype)


def _ssd(proj3, dt3, conv_w_xs, conv_w_bc, conv_b_xs, conv_b_bc, dt_bias, a_log, d_skip, norm_w,
         *, col_z, col_xs, col_bc):
    b, s, _ = proj3.shape
    d_inner = conv_w_xs.shape[1]
    bc_w = conv_w_bc.shape[1]
    n_heads = N_SSM_GROUPS * HEADS_PER_SSM_GROUP
    q = SSD_CHUNK

    def const(shape):
        return pl.BlockSpec(shape, lambda bi, c: (0, 0))

    return pl.pallas_call(
        _ssd_kernel,
        grid=(b, s // q),
        in_specs=[
            pl.BlockSpec((1, q, d_inner), lambda bi, c: (bi, c, col_xs)),
            pl.BlockSpec((1, q, bc_w), lambda bi, c: (bi, c, col_bc)),
            pl.BlockSpec((1, q, d_inner), lambda bi, c: (bi, c, col_z)),
            pl.BlockSpec((1, q, LANES), lambda bi, c: (bi, c, 0)),
            const((CONV_WIDTH, d_inner)), const((CONV_WIDTH, bc_w)), const((1, d_inner)), const((1, bc_w)),
            const((1, LANES)), const((1, LANES)), const((1, d_inner)), const((1, d_inner)),
        ],
        out_specs=pl.BlockSpec((1, q, d_inner), lambda bi, c: (bi, c, 0)),
        out_shape=jax.ShapeDtypeStruct((b, s, d_inner), BF16),
        scratch_shapes=[pltpu.VMEM((n_heads * SSM_HEAD_DIM, D_STATE), F32),
                        pltpu.VMEM((SUBLANES, d_inner), F32),
                        pltpu.VMEM((SUBLANES, bc_w), F32)],
        compiler_params=_compiler_params(("parallel", "arbitrary")),
        name="ssd",
    )(proj3, proj3, proj3, dt3, conv_w_xs, conv_w_bc, conv_b_xs, conv_b_bc, dt_bias, a_log, d_skip, norm_w)


def _merge_kernel(o0_ref, l0_ref, o1_ref, l1_ref, o2_ref, l2_ref, ssm_ref, gate_ref, h_ref,
                  wa_ref, ws_ref, wo_ref, out_ref):
    l0, l1, l2 = l0_ref[...], l1_ref[...], l2_ref[...]
    mx = jnp.maximum(jnp.maximum(l0, l1), l2)
    e0, e1, e2 = jnp.exp(l0 - mx), jnp.exp(l1 - mx), jnp.exp(l2 - mx)
    attn = (e0 * o0_ref[...] + e1 * o1_ref[...] + e2 * o2_ref[...]) / (e0 + e1 + e2)
    a = jnp.dot(attn.astype(BF16), wa_ref[...], preferred_element_type=F32)
    s = jnp.dot(ssm_ref[...], ws_ref[...], preferred_element_type=F32)
    d = a.shape[1]
    gates = _sigmoid(gate_ref[...].astype(F32))
    merged = (gates[:, :d] * a + gates[:, d:] * s).astype(BF16)
    out_ref[...] = h_ref[...] + jnp.dot(merged, wo_ref[...], preferred_element_type=F32)


def _merge(attn_outs, ssm, proj, h, w_attn, w_ssm, w_out, *, col_gates, tm):
    t, d = h.shape
    d_inner = ssm.shape[1]

    def rows(width, col=0):
        return pl.BlockSpec((tm, width), lambda i: (i, col))

    def const(arr):
        return pl.BlockSpec(arr.shape, lambda i: (0, 0))

    flat = [a for pair in attn_outs for a in pair]
    return pl.pallas_call(
        _merge_kernel,
        grid=(t // tm,),
        in_specs=[rows(GROUP_WIDTH)] * len(flat) + [rows(d_inner), rows(2 * d, col_gates), rows(d),
                                                    const(w_attn), const(w_ssm), const(w_out)],
        out_specs=rows(d),
        out_shape=jax.ShapeDtypeStruct((t, d), F32),
        compiler_params=_compiler_params(("parallel",)),
        name="merge",
    )(*flat, ssm, proj, h, w_attn, w_ssm, w_out)


def _ffn_kernel(h_ref, nw_ref, wg_ref, wu_ref, wo_ref, fw_ref, out_ref, *, n_chunks, final_norm):
    h = h_ref[...]
    xn = _rmsnorm(h, nw_ref[...]).astype(BF16)
    ck = wo_ref.shape[0] // n_chunks
    acc = h
    for c in range(n_chunks):
        gate = jnp.dot(xn, wg_ref[:, c * ck:(c + 1) * ck], preferred_element_type=F32)
        up = jnp.dot(xn, wu_ref[:, c * ck:(c + 1) * ck], preferred_element_type=F32)
        act = (gate * _sigmoid(gate) * up).astype(BF16)
        acc = acc + jnp.dot(act, wo_ref[c * ck:(c + 1) * ck, :], preferred_element_type=F32)
    if final_norm:
        acc = _rmsnorm(acc, fw_ref[...])
    out_ref[...] = acc


def _ffn(h, norm_w, w_gate, w_up, w_down, final_w, *, tm, n_chunks, final_norm):
    t, d = h.shape

    def const(arr):
        return pl.BlockSpec(arr.shape, lambda i: (0, 0))

    return pl.pallas_call(
        functools.partial(_ffn_kernel, n_chunks=n_chunks, final_norm=final_norm),
        grid=(t // tm,),
        in_specs=[pl.BlockSpec((tm, d), lambda i: (i, 0)), const(norm_w), const(w_gate), const(w_up),
                  const(w_down), const(final_w)],
        out_specs=pl.BlockSpec((tm, d), lambda i: (i, 0)),
        out_shape=jax.ShapeDtypeStruct((t, d), F32),
        compiler_params=_compiler_params(("parallel",)),
        name="ffn",
    )(h, norm_w, w_gate, w_up, w_down, final_w)


def kernel(x, norm1_w, w_in, conv_w, conv_b, dt_bias, a_log, d_skip, ssm_norm_w, w_attn_branch, w_ssm_branch,
           w_out, norm2_w, w_ffn_in, w_ffn_out, rel_bias, final_norm_w):
    b, s, d = x.shape
    t = b * s
    depth = w_in.shape[0]
    d_inner = ssm_norm_w.shape[1]
    n_heads = dt_bias.shape[1]
    bc_w = 2 * N_SSM_GROUPS * D_STATE
    d_ff = w_ffn_out.shape[1]
    assert d_inner == N_SSM_GROUPS * HEADS_PER_SSM_GROUP * SSM_HEAD_DIM and n_heads <= LANES
    assert s % (DILATED_GROUPS[-1][1] * ATTN_BLOCK) == 0 and s % SSD_CHUNK == 0

    o_z = 3 * ATTN_WIDTH
    o_xbc = o_z + d_inner
    o_dt = o_xbc + d_inner + bc_w
    o_gate = o_dt + n_heads
    w_main = jnp.concatenate([w_in[:, :, o_z:o_xbc], w_in[:, :, o_gate:], w_in[:, :, o_xbc:o_dt],
                              w_in[:, :, :o_z]], axis=2).astype(BF16)
    w_dt = jnp.pad(w_in[:, :, o_dt:o_gate], ((0, 0), (0, 0), (0, LANES - n_heads))).astype(BF16)
    np_cols = w_main.shape[2]
    col_z = 0
    col_gates = d_inner // (2 * d)
    col_xs = (d_inner + 2 * d) // d_inner
    col_bc = (2 * d_inner + 2 * d) // bc_w
    col_q = (2 * d_inner + 2 * d + bc_w) // GROUP_WIDTH
    assert d_inner % (2 * d) == 0 and (d_inner + 2 * d) % d_inner == 0 and (2 * d_inner + 2 * d) % bc_w == 0
    assert (2 * d_inner + 2 * d + bc_w) % GROUP_WIDTH == 0

    pad_h = ((0, 0), (0, LANES - n_heads))
    dt_bias_p = jnp.pad(dt_bias, pad_h)
    a_log_p = jnp.pad(a_log, pad_h)
    d_skip_x = jnp.repeat(d_skip, SSM_HEAD_DIM, axis=1)
    w_attn16, w_ssm16, w_out16 = w_attn_branch.astype(BF16), w_ssm_branch.astype(BF16), w_out.astype(BF16)
    w_gate16, w_up16 = w_ffn_in[:, :, :d_ff].astype(BF16), w_ffn_in[:, :, d_ff:].astype(BF16)
    w_down16 = w_ffn_out.astype(BF16)
    bias_tables = _attention_bias_tables(rel_bias)

    h = x.reshape(t, d)
    for layer in range(depth):
        proj, dt = _in_proj(h, norm1_w[layer:layer + 1], w_main[layer], w_dt[layer], tm=1024, tn=512)

        attn_outs = []
        for g, (_, dil) in enumerate(DILATED_GROUPS):
            o_g, l_g = _attention(proj.reshape(b, s // dil, dil * np_cols), bias_tables[g], dilation=dil,
                                  col_q=col_q + g, col_k=col_q + N_DIL + g, col_v=col_q + 2 * N_DIL + g,
                                  tq=min(512, s // dil))
            attn_outs.append((o_g.reshape(t, GROUP_WIDTH), l_g.reshape(t, GROUP_WIDTH)))

        ssm = _ssd(proj.reshape(b, s, np_cols), dt.reshape(b, s, LANES),
                   conv_w[layer, :, :d_inner], conv_w[layer, :, d_inner:],
                   conv_b[layer:layer + 1, :d_inner], conv_b[layer:layer + 1, d_inner:],
                   dt_bias_p[layer:layer + 1], a_log_p[layer:layer + 1], d_skip_x[layer:layer + 1],
                   ssm_norm_w[layer:layer + 1], col_z=col_z, col_xs=col_xs, col_bc=col_bc)

        h = _merge(attn_outs, ssm.reshape(t, d_inner), proj, h, w_attn16[layer], w_ssm16[layer], w_out16[layer],
                   col_gates=col_gates, tm=256)
        h = _ffn(h, norm2_w[layer:layer + 1], w_gate16[layer], w_up16[layer], w_down16[layer],
                 final_norm_w[None, :], tm=256, n_chunks=2, final_norm=(layer == depth - 1))
    return h.reshape(b, s, d)
```

```python
import functools
import math

import jax
import jax.numpy as jnp
from jax import lax
from jax.experimental import pallas as pl
from jax.experimental.pallas import tpu as pltpu

HEAD_DIM = 64
DILATED_GROUPS = ((128, 1), (512, 4), (2048, 16))
N_DIL = len(DILATED_GROUPS)
HEADS_PER_GROUP = 8
GROUP_WIDTH = HEADS_PER_GROUP * HEAD_DIM
ATTN_WIDTH = N_DIL * GROUP_WIDTH
ATTN_BLOCK = 128
N_REL_BUCKETS = 32
REL_MAX_DISTANCE = 2048
SSM_HEAD_DIM = 64
N_SSM_GROUPS = 4
HEADS_PER_SSM_GROUP = 8
D_STATE = 128
CONV_WIDTH = 4
SSD_CHUNK = 128
EPS = 1e-6

LANES = 128
SUBLANES = 8
VMEM_LIMIT_BYTES = 48 * 1024 * 1024

BF16 = jnp.bfloat16
F32 = jnp.float32
NT_DIMS = (((1,), (1,)), ((), ()))


def _compiler_params(semantics):
    return pltpu.CompilerParams(dimension_semantics=semantics, vmem_limit_bytes=VMEM_LIMIT_BYTES)


def _rmsnorm(x, w):
    return x * lax.rsqrt(jnp.mean(x * x, axis=-1, keepdims=True) + EPS) * w


def _sigmoid(x):
    return 1.0 / (1.0 + jnp.exp(-x))


PERM_BLOCK = 256


def _in_proj_kernel(x_ref, nw_ref, w_ref, *rest, dilation, with_dt):
    if with_dt:
        wdt_ref, out_ref, dt_ref, xn_ref = rest
    else:
        out_ref, xn_ref = rest
    tm = x_ref.shape[1]

    @pl.when(pl.program_id(2) == 0)
    def _():
        xn = _rmsnorm(x_ref[0], nw_ref[...]).astype(BF16)
        if with_dt:
            dt_ref[0] = jnp.dot(xn, wdt_ref[...], preferred_element_type=F32)
        if dilation == 1:
            xn_ref[...] = xn
        else:
            per = PERM_BLOCK // dilation
            i = lax.broadcasted_iota(jnp.int32, (PERM_BLOCK, PERM_BLOCK), 0)
            k = lax.broadcasted_iota(jnp.int32, (PERM_BLOCK, PERM_BLOCK), 1)
            src = (i & (per - 1)) * dilation + (i >> (per.bit_length() - 1))
            perm = jnp.where(k == src, 1.0, 0.0).astype(BF16)
            for u in range(tm // PERM_BLOCK):
                y = jnp.dot(perm, xn[u * PERM_BLOCK:(u + 1) * PERM_BLOCK],
                            preferred_element_type=F32).astype(BF16)
                for r in range(dilation):
                    dst = r * (tm // dilation) + u * per
                    xn_ref[dst:dst + per, :] = y[r * per:(r + 1) * per]

    res = jnp.dot(xn_ref[...], w_ref[...], preferred_element_type=F32).astype(out_ref.dtype)
    out_ref[0] = res.reshape(dilation, tm // dilation, res.shape[1])


def _in_proj(h3, norm_w, w, w_dt=None, *, dilation, tm, tn):
    b, s, d = h3.shape
    n = w.shape[1]
    with_dt = w_dt is not None
    in_specs = [
        pl.BlockSpec((1, tm, d), lambda bi, i, j: (bi, i, 0)),
        pl.BlockSpec((1, d), lambda bi, i, j: (0, 0)),
        pl.BlockSpec((d, tn), lambda bi, i, j: (0, j)),
    ]
    out_specs = [pl.BlockSpec((1, dilation, tm // dilation, tn), lambda bi, i, j: (bi, 0, i, j))]
    out_shape = [jax.ShapeDtypeStruct((b, dilation, s // dilation, n), BF16)]
    args = [h3, norm_w, w]
    if with_dt:
        in_specs.append(pl.BlockSpec((d, LANES), lambda bi, i, j: (0, 0)))
        out_specs.append(pl.BlockSpec((1, tm, LANES), lambda bi, i, j: (bi, i, 0)))
        out_shape.append(jax.ShapeDtypeStruct((b, s, LANES), F32))
        args.append(w_dt)
    return pl.pallas_call(
        functools.partial(_in_proj_kernel, dilation=dilation, with_dt=with_dt),
        grid=(b, s // tm, n // tn),
        in_specs=in_specs,
        out_specs=out_specs,
        out_shape=out_shape,
        scratch_shapes=[pltpu.VMEM((tm, d), BF16)],
        compiler_params=_compiler_params(("parallel", "parallel", "arbitrary")),
        name=f"in_proj_d{dilation}",
    )(*args)


def _attention_kernel(q_ref, kp_ref, kc_ref, vp_ref, vc_ref, bias_ref, o_ref, l_ref, kext_ref, vext_ref, *, tq):
    n = pl.program_id(2)
    kext_ref[0:ATTN_BLOCK] = kp_ref[0, 0]
    kext_ref[ATTN_BLOCK:] = kc_ref[0, 0]
    vext_ref[0:ATTN_BLOCK] = vp_ref[0, 0]
    vext_ref[ATTN_BLOCK:] = vc_ref[0, 0]
    low_half = lax.broadcasted_iota(jnp.int32, (1, LANES), 1) < HEAD_DIM

    def block(s, carry):
        r0 = pl.multiple_of(s * ATTN_BLOCK, ATTN_BLOCK)
        qs = q_ref[0, 0, pl.ds(r0, ATTN_BLOCK), :] * (HEAD_DIM ** -0.5)
        ks = kext_ref[pl.ds(r0, 2 * ATTN_BLOCK), :]
        vs = vext_ref[pl.ds(r0, 2 * ATTN_BLOCK), :]
        bidx = jnp.where(jnp.logical_and(n == 0, s == 0), 0, 1)
        for p in range(GROUP_WIDTH // LANES):
            cols = slice(p * LANES, (p + 1) * LANES)
            qp, kp, vp = qs[:, cols], ks[:, cols], vs[:, cols]
            outs, lses = [], []
            for e in range(2):
                qe = jnp.where(low_half if e == 0 else jnp.logical_not(low_half), qp, jnp.zeros_like(qp))
                sc = lax.dot_general(qe, kp, NT_DIMS, preferred_element_type=F32) + bias_ref[bidx, 2 * p + e]
                m = jnp.max(sc, axis=1, keepdims=True)
                pe = jnp.exp(sc - m)
                den = jnp.sum(pe, axis=1, keepdims=True)
                num = jnp.dot(pe.astype(BF16), vp, preferred_element_type=F32)
                outs.append(num / den)
                lses.append(jnp.broadcast_to(m + jnp.log(den), (ATTN_BLOCK, LANES)))
            o_ref[0, 0, pl.ds(r0, ATTN_BLOCK), cols] = jnp.where(low_half, outs[0], outs[1])
            l_ref[0, 0, pl.ds(r0, ATTN_BLOCK), cols] = jnp.where(low_half, lses[0], lses[1])
        return carry

    lax.fori_loop(0, tq // ATTN_BLOCK, block, 0)


def _attention(qkv, bias_g, *, col_q, col_k, col_v, tq):
    b, dilation, seg, _ = qkv.shape
    blocks_per_tile = tq // ATTN_BLOCK

    def cur(col):
        return pl.BlockSpec((1, 1, tq, GROUP_WIDTH), lambda bi, r, n: (bi, r, n, col))

    def prev(col):
        return pl.BlockSpec((1, 1, ATTN_BLOCK, GROUP_WIDTH),
                            lambda bi, r, n: (bi, r, jnp.maximum(n * blocks_per_tile - 1, 0), col))

    out_spec = pl.BlockSpec((1, 1, tq, GROUP_WIDTH), lambda bi, r, n: (bi, r, n, 0))
    out_sds = jax.ShapeDtypeStruct((b, dilation, seg, GROUP_WIDTH), F32)
    return pl.pallas_call(
        functools.partial(_attention_kernel, tq=tq),
        grid=(b, dilation, seg // tq),
        in_specs=[cur(col_q), prev(col_k), cur(col_k), prev(col_v), cur(col_v),
                  pl.BlockSpec(bias_g.shape, lambda bi, r, n: (0, 0, 0, 0))],
        out_specs=[out_spec, out_spec],
        out_shape=[out_sds, out_sds],
        scratch_shapes=[pltpu.VMEM((tq + ATTN_BLOCK, GROUP_WIDTH), BF16),
                        pltpu.VMEM((tq + ATTN_BLOCK, GROUP_WIDTH), BF16)],
        compiler_params=_compiler_params(("parallel", "parallel", "arbitrary")),
        name=f"attention_d{dilation}",
    )(qkv, qkv, qkv, qkv, qkv, bias_g)


def _t5_causal_bucket(dist):
    max_exact = N_REL_BUCKETS // 2
    d_f = jnp.maximum(dist, 1).astype(F32)
    large = max_exact + (jnp.log(d_f / max_exact) / math.log(REL_MAX_DISTANCE / max_exact)
                         * (N_REL_BUCKETS - max_exact)).astype(jnp.int32)
    large = jnp.minimum(large, N_REL_BUCKETS - 1)
    return jnp.where(dist < max_exact, dist, large)


def _attention_bias_tables(rel_bias):
    qi = jnp.arange(ATTN_BLOCK)[:, None]
    kj = jnp.arange(2 * ATTN_BLOCK)[None, :]
    steps = qi + ATTN_BLOCK - kj
    tables = []
    for g, (window, dil) in enumerate(DILATED_GROUPS):
        n_steps = window // dil
        bucket = _t5_causal_bucket(jnp.clip(steps, 0, n_steps) * dil)
        rel_g = rel_bias[:, g * HEADS_PER_GROUP:(g + 1) * HEADS_PER_GROUP]
        bias = jnp.transpose(rel_g[bucket], (2, 0, 1)).astype(F32)
        valid = (steps >= 0) & (steps <= n_steps)
        rest = jnp.where(valid[None], bias, -jnp.inf)
        first = jnp.where((valid & (kj >= ATTN_BLOCK))[None], bias, -jnp.inf)
        tables.append(jnp.stack([first, rest]))
    return jnp.stack(tables)


def _conv_silu(x, halo_ref, w_ref, b_ref):
    prev = halo_ref[...]
    q = x.shape[0]
    halo_ref[...] = x[q - SUBLANES:]
    row = lax.broadcasted_iota(jnp.int32, (SUBLANES, 1), 0)
    acc = x * w_ref[CONV_WIDTH - 1:CONV_WIDTH, :] + b_ref[...]
    for s in range(1, CONV_WIDTH):
        rolled = pltpu.roll(x, s, axis=0)
        top = jnp.where(row < s, pltpu.roll(prev, s, axis=0), rolled[:SUBLANES])
        shifted = jnp.concatenate([top, rolled[SUBLANES:]], axis=0)
        acc = acc + shifted * w_ref[CONV_WIDTH - 1 - s:CONV_WIDTH - s, :]
    return acc * _sigmoid(acc)


def _ssd_kernel(xs_ref, bc_ref, z_ref, dt_ref, cwx_ref, cwb_ref, cbx_ref, cbb_ref, dtb_ref, alog_ref,
                dskip_ref, nw_ref, out_ref, state_ref, hx_ref, hb_ref):
    q = SSD_CHUNK
    gw = HEADS_PER_SSM_GROUP * SSM_HEAD_DIM

    @pl.when(pl.program_id(1) == 0)
    def _():
        state_ref[...] = jnp.zeros_like(state_ref)
        hx_ref[...] = jnp.zeros_like(hx_ref)
        hb_ref[...] = jnp.zeros_like(hb_ref)

    x_dt = dt_ref[0] + dtb_ref[...]
    dt = jnp.maximum(x_dt, 0.0) + jnp.log1p(jnp.exp(-jnp.abs(x_dt)))
    d_a = dt * (-jnp.exp(alog_ref[...]))
    ri = lax.broadcasted_iota(jnp.int32, (q, q), 0)
    ci = lax.broadcasted_iota(jnp.int32, (q, q), 1)
    causal = ri >= ci
    tril = jnp.where(causal, 1.0, 0.0).astype(BF16)
    d1 = d_a.astype(BF16)
    r1 = d_a - d1.astype(F32)
    d2 = r1.astype(BF16)
    d3 = (r1 - d2.astype(F32)).astype(BF16)
    la = (jnp.dot(tril, d1, preferred_element_type=F32) + jnp.dot(tril, d2, preferred_element_type=F32)
          + jnp.dot(tril, d3, preferred_element_type=F32))
    la_t = la.T
    dt_t = dt.T
    last = jnp.broadcast_to(la_t[:, q - 1:q], (q, q))
    w_t = jnp.exp(last - la_t) * dt_t
    state_decay = jnp.exp(last)

    xs = _conv_silu(xs_ref[0, 0].astype(F32), hx_ref, cwx_ref, cbx_ref)
    bc = _conv_silu(bc_ref[0, 0].astype(F32), hb_ref, cwb_ref, cbb_ref)

    for g in range(N_SSM_GROUPS):
        bm = bc[:, g * D_STATE:(g + 1) * D_STATE]
        cm = bc[:, (N_SSM_GROUPS + g) * D_STATE:(N_SSM_GROUPS + g + 1) * D_STATE]
        bm16 = bm.astype(BF16)
        cb = lax.dot_general(cm.astype(BF16), bm16, NT_DIMS, preferred_element_type=F32)
        xs_g = xs[:, g * gw:(g + 1) * gw]
        xs_t = xs_g.T
        y_t = []
        for h in range(HEADS_PER_SSM_GROUP):
            hh = g * HEADS_PER_SSM_GROUP + h
            rows = slice(hh * SSM_HEAD_DIM, (hh + 1) * SSM_HEAD_DIM)
            la_i = jnp.broadcast_to(la[:, hh:hh + 1], (q, q))
            decay = jnp.exp(jnp.where(causal, la_i - la_t[hh:hh + 1, :], -jnp.inf))
            lhs = jnp.concatenate([(cb * decay).astype(BF16), (cm * jnp.exp(la_i)).astype(BF16)], axis=1)
            x_h = xs_t[h * SSM_HEAD_DIM:(h + 1) * SSM_HEAD_DIM]
            state = state_ref[rows, :]
            rhs_t = jnp.concatenate([(x_h * dt_t[hh:hh + 1, :]).astype(BF16), state.astype(BF16)], axis=1)
            y_t.append(lax.dot_general(rhs_t, lhs, NT_DIMS, preferred_element_type=F32))
            upd = jnp.dot((x_h * w_t[hh:hh + 1, :]).astype(BF16), bm16, preferred_element_type=F32)
            state_ref[rows, :] = state * state_decay[hh:hh + 1, :] + upd
        y = jnp.concatenate(y_t, axis=0).T
        cols = slice(g * gw, (g + 1) * gw)
        y = y + xs_g * dskip_ref[:, cols]
        z = z_ref[0, 0, :, cols].astype(F32)
        yg = y * (z * _sigmoid(z))
        yg = yg * lax.rsqrt(jnp.mean(yg * yg, axis=-1, keepdims=True) + EPS)
        out_ref[0, :, cols] = (yg * nw_ref[:, cols]).astype(out_ref.dtype)


def _ssd(main, dt3, conv_w_xs, conv_w_bc, conv_b_xs, conv_b_bc, dt_bias, a_log, d_skip, norm_w,
         *, col_z, col_xs, col_bc):
    b, _, s, _ = main.shape
    d_inner = conv_w_xs.shape[1]
    bc_w = conv_w_bc.shape[1]
    n_heads = N_SSM_GROUPS * HEADS_PER_SSM_GROUP
    q = SSD_CHUNK

    def const(shape):
        return pl.BlockSpec(shape, lambda bi, c: (0, 0))

    return pl.pallas_call(
        _ssd_kernel,
        grid=(b, s // q),
        in_specs=[
            pl.BlockSpec((1, 1, q, d_inner), lambda bi, c: (bi, 0, c, col_xs)),
            pl.BlockSpec((1, 1, q, bc_w), lambda bi, c: (bi, 0, c, col_bc)),
            pl.BlockSpec((1, 1, q, d_inner), lambda bi, c: (bi, 0, c, col_z)),
            pl.BlockSpec((1, q, LANES), lambda bi, c: (bi, c, 0)),
            const((CONV_WIDTH, d_inner)), const((CONV_WIDTH, bc_w)), const((1, d_inner)), const((1, bc_w)),
            const((1, LANES)), const((1, LANES)), const((1, d_inner)), const((1, d_inner)),
        ],
        out_specs=pl.BlockSpec((1, q, d_inner), lambda bi, c: (bi, c, 0)),
        out_shape=jax.ShapeDtypeStruct((b, s, d_inner), BF16),
        scratch_shapes=[pltpu.VMEM((n_heads * SSM_HEAD_DIM, D_STATE), F32),
                        pltpu.VMEM((SUBLANES, d_inner), F32),
                        pltpu.VMEM((SUBLANES, bc_w), F32)],
        compiler_params=_compiler_params(("parallel", "arbitrary")),
        name="ssd",
    )(main, main, main, dt3, conv_w_xs, conv_w_bc, conv_b_xs, conv_b_bc, dt_bias, a_log, d_skip, norm_w)


def _merge_kernel(*refs):
    attn_refs = refs[:2 * N_DIL]
    ssm_ref, gate_ref, h_ref, wa_ref, ws_ref, wo_ref, out_ref, nat_ref = refs[2 * N_DIL:]
    tm = h_ref.shape[1]
    n_slabs = GROUP_WIDTH // LANES
    for k, src in enumerate(attn_refs):
        dil = src.shape[1]
        for r in range(dil if dil > 1 else 0):
            for c in range(n_slabs):
                nat_ref[k, c, pl.ds(r, tm // dil, stride=dil), :] = src[0, r, :, c * LANES:(c + 1) * LANES]

    def natural(k, c):
        src = attn_refs[k]
        return src[0, 0, :, c * LANES:(c + 1) * LANES] if src.shape[1] == 1 else nat_ref[k, c]

    slabs = []
    for c in range(n_slabs):
        o = [natural(2 * g, c) for g in range(N_DIL)]
        l = [natural(2 * g + 1, c) for g in range(N_DIL)]
        mx = functools.reduce(jnp.maximum, l)
        e = [jnp.exp(lg - mx) for lg in l]
        num = functools.reduce(jnp.add, [eg * og for eg, og in zip(e, o)])
        slabs.append((num / functools.reduce(jnp.add, e)).astype(BF16))
    attn = jnp.concatenate(slabs, axis=1)
    a = jnp.dot(attn, wa_ref[...], preferred_element_type=F32)
    s = jnp.dot(ssm_ref[0], ws_ref[...], preferred_element_type=F32)
    d = a.shape[1]
    gates = _sigmoid(gate_ref[0, 0].astype(F32))
    merged = (gates[:, :d] * a + gates[:, d:] * s).astype(BF16)
    out_ref[0] = h_ref[0] + jnp.dot(merged, wo_ref[...], preferred_element_type=F32)


def _merge(attn_outs, ssm, main, h3, w_attn, w_ssm, w_out, *, col_gates, tm):
    b, s, d = h3.shape
    d_inner = ssm.shape[2]

    def rows(width):
        return pl.BlockSpec((1, tm, width), lambda bi, i: (bi, i, 0))

    def residues(arr):
        dil = arr.shape[1]
        return pl.BlockSpec((1, dil, tm // dil, GROUP_WIDTH), lambda bi, i: (bi, 0, i, 0))

    def const(arr):
        return pl.BlockSpec(arr.shape, lambda bi, i: (0, 0))

    flat = [a for pair in attn_outs for a in pair]
    return pl.pallas_call(
        _merge_kernel,
        grid=(b, s // tm),
        in_specs=[residues(a) for a in flat]
        + [rows(d_inner), pl.BlockSpec((1, 1, tm, 2 * d), lambda bi, i: (bi, 0, i, col_gates)), rows(d),
           const(w_attn), const(w_ssm), const(w_out)],
        out_specs=rows(d),
        out_shape=jax.ShapeDtypeStruct((b, s, d), F32),
        scratch_shapes=[pltpu.VMEM((len(flat), GROUP_WIDTH // LANES, tm, LANES), F32)],
        compiler_params=_compiler_params(("parallel", "parallel")),
        name="merge",
    )(*flat, ssm, main, h3, w_attn, w_ssm, w_out)


def _ffn_kernel(h_ref, nw_ref, wg_ref, wu_ref, wo_ref, fw_ref, out_ref, *, n_chunks, final_norm):
    h = h_ref[...]
    xn = _rmsnorm(h, nw_ref[...]).astype(BF16)
    ck = wo_ref.shape[0] // n_chunks
    acc = h
    for c in range(n_chunks):
        gate = jnp.dot(xn, wg_ref[:, c * ck:(c + 1) * ck], preferred_element_type=F32)
        up = jnp.dot(xn, wu_ref[:, c * ck:(c + 1) * ck], preferred_element_type=F32)
        act = (gate * _sigmoid(gate) * up).astype(BF16)
        acc = acc + jnp.dot(act, wo_ref[c * ck:(c + 1) * ck, :], preferred_element_type=F32)
    if final_norm:
        acc = _rmsnorm(acc, fw_ref[...])
    out_ref[...] = acc


def _ffn(h, norm_w, w_gate, w_up, w_down, final_w, *, tm, n_chunks, final_norm):
    t, d = h.shape

    def const(arr):
        return pl.BlockSpec(arr.shape, lambda i: (0, 0))

    return pl.pallas_call(
        functools.partial(_ffn_kernel, n_chunks=n_chunks, final_norm=final_norm),
        grid=(t // tm,),
        in_specs=[pl.BlockSpec((tm, d), lambda i: (i, 0)), const(norm_w), const(w_gate), const(w_up),
                  const(w_down), const(final_w)],
        out_specs=pl.BlockSpec((tm, d), lambda i: (i, 0)),
        out_shape=jax.ShapeDtypeStruct((t, d), F32),
        compiler_params=_compiler_params(("parallel",)),
        name="ffn",
    )(h, norm_w, w_gate, w_up, w_down, final_w)


def kernel(x, norm1_w, w_in, conv_w, conv_b, dt_bias, a_log, d_skip, ssm_norm_w, w_attn_branch, w_ssm_branch,
           w_out, norm2_w, w_ffn_in, w_ffn_out, rel_bias, final_norm_w):
    b, s, d = x.shape
    t = b * s
    depth = w_in.shape[0]
    d_inner = ssm_norm_w.shape[1]
    n_heads = dt_bias.shape[1]
    bc_w = 2 * N_SSM_GROUPS * D_STATE
    d_ff = w_ffn_out.shape[1]
    assert d_inner == N_SSM_GROUPS * HEADS_PER_SSM_GROUP * SSM_HEAD_DIM and n_heads <= LANES
    assert s % (DILATED_GROUPS[-1][1] * ATTN_BLOCK) == 0 and s % SSD_CHUNK == 0

    o_z = 3 * ATTN_WIDTH
    o_xbc = o_z + d_inner
    o_dt = o_xbc + d_inner + bc_w
    o_gate = o_dt + n_heads

    def qkv_cols(g):
        return [w_in[:, :, part * ATTN_WIDTH + g * GROUP_WIDTH:part * ATTN_WIDTH + (g + 1) * GROUP_WIDTH]
                for part in range(3)]

    w_main = jnp.concatenate([w_in[:, :, o_z:o_xbc], w_in[:, :, o_gate:], w_in[:, :, o_xbc:o_dt]] + qkv_cols(0),
                             axis=2).astype(BF16)
    w_qkv = [None] + [jnp.concatenate(qkv_cols(g), axis=2).astype(BF16) for g in range(1, N_DIL)]
    w_dt = jnp.pad(w_in[:, :, o_dt:o_gate], ((0, 0), (0, 0), (0, LANES - n_heads))).astype(BF16)
    col_z = 0
    col_gates = d_inner // (2 * d)
    col_xs = (d_inner + 2 * d) // d_inner
    col_bc = (2 * d_inner + 2 * d) // bc_w
    col_q = (2 * d_inner + 2 * d + bc_w) // GROUP_WIDTH
    assert d_inner % (2 * d) == 0 and (d_inner + 2 * d) % d_inner == 0 and (2 * d_inner + 2 * d) % bc_w == 0
    assert (2 * d_inner + 2 * d + bc_w) % GROUP_WIDTH == 0

    pad_h = ((0, 0), (0, LANES - n_heads))
    dt_bias_p = jnp.pad(dt_bias, pad_h)
    a_log_p = jnp.pad(a_log, pad_h)
    d_skip_x = jnp.repeat(d_skip, SSM_HEAD_DIM, axis=1)
    w_attn16, w_ssm16, w_out16 = w_attn_branch.astype(BF16), w_ssm_branch.astype(BF16), w_out.astype(BF16)
    w_gate16, w_up16 = w_ffn_in[:, :, :d_ff].astype(BF16), w_ffn_in[:, :, d_ff:].astype(BF16)
    w_down16 = w_ffn_out.astype(BF16)
    bias_tables = _attention_bias_tables(rel_bias)

    h = x
    for layer in range(depth):
        nw1 = norm1_w[layer:layer + 1]
        main, dt = _in_proj(h, nw1, w_main[layer], w_dt[layer], dilation=1, tm=1024, tn=512)

        attn_outs = []
        for g, (_, dil) in enumerate(DILATED_GROUPS):
            if dil == 1:
                qkv, cq = main, col_q
            else:
                (qkv,), cq = _in_proj(h, nw1, w_qkv[g][layer], dilation=dil, tm=1024, tn=GROUP_WIDTH), 0
            attn_outs.append(_attention(qkv, bias_tables[g], col_q=cq, col_k=cq + 1, col_v=cq + 2,
                                        tq=min(512, s // dil)))

        ssm = _ssd(main, dt, conv_w[layer, :, :d_inner], conv_w[layer, :, d_inner:],
                   conv_b[layer:layer + 1, :d_inner], conv_b[layer:layer + 1, d_inner:],
                   dt_bias_p[layer:layer + 1], a_log_p[layer:layer + 1], d_skip_x[layer:layer + 1],
                   ssm_norm_w[layer:layer + 1], col_z=col_z, col_xs=col_xs, col_bc=col_bc)

        h = _merge(attn_outs, ssm, main, h, w_attn16[layer], w_ssm16[layer], w_out16[layer],
                   col_gates=col_gates, tm=256)
        h = _ffn(h.reshape(t, d), norm2_w[layer:layer + 1], w_gate16[layer], w_up16[layer], w_down16[layer],
                 final_norm_w[None, :], tm=256, n_chunks=2, final_norm=(layer == depth - 1)).reshape(b, s, d)
    return h
```

```python
import functools
import math

import jax
import jax.numpy as jnp
from jax import lax
from jax.experimental import pallas as pl
from jax.experimental.pallas import tpu as pltpu

HEAD_DIM = 64
DILATED_GROUPS = ((128, 1), (512, 4), (2048, 16))
N_DIL = len(DILATED_GROUPS)
HEADS_PER_GROUP = 8
GROUP_WIDTH = HEADS_PER_GROUP * HEAD_DIM
ATTN_WIDTH = N_DIL * GROUP_WIDTH
ATTN_BLOCK = 128
N_REL_BUCKETS = 32
REL_MAX_DISTANCE = 2048
SSM_HEAD_DIM = 64
N_SSM_GROUPS = 4
HEADS_PER_SSM_GROUP = 8
D_STATE = 128
CONV_WIDTH = 4
SSD_CHUNK = 128
EPS = 1e-6

LANES = 128
SUBLANES = 8
VMEM_LIMIT_BYTES = 56 * 1024 * 1024

BF16 = jnp.bfloat16
F32 = jnp.float32
NT_DIMS = (((1,), (1,)), ((), ()))


def _compiler_params(semantics):
    return pltpu.CompilerParams(dimension_semantics=semantics, vmem_limit_bytes=VMEM_LIMIT_BYTES)


def _rmsnorm(x, w):
    return x * lax.rsqrt(jnp.mean(x * x, axis=-1, keepdims=True) + EPS) * w


def _sigmoid(x):
    return 1.0 / (1.0 + jnp.exp(-x))


PERM_BLOCK = 256


def _in_proj_kernel(x_ref, nw_ref, w_ref, *rest, dilation, with_dt, chunk):
    if with_dt:
        wdt_ref, out_ref, dt_ref, xn_ref = rest
    else:
        out_ref, xn_ref = rest
    tm = x_ref.shape[1]
    n = w_ref.shape[1]
    xn = _rmsnorm(x_ref[0], nw_ref[...]).astype(BF16)
    if with_dt:
        dt_ref[0] = jnp.dot(xn, wdt_ref[...], preferred_element_type=F32)
    if dilation == 1:
        xn_ref[...] = xn
    else:
        per = PERM_BLOCK // dilation
        i = lax.broadcasted_iota(jnp.int32, (PERM_BLOCK, PERM_BLOCK), 0)
        k = lax.broadcasted_iota(jnp.int32, (PERM_BLOCK, PERM_BLOCK), 1)
        src = (i & (per - 1)) * dilation + (i >> (per.bit_length() - 1))
        perm = jnp.where(k == src, 1.0, 0.0).astype(BF16)
        for u in range(tm // PERM_BLOCK):
            y = jnp.dot(perm, xn[u * PERM_BLOCK:(u + 1) * PERM_BLOCK], preferred_element_type=F32).astype(BF16)
            for r in range(dilation):
                dst = r * (tm // dilation) + u * per
                xn_ref[dst:dst + per, :] = y[r * per:(r + 1) * per]
    for c0 in range(0, n, chunk):
        c1 = min(c0 + chunk, n)
        res = jnp.dot(xn_ref[...], w_ref[:, c0:c1], preferred_element_type=F32).astype(out_ref.dtype)
        out_ref[0, :, :, c0:c1] = res.reshape(dilation, tm // dilation, c1 - c0)


def _resident(shape):
    zeros = (0,) * len(shape)
    return pl.BlockSpec(shape, lambda *_: zeros, pipeline_mode=pl.Buffered(1))


def _in_proj(h3, norm_w, w, w_dt=None, *, dilation, tm, chunk):
    b, s, d = h3.shape
    n = w.shape[1]
    with_dt = w_dt is not None
    in_specs = [pl.BlockSpec((1, tm, d), lambda bi, i: (bi, i, 0)), _resident((1, d)), _resident((d, n))]
    out_specs = [pl.BlockSpec((1, dilation, tm // dilation, n), lambda bi, i: (bi, 0, i, 0))]
    out_shape = [jax.ShapeDtypeStruct((b, dilation, s // dilation, n), BF16)]
    args = [h3, norm_w, w]
    if with_dt:
        in_specs.append(_resident((d, LANES)))
        out_specs.append(pl.BlockSpec((1, tm, LANES), lambda bi, i: (bi, i, 0)))
        out_shape.append(jax.ShapeDtypeStruct((b, s, LANES), F32))
        args.append(w_dt)
    return pl.pallas_call(
        functools.partial(_in_proj_kernel, dilation=dilation, with_dt=with_dt, chunk=chunk),
        grid=(b, s // tm),
        in_specs=in_specs,
        out_specs=out_specs,
        out_shape=out_shape,
        scratch_shapes=[pltpu.VMEM((tm, d), BF16)],
        compiler_params=_compiler_params(("parallel", "parallel")),
        name=f"in_proj_d{dilation}",
    )(*args)


def _attention_kernel(q_ref, kp_ref, kc_ref, vp_ref, vc_ref, bias_ref, o_ref, l_ref, kext_ref, vext_ref, *, tq):
    n = pl.program_id(2)
    kext_ref[0:ATTN_BLOCK] = kp_ref[0, 0]
    kext_ref[ATTN_BLOCK:] = kc_ref[0, 0]
    vext_ref[0:ATTN_BLOCK] = vp_ref[0, 0]
    vext_ref[ATTN_BLOCK:] = vc_ref[0, 0]
    low_half = lax.broadcasted_iota(jnp.int32, (1, LANES), 1) < HEAD_DIM

    for s in range(tq // ATTN_BLOCK):
        r0 = s * ATTN_BLOCK
        qs = q_ref[0, 0, pl.ds(r0, ATTN_BLOCK), :] * (HEAD_DIM ** -0.5)
        ks = kext_ref[pl.ds(r0, 2 * ATTN_BLOCK), :]
        vs = vext_ref[pl.ds(r0, 2 * ATTN_BLOCK), :]
        bidx = jnp.where(n == 0, 0, 1) if s == 0 else 1
        for p in range(GROUP_WIDTH // LANES):
            cols = slice(p * LANES, (p + 1) * LANES)
            qp, kp, vp = qs[:, cols], ks[:, cols], vs[:, cols]
            outs, lses = [], []
            for e in range(2):
                qe = jnp.where(low_half if e == 0 else jnp.logical_not(low_half), qp, jnp.zeros_like(qp))
                sc = lax.dot_general(qe, kp, NT_DIMS, preferred_element_type=F32) + bias_ref[bidx, 2 * p + e]
                m = jnp.max(sc, axis=1, keepdims=True)
                pe = jnp.exp(sc - m)
                den = jnp.sum(pe, axis=1, keepdims=True)
                num = jnp.dot(pe.astype(BF16), vp, preferred_element_type=F32)
                outs.append(num / den)
                lses.append(jnp.broadcast_to(m + jnp.log(den), (ATTN_BLOCK, LANES)))
            o_ref[0, 0, pl.ds(r0, ATTN_BLOCK), cols] = jnp.where(low_half, outs[0], outs[1])
            l_ref[0, 0, pl.ds(r0, ATTN_BLOCK), cols] = jnp.where(low_half, lses[0], lses[1])


def _attention(qkv, bias_g, *, col_q, col_k, col_v, tq):
    b, dilation, seg, _ = qkv.shape
    blocks_per_tile = tq // ATTN_BLOCK

    def cur(col):
        return pl.BlockSpec((1, 1, tq, GROUP_WIDTH), lambda bi, r, n: (bi, r, n, col))

    def prev(col):
        return pl.BlockSpec((1, 1, ATTN_BLOCK, GROUP_WIDTH),
                            lambda bi, r, n: (bi, r, jnp.maximum(n * blocks_per_tile - 1, 0), col))

    out_spec = pl.BlockSpec((1, 1, tq, GROUP_WIDTH), lambda bi, r, n: (bi, r, n, 0))
    out_sds = jax.ShapeDtypeStruct((b, dilation, seg, GROUP_WIDTH), F32)
    return pl.pallas_call(
        functools.partial(_attention_kernel, tq=tq),
        grid=(b, dilation, seg // tq),
        in_specs=[cur(col_q), prev(col_k), cur(col_k), prev(col_v), cur(col_v),
                  pl.BlockSpec(bias_g.shape, lambda bi, r, n: (0, 0, 0, 0))],
        out_specs=[out_spec, out_spec],
        out_shape=[out_sds, out_sds],
        scratch_shapes=[pltpu.VMEM((tq + ATTN_BLOCK, GROUP_WIDTH), BF16),
                        pltpu.VMEM((tq + ATTN_BLOCK, GROUP_WIDTH), BF16)],
        compiler_params=_compiler_params(("parallel", "parallel", "arbitrary")),
        name=f"attention_d{dilation}",
    )(qkv, qkv, qkv, qkv, qkv, bias_g)


def _t5_causal_bucket(dist):
    max_exact = N_REL_BUCKETS // 2
    d_f = jnp.maximum(dist, 1).astype(F32)
    large = max_exact + (jnp.log(d_f / max_exact) / math.log(REL_MAX_DISTANCE / max_exact)
                         * (N_REL_BUCKETS - max_exact)).astype(jnp.int32)
    large = jnp.minimum(large, N_REL_BUCKETS - 1)
    return jnp.where(dist < max_exact, dist, large)


def _attention_bias_tables(rel_bias):
    q, q2 = ATTN_BLOCK, 2 * ATTN_BLOCK
    steps = jnp.arange(q2) - (q - 1)
    in_prev = (jnp.arange(q2) < q)[None, None, :]
    tables = []
    for g, (window, dil) in enumerate(DILATED_GROUPS):
        n_steps = window // dil
        assert n_steps <= q
        rel_g = rel_bias[:, g * HEADS_PER_GROUP:(g + 1) * HEADS_PER_GROUP].astype(F32)
        vals = rel_g[_t5_causal_bucket(jnp.clip(steps, 0, n_steps) * dil)]
        vec = jnp.where(((steps >= 0) & (steps <= n_steps))[:, None], vals, -jnp.inf).T
        skew = jnp.tile(vec, (1, q + 1))[:, :q * (q2 + 1)].reshape(HEADS_PER_GROUP, q, q2 + 1)[:, :, :q2]
        rest = skew[:, :, ::-1]
        first = jnp.where(in_prev, -jnp.inf, rest)
        tables.append(jnp.stack([first, rest]))
    return jnp.stack(tables)


def _conv_silu(x, halo_ref, w_ref, b_ref):
    prev = halo_ref[...]
    q = x.shape[0]
    halo_ref[...] = x[q - SUBLANES:]
    row = lax.broadcasted_iota(jnp.int32, (SUBLANES, 1), 0)
    acc = x * w_ref[CONV_WIDTH - 1:CONV_WIDTH, :] + b_ref[...]
    for s in range(1, CONV_WIDTH):
        rolled = pltpu.roll(x, s, axis=0)
        top = jnp.where(row < s, pltpu.roll(prev, s, axis=0), rolled[:SUBLANES])
        shifted = jnp.concatenate([top, rolled[SUBLANES:]], axis=0)
        acc = acc + shifted * w_ref[CONV_WIDTH - 1 - s:CONV_WIDTH - s, :]
    return acc * _sigmoid(acc)


def _ssd_kernel(xs_ref, bc_ref, z_ref, dt_ref, cwx_ref, cwb_ref, cbx_ref, cbb_ref, dtb_ref, alog_ref,
                dskip_ref, nw_ref, out_ref, state_ref, hx_ref, hb_ref):
    q = SSD_CHUNK
    gw = HEADS_PER_SSM_GROUP * SSM_HEAD_DIM

    @pl.when(pl.program_id(1) == 0)
    def _():
        state_ref[...] = jnp.zeros_like(state_ref)
        hx_ref[...] = jnp.zeros_like(hx_ref)
        hb_ref[...] = jnp.zeros_like(hb_ref)

    x_dt = dt_ref[0] + dtb_ref[...]
    dt = jnp.maximum(x_dt, 0.0) + jnp.log1p(jnp.exp(-jnp.abs(x_dt)))
    d_a = dt * (-jnp.exp(alog_ref[...]))
    ri = lax.broadcasted_iota(jnp.int32, (q, q), 0)
    ci = lax.broadcasted_iota(jnp.int32, (q, q), 1)
    causal = ri >= ci
    tril = jnp.where(causal, 1.0, 0.0).astype(BF16)
    d1 = d_a.astype(BF16)
    r1 = d_a - d1.astype(F32)
    d2 = r1.astype(BF16)
    d3 = (r1 - d2.astype(F32)).astype(BF16)
    la = (jnp.dot(tril, d1, preferred_element_type=F32) + jnp.dot(tril, d2, preferred_element_type=F32)
          + jnp.dot(tril, d3, preferred_element_type=F32))
    la_t = la.T
    dt_t = dt.T
    last = jnp.broadcast_to(la_t[:, q - 1:q], (q, q))
    w_t = jnp.exp(last - la_t) * dt_t
    state_decay = jnp.exp(last)

    xs = _conv_silu(xs_ref[0, 0].astype(F32), hx_ref, cwx_ref, cbx_ref)
    bc = _conv_silu(bc_ref[0, 0].astype(F32), hb_ref, cwb_ref, cbb_ref)

    for g in range(N_SSM_GROUPS):
        bm = bc[:, g * D_STATE:(g + 1) * D_STATE]
        cm = bc[:, (N_SSM_GROUPS + g) * D_STATE:(N_SSM_GROUPS + g + 1) * D_STATE]
        bm16 = bm.astype(BF16)
        cb = lax.dot_general(cm.astype(BF16), bm16, NT_DIMS, preferred_element_type=F32)
        xs_g = xs[:, g * gw:(g + 1) * gw]
        xs_t = xs_g.T
        y_t = []
        for h in range(HEADS_PER_SSM_GROUP):
            hh = g * HEADS_PER_SSM_GROUP + h
            rows = slice(hh * SSM_HEAD_DIM, (hh + 1) * SSM_HEAD_DIM)
            la_i = jnp.broadcast_to(la[:, hh:hh + 1], (q, q))
            decay = jnp.exp(jnp.where(causal, la_i - la_t[hh:hh + 1, :], -jnp.inf))
            lhs = jnp.concatenate([(cb * decay).astype(BF16), (cm * jnp.exp(la_i)).astype(BF16)], axis=1)
            x_h = xs_t[h * SSM_HEAD_DIM:(h + 1) * SSM_HEAD_DIM]
            state = state_ref[rows, :]
            rhs_t = jnp.concatenate([(x_h * dt_t[hh:hh + 1, :]).astype(BF16), state.astype(BF16)], axis=1)
            y_t.append(lax.dot_general(rhs_t, lhs, NT_DIMS, preferred_element_type=F32))
            upd = jnp.dot((x_h * w_t[hh:hh + 1, :]).astype(BF16), bm16, preferred_element_type=F32)
            state_ref[rows, :] = state * state_decay[hh:hh + 1, :] + upd
        y = jnp.concatenate(y_t, axis=0).T
        cols = slice(g * gw, (g + 1) * gw)
        y = y + xs_g * dskip_ref[:, cols]
        z = z_ref[0, 0, :, cols].astype(F32)
        yg = y * (z * _sigmoid(z))
        yg = yg * lax.rsqrt(jnp.mean(yg * yg, axis=-1, keepdims=True) + EPS)
        out_ref[0, :, cols] = (yg * nw_ref[:, cols]).astype(out_ref.dtype)


def _ssd(main, dt3, conv_w_xs, conv_w_bc, conv_b_xs, conv_b_bc, dt_bias, a_log, d_skip, norm_w,
         *, col_z, col_xs, col_bc):
    b, _, s, _ = main.shape
    d_inner = conv_w_xs.shape[1]
    bc_w = conv_w_bc.shape[1]
    n_heads = N_SSM_GROUPS * HEADS_PER_SSM_GROUP
    q = SSD_CHUNK

    def const(shape):
        return pl.BlockSpec(shape, lambda bi, c: (0, 0))

    return pl.pallas_call(
        _ssd_kernel,
        grid=(b, s // q),
        in_specs=[
            pl.BlockSpec((1, 1, q, d_inner), lambda bi, c: (bi, 0, c, col_xs)),
            pl.BlockSpec((1, 1, q, bc_w), lambda bi, c: (bi, 0, c, col_bc)),
            pl.BlockSpec((1, 1, q, d_inner), lambda bi, c: (bi, 0, c, col_z)),
            pl.BlockSpec((1, q, LANES), lambda bi, c: (bi, c, 0)),
            const((CONV_WIDTH, d_inner)), const((CONV_WIDTH, bc_w)), const((1, d_inner)), const((1, bc_w)),
            const((1, LANES)), const((1, LANES)), const((1, d_inner)), const((1, d_inner)),
        ],
        out_specs=pl.BlockSpec((1, q, d_inner), lambda bi, c: (bi, c, 0)),
        out_shape=jax.ShapeDtypeStruct((b, s, d_inner), BF16),
        scratch_shapes=[pltpu.VMEM((n_heads * SSM_HEAD_DIM, D_STATE), F32),
                        pltpu.VMEM((SUBLANES, d_inner), F32),
                        pltpu.VMEM((SUBLANES, bc_w), F32)],
        compiler_params=_compiler_params(("parallel", "arbitrary")),
        name="ssd",
    )(main, main, main, dt3, conv_w_xs, conv_w_bc, conv_b_xs, conv_b_bc, dt_bias, a_log, d_skip, norm_w)


def _merge_kernel(*refs):
    attn_refs = refs[:2 * N_DIL]
    ssm_ref, gate_ref, h_ref, wa_ref, ws_ref, wo_ref, out_ref, nat_ref = refs[2 * N_DIL:]
    tm = h_ref.shape[1]
    n_slabs = GROUP_WIDTH // LANES
    for k, src in enumerate(attn_refs):
        dil = src.shape[1]
        for r in range(dil if dil > 1 else 0):
            for c in range(n_slabs):
                nat_ref[k, c, pl.ds(r, tm // dil, stride=dil), :] = src[0, r, :, c * LANES:(c + 1) * LANES]

    def natural(k, c):
        src = attn_refs[k]
        return src[0, 0, :, c * LANES:(c + 1) * LANES] if src.shape[1] == 1 else nat_ref[k, c]

    slabs = []
    for c in range(n_slabs):
        o = [natural(2 * g, c) for g in range(N_DIL)]
        l = [natural(2 * g + 1, c) for g in range(N_DIL)]
        mx = functools.reduce(jnp.maximum, l)
        e = [jnp.exp(lg - mx) for lg in l]
        num = functools.reduce(jnp.add, [eg * og for eg, og in zip(e, o)])
        slabs.append((num / functools.reduce(jnp.add, e)).astype(BF16))
    attn = jnp.concatenate(slabs, axis=1)
    a = jnp.dot(attn, wa_ref[...], preferred_element_type=F32)
    s = jnp.dot(ssm_ref[0], ws_ref[...], preferred_element_type=F32)
    d = a.shape[1]
    gates = _sigmoid(gate_ref[0, 0].astype(F32))
    merged = (gates[:, :d] * a + gates[:, d:] * s).astype(BF16)
    out_ref[0] = h_ref[0] + jnp.dot(merged, wo_ref[...], preferred_element_type=F32)


def _merge(attn_outs, ssm, main, h3, w_attn, w_ssm, w_out, *, col_gates, tm):
    b, s, d = h3.shape
    d_inner = ssm.shape[2]

    def rows(width):
        return pl.BlockSpec((1, tm, width), lambda bi, i: (bi, i, 0))

    def residues(arr):
        dil = arr.shape[1]
        return pl.BlockSpec((1, dil, tm // dil, GROUP_WIDTH), lambda bi, i: (bi, 0, i, 0))

    def const(arr):
        return pl.BlockSpec(arr.shape, lambda bi, i: (0, 0))

    flat = [a for pair in attn_outs for a in pair]
    return pl.pallas_call(
        _merge_kernel,
        grid=(b, s // tm),
        in_specs=[residues(a) for a in flat]
        + [rows(d_inner), pl.BlockSpec((1, 1, tm, 2 * d), lambda bi, i: (bi, 0, i, col_gates)), rows(d),
           const(w_attn), const(w_ssm), const(w_out)],
        out_specs=rows(d),
        out_shape=jax.ShapeDtypeStruct((b, s, d), F32),
        scratch_shapes=[pltpu.VMEM((len(flat), GROUP_WIDTH // LANES, tm, LANES), F32)],
        compiler_params=_compiler_params(("parallel", "parallel")),
        name="merge",
    )(*flat, ssm, main, h3, w_attn, w_ssm, w_out)


def _ffn_kernel(h_ref, nw_ref, wg_ref, wu_ref, wo_ref, fw_ref, out_ref, *, n_chunks, final_norm):
    h = h_ref[...]
    xn = _rmsnorm(h, nw_ref[...]).astype(BF16)
    ck = wo_ref.shape[0] // n_chunks
    acc = h
    for c in range(n_chunks):
        gate = jnp.dot(xn, wg_ref[:, c * ck:(c + 1) * ck], preferred_element_type=F32)
        up = jnp.dot(xn, wu_ref[:, c * ck:(c + 1) * ck], preferred_element_type=F32)
        act = (gate * _sigmoid(gate) * up).astype(BF16)
        acc = acc + jnp.dot(act, wo_ref[c * ck:(c + 1) * ck, :], preferred_element_type=F32)
    if final_norm:
        acc = _rmsnorm(acc, fw_ref[...])
    out_ref[...] = acc


def _ffn(h, norm_w, w_gate, w_up, w_down, final_w, *, tm, n_chunks, final_norm):
    t, d = h.shape

    def const(arr):
        return pl.BlockSpec(arr.shape, lambda i: (0, 0))

    return pl.pallas_call(
        functools.partial(_ffn_kernel, n_chunks=n_chunks, final_norm=final_norm),
        grid=(t // tm,),
        in_specs=[pl.BlockSpec((tm, d), lambda i: (i, 0)), const(norm_w), const(w_gate), const(w_up),
                  const(w_down), const(final_w)],
        out_specs=pl.BlockSpec((tm, d), lambda i: (i, 0)),
        out_shape=jax.ShapeDtypeStruct((t, d), F32),
        compiler_params=_compiler_params(("parallel",)),
        name="ffn",
    )(h, norm_w, w_gate, w_up, w_down, final_w)


def kernel(x, norm1_w, w_in, conv_w, conv_b, dt_bias, a_log, d_skip, ssm_norm_w, w_attn_branch, w_ssm_branch,
           w_out, norm2_w, w_ffn_in, w_ffn_out, rel_bias, final_norm_w):
    b, s, d = x.shape
    t = b * s
    depth = w_in.shape[0]
    d_inner = ssm_norm_w.shape[1]
    n_heads = dt_bias.shape[1]
    bc_w = 2 * N_SSM_GROUPS * D_STATE
    d_ff = w_ffn_out.shape[1]
    assert d_inner == N_SSM_GROUPS * HEADS_PER_SSM_GROUP * SSM_HEAD_DIM and n_heads <= LANES
    assert s % (DILATED_GROUPS[-1][1] * ATTN_BLOCK) == 0 and s % SSD_CHUNK == 0

    o_z = 3 * ATTN_WIDTH
    o_xbc = o_z + d_inner
    o_dt = o_xbc + d_inner + bc_w
    o_gate = o_dt + n_heads

    def qkv_cols(g):
        return [w_in[:, :, part * ATTN_WIDTH + g * GROUP_WIDTH:part * ATTN_WIDTH + (g + 1) * GROUP_WIDTH]
                for part in range(3)]

    w_main = jnp.concatenate([w_in[:, :, o_z:o_xbc], w_in[:, :, o_gate:], w_in[:, :, o_xbc:o_dt]] + qkv_cols(0),
                             axis=2).astype(BF16)
    w_qkv = [None] + [jnp.concatenate(qkv_cols(g), axis=2).astype(BF16) for g in range(1, N_DIL)]
    w_dt = jnp.pad(w_in[:, :, o_dt:o_gate], ((0, 0), (0, 0), (0, LANES - n_heads))).astype(BF16)
    col_z = 0
    col_gates = d_inner // (2 * d)
    col_xs = (d_inner + 2 * d) // d_inner
    col_bc = (2 * d_inner + 2 * d) // bc_w
    col_q = (2 * d_inner + 2 * d + bc_w) // GROUP_WIDTH
    assert d_inner % (2 * d) == 0 and (d_inner + 2 * d) % d_inner == 0 and (2 * d_inner + 2 * d) % bc_w == 0
    assert (2 * d_inner + 2 * d + bc_w) % GROUP_WIDTH == 0

    pad_h = ((0, 0), (0, LANES - n_heads))
    dt_bias_p = jnp.pad(dt_bias, pad_h)
    a_log_p = jnp.pad(a_log, pad_h)
    d_skip_x = jnp.repeat(d_skip, SSM_HEAD_DIM, axis=1)
    w_attn16, w_ssm16, w_out16 = w_attn_branch.astype(BF16), w_ssm_branch.astype(BF16), w_out.astype(BF16)
    w_gate16, w_up16 = w_ffn_in[:, :, :d_ff].astype(BF16), w_ffn_in[:, :, d_ff:].astype(BF16)
    w_down16 = w_ffn_out.astype(BF16)
    bias_tables = _attention_bias_tables(rel_bias)

    h = x
    for layer in range(depth):
        nw1 = norm1_w[layer:layer + 1]
        main, dt = _in_proj(h, nw1, w_main[layer], w_dt[layer], dilation=1, tm=512, chunk=1024)

        attn_outs = []
        for g, (_, dil) in enumerate(DILATED_GROUPS):
            if dil == 1:
                qkv, cq = main, col_q
            else:
                (qkv,), cq = _in_proj(h, nw1, w_qkv[g][layer], dilation=dil, tm=1024, chunk=GROUP_WIDTH), 0
            attn_outs.append(_attention(qkv, bias_tables[g], col_q=cq, col_k=cq + 1, col_v=cq + 2,
                                        tq=min(512, s // dil)))

        ssm = _ssd(main, dt, conv_w[layer, :, :d_inner], conv_w[layer, :, d_inner:],
                   conv_b[layer:layer + 1, :d_inner], conv_b[layer:layer + 1, d_inner:],
                   dt_bias_p[layer:layer + 1], a_log_p[layer:layer + 1], d_skip_x[layer:layer + 1],
                   ssm_norm_w[layer:layer + 1], col_z=col_z, col_xs=col_xs, col_bc=col_bc)

        h = _merge(attn_outs, ssm, main, h, w_attn16[layer], w_ssm16[layer], w_out16[layer],
                   col_gates=col_gates, tm=256)
        h = _ffn(h.reshape(t, d), norm2_w[layer:layer + 1], w_gate16[layer], w_up16[layer], w_down16[layer],
                 final_norm_w[None, :], tm=256, n_chunks=2, final_norm=(layer == depth - 1)).reshape(b, s, d)
    return h
```

```python
import functools
import math

import jax
import jax.numpy as jnp
from jax import lax
from jax.experimental import pallas as pl
from jax.experimental.pallas import tpu as pltpu

HEAD_DIM = 64
DILATED_GROUPS = ((128, 1), (512, 4), (2048, 16))
N_DIL = len(DILATED_GROUPS)
HEADS_PER_GROUP = 8
GROUP_WIDTH = HEADS_PER_GROUP * HEAD_DIM
ATTN_WIDTH = N_DIL * GROUP_WIDTH
ATTN_BLOCK = 128
N_REL_BUCKETS = 32
REL_MAX_DISTANCE = 2048
SSM_HEAD_DIM = 64
N_SSM_GROUPS = 4
HEADS_PER_SSM_GROUP = 8
D_STATE = 128
CONV_WIDTH = 4
SSD_CHUNK = 128
EPS = 1e-6

LANES = 128
SUBLANES = 8
VMEM_LIMIT_BYTES = 56 * 1024 * 1024

BF16 = jnp.bfloat16
F32 = jnp.float32
NT_DIMS = (((1,), (1,)), ((), ()))


def _compiler_params(semantics):
    return pltpu.CompilerParams(dimension_semantics=semantics, vmem_limit_bytes=VMEM_LIMIT_BYTES)


def _rmsnorm(x, w):
    return x * lax.rsqrt(jnp.mean(x * x, axis=-1, keepdims=True) + EPS) * w


def _sigmoid(x):
    return 1.0 / (1.0 + jnp.exp(-x))


PERM_BLOCK = 256


def _in_proj_kernel(x_ref, nw_ref, w_ref, *rest, dilation, with_dt, chunk):
    if with_dt:
        wdt_ref, out_ref, dt_ref, xn_ref = rest
    else:
        out_ref, xn_ref = rest
    tm = x_ref.shape[1]
    n = w_ref.shape[1]
    xn = _rmsnorm(x_ref[0], nw_ref[...]).astype(BF16)
    if with_dt:
        dt_ref[0] = jnp.dot(xn, wdt_ref[...], preferred_element_type=F32)
    if dilation == 1:
        xn_ref[...] = xn
    else:
        per = PERM_BLOCK // dilation
        i = lax.broadcasted_iota(jnp.int32, (PERM_BLOCK, PERM_BLOCK), 0)
        k = lax.broadcasted_iota(jnp.int32, (PERM_BLOCK, PERM_BLOCK), 1)
        src = (i & (per - 1)) * dilation + (i >> (per.bit_length() - 1))
        perm = jnp.where(k == src, 1.0, 0.0).astype(BF16)
        for u in range(tm // PERM_BLOCK):
            y = jnp.dot(perm, xn[u * PERM_BLOCK:(u + 1) * PERM_BLOCK], preferred_element_type=F32).astype(BF16)
            for r in range(dilation):
                dst = r * (tm // dilation) + u * per
                xn_ref[dst:dst + per, :] = y[r * per:(r + 1) * per]
    for c0 in range(0, n, chunk):
        c1 = min(c0 + chunk, n)
        res = jnp.dot(xn_ref[...], w_ref[:, c0:c1], preferred_element_type=F32).astype(out_ref.dtype)
        out_ref[0, :, :, c0:c1] = res.reshape(dilation, tm // dilation, c1 - c0)


def _resident(shape):
    zeros = (0,) * len(shape)
    return pl.BlockSpec(shape, lambda *_: zeros, pipeline_mode=pl.Buffered(1))


def _in_proj(h3, norm_w, w, w_dt=None, *, dilation, tm, chunk):
    b, s, d = h3.shape
    n = w.shape[1]
    with_dt = w_dt is not None
    in_specs = [pl.BlockSpec((1, tm, d), lambda bi, i: (bi, i, 0)), _resident((1, d)), _resident((d, n))]
    out_specs = [pl.BlockSpec((1, dilation, tm // dilation, n), lambda bi, i: (bi, 0, i, 0))]
    out_shape = [jax.ShapeDtypeStruct((b, dilation, s // dilation, n), BF16)]
    args = [h3, norm_w, w]
    if with_dt:
        in_specs.append(_resident((d, LANES)))
        out_specs.append(pl.BlockSpec((1, tm, LANES), lambda bi, i: (bi, i, 0)))
        out_shape.append(jax.ShapeDtypeStruct((b, s, LANES), F32))
        args.append(w_dt)
    return pl.pallas_call(
        functools.partial(_in_proj_kernel, dilation=dilation, with_dt=with_dt, chunk=chunk),
        grid=(b, s // tm),
        in_specs=in_specs,
        out_specs=out_specs,
        out_shape=out_shape,
        scratch_shapes=[pltpu.VMEM((tm, d), BF16)],
        compiler_params=_compiler_params(("parallel", "parallel")),
        name=f"in_proj_d{dilation}",
    )(*args)


def _attention_kernel(q_ref, kp_ref, kc_ref, vp_ref, vc_ref, bias_ref, o_ref, l_ref, kext_ref, vext_ref, *, tq):
    n = pl.program_id(2)
    kext_ref[0:ATTN_BLOCK] = kp_ref[0, 0]
    kext_ref[ATTN_BLOCK:] = kc_ref[0, 0]
    vext_ref[0:ATTN_BLOCK] = vp_ref[0, 0]
    vext_ref[ATTN_BLOCK:] = vc_ref[0, 0]
    low_half = lax.broadcasted_iota(jnp.int32, (1, LANES), 1) < HEAD_DIM

    for s in range(tq // ATTN_BLOCK):
        r0 = s * ATTN_BLOCK
        qs = q_ref[0, 0, pl.ds(r0, ATTN_BLOCK), :] * (HEAD_DIM ** -0.5)
        ks = kext_ref[pl.ds(r0, 2 * ATTN_BLOCK), :]
        vs = vext_ref[pl.ds(r0, 2 * ATTN_BLOCK), :]
        bidx = jnp.where(n == 0, 0, 1) if s == 0 else 1
        for p in range(GROUP_WIDTH // LANES):
            cols = slice(p * LANES, (p + 1) * LANES)
            qp, kp, vp = qs[:, cols], ks[:, cols], vs[:, cols]
            outs, lses = [], []
            for e in range(2):
                qe = jnp.where(low_half if e == 0 else jnp.logical_not(low_half), qp, jnp.zeros_like(qp))
                sc = lax.dot_general(qe, kp, NT_DIMS, preferred_element_type=F32) + bias_ref[bidx, 2 * p + e]
                m = jnp.max(sc, axis=1, keepdims=True)
                pe = jnp.exp(sc - m)
                den = jnp.sum(pe, axis=1, keepdims=True)
                num = jnp.dot(pe.astype(BF16), vp, preferred_element_type=F32)
                outs.append(num / den)
                lses.append(jnp.broadcast_to(m + jnp.log(den), (ATTN_BLOCK, LANES)))
            o_ref[0, 0, pl.ds(r0, ATTN_BLOCK), cols] = jnp.where(low_half, outs[0], outs[1])
            l_ref[0, 0, pl.ds(r0, ATTN_BLOCK), cols] = jnp.where(low_half, lses[0], lses[1])


def _attention(qkv, bias_g, *, col_q, col_k, col_v, tq):
    b, dilation, seg, _ = qkv.shape
    blocks_per_tile = tq // ATTN_BLOCK

    def cur(col):
        return pl.BlockSpec((1, 1, tq, GROUP_WIDTH), lambda bi, r, n: (bi, r, n, col))

    def prev(col):
        return pl.BlockSpec((1, 1, ATTN_BLOCK, GROUP_WIDTH),
                            lambda bi, r, n: (bi, r, jnp.maximum(n * blocks_per_tile - 1, 0), col))

    out_spec = pl.BlockSpec((1, 1, tq, GROUP_WIDTH), lambda bi, r, n: (bi, r, n, 0))
    out_sds = jax.ShapeDtypeStruct((b, dilation, seg, GROUP_WIDTH), F32)
    return pl.pallas_call(
        functools.partial(_attention_kernel, tq=tq),
        grid=(b, dilation, seg // tq),
        in_specs=[cur(col_q), prev(col_k), cur(col_k), prev(col_v), cur(col_v),
                  pl.BlockSpec(bias_g.shape, lambda bi, r, n: (0, 0, 0, 0))],
        out_specs=[out_spec, out_spec],
        out_shape=[out_sds, out_sds],
        scratch_shapes=[pltpu.VMEM((tq + ATTN_BLOCK, GROUP_WIDTH), BF16),
                        pltpu.VMEM((tq + ATTN_BLOCK, GROUP_WIDTH), BF16)],
        compiler_params=_compiler_params(("parallel", "parallel", "arbitrary")),
        name=f"attention_d{dilation}",
    )(qkv, qkv, qkv, qkv, qkv, bias_g)


def _t5_causal_bucket(dist):
    max_exact = N_REL_BUCKETS // 2
    d_f = jnp.maximum(dist, 1).astype(F32)
    large = max_exact + (jnp.log(d_f / max_exact) / math.log(REL_MAX_DISTANCE / max_exact)
                         * (N_REL_BUCKETS - max_exact)).astype(jnp.int32)
    large = jnp.minimum(large, N_REL_BUCKETS - 1)
    return jnp.where(dist < max_exact, dist, large)


def _attention_bias_tables(rel_bias):
    q, q2 = ATTN_BLOCK, 2 * ATTN_BLOCK
    steps = jnp.arange(q2) - (q - 1)
    in_prev = (jnp.arange(q2) < q)[None, None, :]
    tables = []
    for g, (window, dil) in enumerate(DILATED_GROUPS):
        n_steps = window // dil
        assert n_steps <= q
        rel_g = rel_bias[:, g * HEADS_PER_GROUP:(g + 1) * HEADS_PER_GROUP].astype(F32)
        vals = rel_g[_t5_causal_bucket(jnp.clip(steps, 0, n_steps) * dil)]
        vec = jnp.where(((steps >= 0) & (steps <= n_steps))[:, None], vals, -jnp.inf).T
        skew = jnp.tile(vec, (1, q + 1))[:, :q * (q2 + 1)].reshape(HEADS_PER_GROUP, q, q2 + 1)[:, :, :q2]
        rest = skew[:, :, ::-1]
        first = jnp.where(in_prev, -jnp.inf, rest)
        tables.append(jnp.stack([first, rest]))
    return jnp.stack(tables)


LOG2E = math.log2(math.e)


def _conv_silu(slab_ref, c, w_ref, b_ref):
    q = slab_ref.shape[1] - SUBLANES
    cols = slice(c * LANES, (c + 1) * LANES)
    acc = b_ref[:, cols] + slab_ref[c, pl.ds(SUBLANES, q), :] * w_ref[CONV_WIDTH - 1:CONV_WIDTH, cols]
    for s in range(1, CONV_WIDTH):
        acc = acc + slab_ref[c, pl.ds(SUBLANES - s, q), :] * w_ref[CONV_WIDTH - 1 - s:CONV_WIDTH - s, cols]
    return acc * _sigmoid(acc)


def _ssd_kernel(xs_ref, bc_ref, z_ref, dt_ref, cw_ref, cb_ref, dtb_ref, alog_ref,
                dskip_ref, nw_ref, out_ref, state_ref, slab_ref):
    q = SSD_CHUNK
    gw = HEADS_PER_SSM_GROUP * SSM_HEAD_DIM
    xs_slabs = xs_ref.shape[3] // LANES
    slabs_per_group = gw // LANES

    @pl.when(pl.program_id(1) == 0)
    def _():
        state_ref[...] = jnp.zeros_like(state_ref)
        slab_ref[:, 0:SUBLANES, :] = jnp.zeros((slab_ref.shape[0], SUBLANES, LANES), F32)

    for c in range(xs_slabs):
        slab_ref[c, SUBLANES:, :] = xs_ref[0, 0, :, c * LANES:(c + 1) * LANES].astype(F32)
    for c in range(bc_ref.shape[3] // LANES):
        slab_ref[xs_slabs + c, SUBLANES:, :] = bc_ref[0, 0, :, c * LANES:(c + 1) * LANES].astype(F32)

    x_dt = dt_ref[0] + dtb_ref[...]
    dt = jnp.maximum(x_dt, 0.0) + jnp.log1p(jnp.exp(-jnp.abs(x_dt)))
    d_a = dt * (-jnp.exp(alog_ref[...]))
    ri = lax.broadcasted_iota(jnp.int32, (q, q), 0)
    ci = lax.broadcasted_iota(jnp.int32, (q, q), 1)
    causal = ri >= ci
    tril = jnp.where(causal, 1.0, 0.0).astype(BF16)
    d1 = d_a.astype(BF16)
    r1 = d_a - d1.astype(F32)
    d2 = r1.astype(BF16)
    d3 = (r1 - d2.astype(F32)).astype(BF16)
    la = (jnp.dot(tril, d1, preferred_element_type=F32) + jnp.dot(tril, d2, preferred_element_type=F32)
          + jnp.dot(tril, d3, preferred_element_type=F32))
    la = la * LOG2E
    la_t = la.T
    dt_t = dt.T
    last = jnp.broadcast_to(la_t[:, q - 1:q], (q, q))
    w_t = jnp.exp2(last - la_t) * dt_t
    state_decay = jnp.exp2(last)

    for g in range(N_SSM_GROUPS):
        bm = _conv_silu(slab_ref, xs_slabs + g, cw_ref, cb_ref)
        cm = _conv_silu(slab_ref, xs_slabs + N_SSM_GROUPS + g, cw_ref, cb_ref)
        bm16 = bm.astype(BF16)
        cb = lax.dot_general(cm.astype(BF16), bm16, NT_DIMS, preferred_element_type=F32)
        xs_g = jnp.concatenate([_conv_silu(slab_ref, g * slabs_per_group + c, cw_ref, cb_ref)
                                for c in range(slabs_per_group)], axis=1)
        xs_t = xs_g.T
        y_t = []
        for h in range(HEADS_PER_SSM_GROUP):
            hh = g * HEADS_PER_SSM_GROUP + h
            rows = slice(hh * SSM_HEAD_DIM, (hh + 1) * SSM_HEAD_DIM)
            la_i = jnp.broadcast_to(la[:, hh:hh + 1], (q, q))
            decay = jnp.exp2(jnp.where(causal, la_i - la_t[hh:hh + 1, :], -jnp.inf))
            lhs = jnp.concatenate([(cb * decay).astype(BF16), (cm * jnp.exp2(la_i)).astype(BF16)], axis=1)
            x_h = xs_t[h * SSM_HEAD_DIM:(h + 1) * SSM_HEAD_DIM]
            state = state_ref[rows, :]
            rhs_t = jnp.concatenate([(x_h * dt_t[hh:hh + 1, :]).astype(BF16), state.astype(BF16)], axis=1)
            y_t.append(lax.dot_general(rhs_t, lhs, NT_DIMS, preferred_element_type=F32))
            upd = jnp.dot((x_h * w_t[hh:hh + 1, :]).astype(BF16), bm16, preferred_element_type=F32)
            state_ref[rows, :] = state * state_decay[hh:hh + 1, :] + upd
        y = jnp.concatenate(y_t, axis=0).T
        cols = slice(g * gw, (g + 1) * gw)
        y = y + xs_g * dskip_ref[:, cols]
        z = z_ref[0, 0, :, cols].astype(F32)
        yg = y * (z * _sigmoid(z))
        yg = yg * lax.rsqrt(jnp.mean(yg * yg, axis=-1, keepdims=True) + EPS)
        out_ref[0, :, cols] = (yg * nw_ref[:, cols]).astype(out_ref.dtype)

    slab_ref[:, 0:SUBLANES, :] = slab_ref[:, q:q + SUBLANES, :]


def _ssd(main, dt3, conv_w, conv_b, dt_bias, a_log, d_skip, norm_w, *, col_z, col_xs, col_bc):
    b, _, s, _ = main.shape
    d_inner = norm_w.shape[1]
    conv_ch = conv_w.shape[1]
    bc_w = conv_ch - d_inner
    n_heads = N_SSM_GROUPS * HEADS_PER_SSM_GROUP
    q = SSD_CHUNK

    return pl.pallas_call(
        _ssd_kernel,
        grid=(b, s // q),
        in_specs=[
            pl.BlockSpec((1, 1, q, d_inner), lambda bi, c: (bi, 0, c, col_xs)),
            pl.BlockSpec((1, 1, q, bc_w), lambda bi, c: (bi, 0, c, col_bc)),
            pl.BlockSpec((1, 1, q, d_inner), lambda bi, c: (bi, 0, c, col_z)),
            pl.BlockSpec((1, q, LANES), lambda bi, c: (bi, c, 0)),
            _resident((CONV_WIDTH, conv_ch)), _resident((1, conv_ch)),
            _resident((1, LANES)), _resident((1, LANES)), _resident((1, d_inner)), _resident((1, d_inner)),
        ],
        out_specs=pl.BlockSpec((1, q, d_inner), lambda bi, c: (bi, c, 0)),
        out_shape=jax.ShapeDtypeStruct((b, s, d_inner), BF16),
        scratch_shapes=[pltpu.VMEM((n_heads * SSM_HEAD_DIM, D_STATE), F32),
                        pltpu.VMEM((conv_ch // LANES, SUBLANES + q, LANES), F32)],
        compiler_params=_compiler_params(("parallel", "arbitrary")),
        name="ssd",
    )(main, main, main, dt3, conv_w, conv_b, dt_bias, a_log, d_skip, norm_w)


def _merge_kernel(*refs):
    attn_refs = refs[:2 * N_DIL]
    ssm_ref, gate_ref, h_ref, wa_ref, ws_ref, wo_ref, out_ref, nat_ref = refs[2 * N_DIL:]
    tm = h_ref.shape[1]
    n_slabs = GROUP_WIDTH // LANES
    for k, src in enumerate(attn_refs):
        dil = src.shape[1]
        for r in range(dil if dil > 1 else 0):
            for c in range(n_slabs):
                nat_ref[k, c, pl.ds(r, tm // dil, stride=dil), :] = src[0, r, :, c * LANES:(c + 1) * LANES]

    def natural(k, c):
        src = attn_refs[k]
        return src[0, 0, :, c * LANES:(c + 1) * LANES] if src.shape[1] == 1 else nat_ref[k, c]

    slabs = []
    for c in range(n_slabs):
        o = [natural(2 * g, c) for g in range(N_DIL)]
        l = [natural(2 * g + 1, c) for g in range(N_DIL)]
        mx = functools.reduce(jnp.maximum, l)
        e = [jnp.exp(lg - mx) for lg in l]
        num = functools.reduce(jnp.add, [eg * og for eg, og in zip(e, o)])
        slabs.append((num / functools.reduce(jnp.add, e)).astype(BF16))
    attn = jnp.concatenate(slabs, axis=1)
    a = jnp.dot(attn, wa_ref[...], preferred_element_type=F32)
    s = jnp.dot(ssm_ref[0], ws_ref[...], preferred_element_type=F32)
    d = a.shape[1]
    gates = _sigmoid(gate_ref[0, 0].astype(F32))
    merged = (gates[:, :d] * a + gates[:, d:] * s).astype(BF16)
    out_ref[0] = h_ref[0] + jnp.dot(merged, wo_ref[...], preferred_element_type=F32)


def _merge(attn_outs, ssm, main, h3, w_attn, w_ssm, w_out, *, col_gates, tm):
    b, s, d = h3.shape
    d_inner = ssm.shape[2]

    def rows(width):
        return pl.BlockSpec((1, tm, width), lambda bi, i: (bi, i, 0))

    def residues(arr):
        dil = arr.shape[1]
        return pl.BlockSpec((1, dil, tm // dil, GROUP_WIDTH), lambda bi, i: (bi, 0, i, 0))

    def const(arr):
        return _resident(arr.shape)

    flat = [a for pair in attn_outs for a in pair]
    return pl.pallas_call(
        _merge_kernel,
        grid=(b, s // tm),
        in_specs=[residues(a) for a in flat]
        + [rows(d_inner), pl.BlockSpec((1, 1, tm, 2 * d), lambda bi, i: (bi, 0, i, col_gates)), rows(d),
           const(w_attn), const(w_ssm), const(w_out)],
        out_specs=rows(d),
        out_shape=jax.ShapeDtypeStruct((b, s, d), F32),
        scratch_shapes=[pltpu.VMEM((len(flat), GROUP_WIDTH // LANES, tm, LANES), F32)],
        compiler_params=_compiler_params(("parallel", "parallel")),
        name="merge",
    )(*flat, ssm, main, h3, w_attn, w_ssm, w_out)


def _ffn_kernel(h_ref, nw_ref, wg_ref, wu_ref, wo_ref, fw_ref, out_ref, *, n_chunks, final_norm):
    h = h_ref[...]
    xn = _rmsnorm(h, nw_ref[...]).astype(BF16)
    ck = wo_ref.shape[0] // n_chunks
    acc = h
    for c in range(n_chunks):
        gate = jnp.dot(xn, wg_ref[:, c * ck:(c + 1) * ck], preferred_element_type=F32)
        up = jnp.dot(xn, wu_ref[:, c * ck:(c + 1) * ck], preferred_element_type=F32)
        act = (gate * _sigmoid(gate) * up).astype(BF16)
        acc = acc + jnp.dot(act, wo_ref[c * ck:(c + 1) * ck, :], preferred_element_type=F32)
    if final_norm:
        acc = _rmsnorm(acc, fw_ref[...])
    out_ref[...] = acc


def _ffn(h, norm_w, w_gate, w_up, w_down, final_w, *, tm, n_chunks, final_norm):
    t, d = h.shape

    def const(arr):
        return _resident(arr.shape)

    return pl.pallas_call(
        functools.partial(_ffn_kernel, n_chunks=n_chunks, final_norm=final_norm),
        grid=(t // tm,),
        in_specs=[pl.BlockSpec((tm, d), lambda i: (i, 0)), const(norm_w), const(w_gate), const(w_up),
                  const(w_down), const(final_w)],
        out_specs=pl.BlockSpec((tm, d), lambda i: (i, 0)),
        out_shape=jax.ShapeDtypeStruct((t, d), F32),
        compiler_params=_compiler_params(("parallel",)),
        name="ffn",
    )(h, norm_w, w_gate, w_up, w_down, final_w)


def kernel(x, norm1_w, w_in, conv_w, conv_b, dt_bias, a_log, d_skip, ssm_norm_w, w_attn_branch, w_ssm_branch,
           w_out, norm2_w, w_ffn_in, w_ffn_out, rel_bias, final_norm_w):
    b, s, d = x.shape
    t = b * s
    depth = w_in.shape[0]
    d_inner = ssm_norm_w.shape[1]
    n_heads = dt_bias.shape[1]
    bc_w = 2 * N_SSM_GROUPS * D_STATE
    d_ff = w_ffn_out.shape[1]
    assert d_inner == N_SSM_GROUPS * HEADS_PER_SSM_GROUP * SSM_HEAD_DIM and n_heads <= LANES
    assert s % (DILATED_GROUPS[-1][1] * ATTN_BLOCK) == 0 and s % SSD_CHUNK == 0

    o_z = 3 * ATTN_WIDTH
    o_xbc = o_z + d_inner
    o_dt = o_xbc + d_inner + bc_w
    o_gate = o_dt + n_heads

    def qkv_cols(g):
        return [w_in[:, :, part * ATTN_WIDTH + g * GROUP_WIDTH:part * ATTN_WIDTH + (g + 1) * GROUP_WIDTH]
                for part in range(3)]

    w_main = jnp.concatenate([w_in[:, :, o_z:o_xbc], w_in[:, :, o_gate:], w_in[:, :, o_xbc:o_dt]] + qkv_cols(0),
                             axis=2).astype(BF16)
    w_qkv = [None] + [jnp.concatenate(qkv_cols(g), axis=2).astype(BF16) for g in range(1, N_DIL)]
    w_dt = jnp.pad(w_in[:, :, o_dt:o_gate], ((0, 0), (0, 0), (0, LANES - n_heads))).astype(BF16)
    col_z = 0
    col_gates = d_inner // (2 * d)
    col_xs = (d_inner + 2 * d) // d_inner
    col_bc = (2 * d_inner + 2 * d) // bc_w
    col_q = (2 * d_inner + 2 * d + bc_w) // GROUP_WIDTH
    assert d_inner % (2 * d) == 0 and (d_inner + 2 * d) % d_inner == 0 and (2 * d_inner + 2 * d) % bc_w == 0
    assert (2 * d_inner + 2 * d + bc_w) % GROUP_WIDTH == 0

    pad_h = ((0, 0), (0, LANES - n_heads))
    dt_bias_p = jnp.pad(dt_bias, pad_h)
    a_log_p = jnp.pad(a_log, pad_h)
    d_skip_x = jnp.repeat(d_skip, SSM_HEAD_DIM, axis=1)
    w_attn16, w_ssm16, w_out16 = w_attn_branch.astype(BF16), w_ssm_branch.astype(BF16), w_out.astype(BF16)
    w_gate16, w_up16 = w_ffn_in[:, :, :d_ff].astype(BF16), w_ffn_in[:, :, d_ff:].astype(BF16)
    w_down16 = w_ffn_out.astype(BF16)
    bias_tables = _attention_bias_tables(rel_bias)

    h = x
    for layer in range(depth):
        nw1 = norm1_w[layer:layer + 1]
        main, dt = _in_proj(h, nw1, w_main[layer], w_dt[layer], dilation=1, tm=512, chunk=1024)

        attn_outs = []
        for g, (_, dil) in enumerate(DILATED_GROUPS):
            if dil == 1:
                qkv, cq = main, col_q
            else:
                (qkv,), cq = _in_proj(h, nw1, w_qkv[g][layer], dilation=dil, tm=1024, chunk=GROUP_WIDTH), 0
            attn_outs.append(_attention(qkv, bias_tables[g], col_q=cq, col_k=cq + 1, col_v=cq + 2,
                                        tq=min(512, s // dil)))

        ssm = _ssd(main, dt, conv_w[layer], conv_b[layer:layer + 1],
                   dt_bias_p[layer:layer + 1], a_log_p[layer:layer + 1], d_skip_x[layer:layer + 1],
                   ssm_norm_w[layer:layer + 1], col_z=col_z, col_xs=col_xs, col_bc=col_bc)

        h = _merge(attn_outs, ssm, main, h, w_attn16[layer], w_ssm16[layer], w_out16[layer],
                   col_gates=col_gates, tm=512)
        h = _ffn(h.reshape(t, d), norm2_w[layer:layer + 1], w_gate16[layer], w_up16[layer], w_down16[layer],
                 final_norm_w[None, :], tm=512, n_chunks=2, final_norm=(layer == depth - 1)).reshape(b, s, d)
    return h
```

```python
import functools
import math

import jax
import jax.numpy as jnp
from jax import lax
from jax.experimental import pallas as pl
from jax.experimental.pallas import tpu as pltpu

HEAD_DIM = 64
DILATED_GROUPS = ((128, 1), (512, 4), (2048, 16))
N_DIL = len(DILATED_GROUPS)
HEADS_PER_GROUP = 8
GROUP_WIDTH = HEADS_PER_GROUP * HEAD_DIM
ATTN_WIDTH = N_DIL * GROUP_WIDTH
ATTN_BLOCK = 128
N_REL_BUCKETS = 32
REL_MAX_DISTANCE = 2048
SSM_HEAD_DIM = 64
N_SSM_GROUPS = 4
HEADS_PER_SSM_GROUP = 8
D_STATE = 128
CONV_WIDTH = 4
SSD_CHUNK = 128
EPS = 1e-6

LANES = 128
SUBLANES = 8
VMEM_LIMIT_BYTES = 56 * 1024 * 1024

BF16 = jnp.bfloat16
F32 = jnp.float32
NT_DIMS = (((1,), (1,)), ((), ()))


def _compiler_params(semantics):
    return pltpu.CompilerParams(dimension_semantics=semantics, vmem_limit_bytes=VMEM_LIMIT_BYTES)


def _rmsnorm(x, w):
    return x * lax.rsqrt(jnp.mean(x * x, axis=-1, keepdims=True) + EPS) * w


def _sigmoid(x):
    return 1.0 / (1.0 + jnp.exp(-x))


PERM_BLOCK = 256


def _in_proj_kernel(x_ref, nw_ref, w_ref, *rest, dilation, with_dt, chunk):
    if with_dt:
        wdt_ref, out_ref, dt_ref, xn_ref = rest
    else:
        out_ref, xn_ref = rest
    tm = x_ref.shape[1]
    n = w_ref.shape[1]
    xn = _rmsnorm(x_ref[0], nw_ref[...]).astype(BF16)
    if with_dt:
        dt_ref[0] = jnp.dot(xn, wdt_ref[...], preferred_element_type=F32)
    if dilation == 1:
        xn_ref[...] = xn
    else:
        per = PERM_BLOCK // dilation
        i = lax.broadcasted_iota(jnp.int32, (PERM_BLOCK, PERM_BLOCK), 0)
        k = lax.broadcasted_iota(jnp.int32, (PERM_BLOCK, PERM_BLOCK), 1)
        src = (i & (per - 1)) * dilation + (i >> (per.bit_length() - 1))
        perm = jnp.where(k == src, 1.0, 0.0).astype(BF16)
        for u in range(tm // PERM_BLOCK):
            y = jnp.dot(perm, xn[u * PERM_BLOCK:(u + 1) * PERM_BLOCK], preferred_element_type=F32).astype(BF16)
            for r in range(dilation):
                dst = r * (tm // dilation) + u * per
                xn_ref[dst:dst + per, :] = y[r * per:(r + 1) * per]
    for c0 in range(0, n, chunk):
        c1 = min(c0 + chunk, n)
        res = jnp.dot(xn_ref[...], w_ref[:, c0:c1], preferred_element_type=F32).astype(out_ref.dtype)
        out_ref[0, :, :, c0:c1] = res.reshape(dilation, tm // dilation, c1 - c0)


def _resident(shape):
    zeros = (0,) * len(shape)
    return pl.BlockSpec(shape, lambda *_: zeros, pipeline_mode=pl.Buffered(1))


def _in_proj(h3, norm_w, w, w_dt=None, *, dilation, tm, chunk):
    b, s, d = h3.shape
    n = w.shape[1]
    with_dt = w_dt is not None
    in_specs = [pl.BlockSpec((1, tm, d), lambda bi, i: (bi, i, 0)), _resident((1, d)), _resident((d, n))]
    out_specs = [pl.BlockSpec((1, dilation, tm // dilation, n), lambda bi, i: (bi, 0, i, 0))]
    out_shape = [jax.ShapeDtypeStruct((b, dilation, s // dilation, n), BF16)]
    args = [h3, norm_w, w]
    if with_dt:
        in_specs.append(_resident((d, LANES)))
        out_specs.append(pl.BlockSpec((1, tm, LANES), lambda bi, i: (bi, i, 0)))
        out_shape.append(jax.ShapeDtypeStruct((b, s, LANES), F32))
        args.append(w_dt)
    return pl.pallas_call(
        functools.partial(_in_proj_kernel, dilation=dilation, with_dt=with_dt, chunk=chunk),
        grid=(b, s // tm),
        in_specs=in_specs,
        out_specs=out_specs,
        out_shape=out_shape,
        scratch_shapes=[pltpu.VMEM((tm, d), BF16)],
        compiler_params=_compiler_params(("parallel", "parallel")),
        name=f"in_proj_d{dilation}",
    )(*args)


def _attention_kernel(q_ref, kp_ref, kc_ref, vp_ref, vc_ref, bias_ref, o_ref, l_ref, kext_ref, vext_ref, *, tq):
    n = pl.program_id(2)
    kext_ref[0:ATTN_BLOCK] = kp_ref[0, 0]
    kext_ref[ATTN_BLOCK:] = kc_ref[0, 0]
    vext_ref[0:ATTN_BLOCK] = vp_ref[0, 0]
    vext_ref[ATTN_BLOCK:] = vc_ref[0, 0]
    low_half = lax.broadcasted_iota(jnp.int32, (1, LANES), 1) < HEAD_DIM

    for s in range(tq // ATTN_BLOCK):
        r0 = s * ATTN_BLOCK
        qs = q_ref[0, 0, pl.ds(r0, ATTN_BLOCK), :] * (HEAD_DIM ** -0.5)
        ks = kext_ref[pl.ds(r0, 2 * ATTN_BLOCK), :]
        vs = vext_ref[pl.ds(r0, 2 * ATTN_BLOCK), :]
        bidx = jnp.where(n == 0, 0, 1) if s == 0 else 1
        for p in range(GROUP_WIDTH // LANES):
            cols = slice(p * LANES, (p + 1) * LANES)
            qp, kp, vp = qs[:, cols], ks[:, cols], vs[:, cols]
            outs, lses = [], []
            for e in range(2):
                qe = jnp.where(low_half if e == 0 else jnp.logical_not(low_half), qp, jnp.zeros_like(qp))
                sc = lax.dot_general(qe, kp, NT_DIMS, preferred_element_type=F32) + bias_ref[bidx, 2 * p + e]
                m = jnp.max(sc, axis=1, keepdims=True)
                pe = jnp.exp(sc - m)
                den = jnp.sum(pe, axis=1, keepdims=True)
                num = jnp.dot(pe.astype(BF16), vp, preferred_element_type=F32)
                outs.append(num / den)
                lses.append(jnp.broadcast_to(m + jnp.log(den), (ATTN_BLOCK, LANES)))
            o_ref[0, 0, pl.ds(r0, ATTN_BLOCK), cols] = jnp.where(low_half, outs[0], outs[1])
            l_ref[0, 0, pl.ds(r0, ATTN_BLOCK), cols] = jnp.where(low_half, lses[0], lses[1])


def _attention(qkv, bias_g, *, col_q, col_k, col_v, tq):
    b, dilation, seg, _ = qkv.shape
    blocks_per_tile = tq // ATTN_BLOCK

    def cur(col):
        return pl.BlockSpec((1, 1, tq, GROUP_WIDTH), lambda bi, r, n: (bi, r, n, col))

    def prev(col):
        return pl.BlockSpec((1, 1, ATTN_BLOCK, GROUP_WIDTH),
                            lambda bi, r, n: (bi, r, jnp.maximum(n * blocks_per_tile - 1, 0), col))

    out_spec = pl.BlockSpec((1, 1, tq, GROUP_WIDTH), lambda bi, r, n: (bi, r, n, 0))
    out_sds = jax.ShapeDtypeStruct((b, dilation, seg, GROUP_WIDTH), F32)
    return pl.pallas_call(
        functools.partial(_attention_kernel, tq=tq),
        grid=(b, dilation, seg // tq),
        in_specs=[cur(col_q), prev(col_k), cur(col_k), prev(col_v), cur(col_v),
                  pl.BlockSpec(bias_g.shape, lambda bi, r, n: (0, 0, 0, 0))],
        out_specs=[out_spec, out_spec],
        out_shape=[out_sds, out_sds],
        scratch_shapes=[pltpu.VMEM((tq + ATTN_BLOCK, GROUP_WIDTH), BF16),
                        pltpu.VMEM((tq + ATTN_BLOCK, GROUP_WIDTH), BF16)],
        compiler_params=_compiler_params(("parallel", "parallel", "arbitrary")),
        name=f"attention_d{dilation}",
    )(qkv, qkv, qkv, qkv, qkv, bias_g)


def _t5_causal_bucket(dist):
    max_exact = N_REL_BUCKETS // 2
    d_f = jnp.maximum(dist, 1).astype(F32)
    large = max_exact + (jnp.log(d_f / max_exact) / math.log(REL_MAX_DISTANCE / max_exact)
                         * (N_REL_BUCKETS - max_exact)).astype(jnp.int32)
    large = jnp.minimum(large, N_REL_BUCKETS - 1)
    return jnp.where(dist < max_exact, dist, large)


def _attention_bias_tables(rel_bias):
    q, q2 = ATTN_BLOCK, 2 * ATTN_BLOCK
    steps = jnp.arange(q2) - (q - 1)
    in_prev = (jnp.arange(q2) < q)[None, None, :]
    tables = []
    for g, (window, dil) in enumerate(DILATED_GROUPS):
        n_steps = window // dil
        assert n_steps <= q
        rel_g = rel_bias[:, g * HEADS_PER_GROUP:(g + 1) * HEADS_PER_GROUP].astype(F32)
        vals = rel_g[_t5_causal_bucket(jnp.clip(steps, 0, n_steps) * dil)]
        vec = jnp.where(((steps >= 0) & (steps <= n_steps))[:, None], vals, -jnp.inf).T
        skew = jnp.tile(vec, (1, q + 1))[:, :q * (q2 + 1)].reshape(HEADS_PER_GROUP, q, q2 + 1)[:, :, :q2]
        rest = skew[:, :, ::-1]
        first = jnp.where(in_prev, -jnp.inf, rest)
        tables.append(jnp.stack([first, rest]))
    return jnp.stack(tables)


LOG2E = math.log2(math.e)


def _conv_silu(slab_ref, c, w_ref, b_ref):
    q = slab_ref.shape[1] - SUBLANES
    cols = slice(c * LANES, (c + 1) * LANES)
    acc = b_ref[:, cols] + slab_ref[c, pl.ds(SUBLANES, q), :] * w_ref[CONV_WIDTH - 1:CONV_WIDTH, cols]
    for s in range(1, CONV_WIDTH):
        acc = acc + slab_ref[c, pl.ds(SUBLANES - s, q), :] * w_ref[CONV_WIDTH - 1 - s:CONV_WIDTH - s, cols]
    return acc * _sigmoid(acc)


def _ssd_chunk_stages(r0, xs_ref, bc_ref, z_ref, dt_ref, cw_ref, cb_ref, dtb_ref, alog_ref, dskip_ref, nw_ref,
                      ssm_ref, state_ref, slab_ref):
    q = SSD_CHUNK
    rq = slice(r0, r0 + q)
    gw = HEADS_PER_SSM_GROUP * SSM_HEAD_DIM
    xs_slabs = xs_ref.shape[3] // LANES
    slabs_per_group = gw // LANES

    for c in range(xs_slabs):
        slab_ref[c, SUBLANES:, :] = xs_ref[0, 0, rq, c * LANES:(c + 1) * LANES].astype(F32)
    for c in range(bc_ref.shape[3] // LANES):
        slab_ref[xs_slabs + c, SUBLANES:, :] = bc_ref[0, 0, rq, c * LANES:(c + 1) * LANES].astype(F32)

    x_dt = dt_ref[0, rq, :] + dtb_ref[...]
    dt = jnp.maximum(x_dt, 0.0) + jnp.log1p(jnp.exp(-jnp.abs(x_dt)))
    d_a = dt * (-jnp.exp(alog_ref[...]))
    ri = lax.broadcasted_iota(jnp.int32, (q, q), 0)
    ci = lax.broadcasted_iota(jnp.int32, (q, q), 1)
    causal = ri >= ci
    tril = jnp.where(causal, 1.0, 0.0).astype(BF16)
    d1 = d_a.astype(BF16)
    r1 = d_a - d1.astype(F32)
    d2 = r1.astype(BF16)
    d3 = (r1 - d2.astype(F32)).astype(BF16)
    la = (jnp.dot(tril, d1, preferred_element_type=F32) + jnp.dot(tril, d2, preferred_element_type=F32)
          + jnp.dot(tril, d3, preferred_element_type=F32))
    la = la * LOG2E
    la_t = la.T
    dt_t = dt.T
    last = jnp.broadcast_to(la_t[:, q - 1:q], (q, q))
    w_t = jnp.exp2(last - la_t) * dt_t
    state_decay = jnp.exp2(last)
    yield

    for g in range(N_SSM_GROUPS):
        bm = _conv_silu(slab_ref, xs_slabs + g, cw_ref, cb_ref)
        cm = _conv_silu(slab_ref, xs_slabs + N_SSM_GROUPS + g, cw_ref, cb_ref)
        bm16 = bm.astype(BF16)
        cb = lax.dot_general(cm.astype(BF16), bm16, NT_DIMS, preferred_element_type=F32)
        xs_g = jnp.concatenate([_conv_silu(slab_ref, g * slabs_per_group + c, cw_ref, cb_ref)
                                for c in range(slabs_per_group)], axis=1)
        xs_t = xs_g.T
        yield
        y_t = []
        for h in range(HEADS_PER_SSM_GROUP):
            hh = g * HEADS_PER_SSM_GROUP + h
            rows = slice(hh * SSM_HEAD_DIM, (hh + 1) * SSM_HEAD_DIM)
            la_i = jnp.broadcast_to(la[:, hh:hh + 1], (q, q))
            decay = jnp.exp2(jnp.where(causal, la_i - la_t[hh:hh + 1, :], -jnp.inf))
            lhs = jnp.concatenate([(cb * decay).astype(BF16), (cm * jnp.exp2(la_i)).astype(BF16)], axis=1)
            x_h = xs_t[h * SSM_HEAD_DIM:(h + 1) * SSM_HEAD_DIM]
            state = state_ref[rows, :]
            rhs_t = jnp.concatenate([(x_h * dt_t[hh:hh + 1, :]).astype(BF16), state.astype(BF16)], axis=1)
            y_t.append(lax.dot_general(rhs_t, lhs, NT_DIMS, preferred_element_type=F32))
            upd = jnp.dot((x_h * w_t[hh:hh + 1, :]).astype(BF16), bm16, preferred_element_type=F32)
            state_ref[rows, :] = state * state_decay[hh:hh + 1, :] + upd
            yield
        y = jnp.concatenate(y_t, axis=0).T
        cols = slice(g * gw, (g + 1) * gw)
        y = y + xs_g * dskip_ref[:, cols]
        z = z_ref[0, 0, rq, cols].astype(F32)
        yg = y * (z * _sigmoid(z))
        yg = yg * lax.rsqrt(jnp.mean(yg * yg, axis=-1, keepdims=True) + EPS)
        ssm_ref[rq, cols] = (yg * nw_ref[:, cols]).astype(ssm_ref.dtype)
        yield

    slab_ref[:, 0:SUBLANES, :] = slab_ref[:, q:q + SUBLANES, :]


def _merge_ffn_stages(attn_refs, ssm_ref, gate_ref, h_ref, wa_ref, ws_ref, wo_ref, nat_ref,
                      nw_ref, wg_ref, wu_ref, wd_ref, fw_ref, out_ref, *, ffn_chunk, final_norm):
    tm = h_ref.shape[1]
    n_slabs = GROUP_WIDTH // LANES
    for k, src in enumerate(attn_refs):
        dil = src.shape[1]
        for r in range(dil if dil > 1 else 0):
            for c in range(n_slabs):
                nat_ref[k, c, pl.ds(r, tm // dil, stride=dil), :] = src[0, r, :, c * LANES:(c + 1) * LANES]

    def natural(k, c):
        src = attn_refs[k]
        return src[0, 0, :, c * LANES:(c + 1) * LANES] if src.shape[1] == 1 else nat_ref[k, c]

    slabs = []
    for c in range(n_slabs):
        o = [natural(2 * g, c) for g in range(N_DIL)]
        l = [natural(2 * g + 1, c) for g in range(N_DIL)]
        mx = functools.reduce(jnp.maximum, l)
        e = [jnp.exp(lg - mx) for lg in l]
        num = functools.reduce(jnp.add, [eg * og for eg, og in zip(e, o)])
        slabs.append((num / functools.reduce(jnp.add, e)).astype(BF16))
    attn = jnp.concatenate(slabs, axis=1)
    yield
    a = jnp.dot(attn, wa_ref[...], preferred_element_type=F32)
    yield
    s = jnp.dot(ssm_ref[...], ws_ref[...], preferred_element_type=F32)
    d = a.shape[1]
    gates = _sigmoid(gate_ref[0, 0].astype(F32))
    merged = (gates[:, :d] * a + gates[:, d:] * s).astype(BF16)
    yield
    acc = h_ref[0] + jnp.dot(merged, wo_ref[...], preferred_element_type=F32)
    xn = _rmsnorm(acc, nw_ref[...]).astype(BF16)
    yield
    chunks = [slice(c0, c0 + ffn_chunk) for c0 in range(0, wd_ref.shape[0], ffn_chunk)]
    gate_up = None
    for c, cols in enumerate(chunks + [None]):
        prev_cols, prev_gate_up = (chunks[c - 1], gate_up) if c > 0 else (None, None)
        if cols is not None:
            gate = jnp.dot(xn, wg_ref[:, cols], preferred_element_type=F32)
            yield
            gate_up = (gate, jnp.dot(xn, wu_ref[:, cols], preferred_element_type=F32))
            yield
        if prev_gate_up is not None:
            act = (prev_gate_up[0] * _sigmoid(prev_gate_up[0]) * prev_gate_up[1]).astype(BF16)
            acc = acc + jnp.dot(act, wd_ref[prev_cols, :], preferred_element_type=F32)
            yield
    out_ref[0] = _rmsnorm(acc, fw_ref[...]) if final_norm else acc


def _tail_kernel(*refs, ffn_chunk, ssd_stages_per_matmul, final_norm):
    attn_refs = refs[:2 * N_DIL]
    (gate_ref, h_ref, xs_ref, bc_ref, z_ref, dt_ref, wa_ref, ws_ref, wo_ref, n2_ref, wg_ref, wu_ref, wd_ref, fw_ref,
     cw_ref, cb_ref, dtb_ref, alog_ref, dskip_ref, nw_ref, out_ref,
     state_ref, slab_ref, ssm_ref, new_ref, nat_ref) = refs[2 * N_DIL:]

    @pl.when(pl.program_id(1) == 0)
    def _():
        state_ref[...] = jnp.zeros_like(state_ref)
        slab_ref[:, 0:SUBLANES, :] = jnp.zeros((slab_ref.shape[0], SUBLANES, LANES), F32)
        ssm_ref[...] = jnp.zeros_like(ssm_ref)

    def ssd_stages():
        for r0 in range(0, xs_ref.shape[2], SSD_CHUNK):
            yield from _ssd_chunk_stages(r0, xs_ref, bc_ref, z_ref, dt_ref, cw_ref, cb_ref, dtb_ref, alog_ref,
                                         dskip_ref, nw_ref, new_ref, state_ref, slab_ref)

    matmuls = _merge_ffn_stages(attn_refs, ssm_ref, gate_ref, h_ref, wa_ref, ws_ref, wo_ref, nat_ref,
                                n2_ref, wg_ref, wu_ref, wd_ref, fw_ref, out_ref,
                                ffn_chunk=ffn_chunk, final_norm=final_norm)
    scan = ssd_stages()
    live = [matmuls, scan]
    while live:
        for gen, reps in ((matmuls, 1), (scan, ssd_stages_per_matmul)):
            for _ in range(reps if gen in live else 0):
                if next(gen, StopIteration) is StopIteration:
                    live.remove(gen)
                    break
    ssm_ref[...] = new_ref[...]


def _tail(attn_outs, main, dt3, h3, weights, *, col_z, col_gates, col_xs, col_bc, tb, ffn_chunk,
          ssd_stages_per_matmul, final_norm):
    b, s, d = h3.shape
    (w_attn, w_ssm, w_out, norm2_w, w_gate, w_up, w_down, final_w, conv_w, conv_b, dt_bias, a_log, d_skip,
     norm_w) = weights
    d_inner = norm_w.shape[1]
    conv_ch = conv_w.shape[1]
    bc_w = conv_ch - d_inner
    n_blocks = s // tb

    def cur(k):
        return jnp.minimum(k, n_blocks - 1)

    def prev(k):
        return jnp.maximum(k - 1, 0)

    def residues(arr):
        dil = arr.shape[1]
        return pl.BlockSpec((1, dil, tb // dil, GROUP_WIDTH), lambda bi, k: (bi, 0, prev(k), 0))

    flat = [a for pair in attn_outs for a in pair]
    in_specs = [residues(a) for a in flat] + [
        pl.BlockSpec((1, 1, tb, 2 * d), lambda bi, k: (bi, 0, prev(k), col_gates)),
        pl.BlockSpec((1, tb, d), lambda bi, k: (bi, prev(k), 0)),
        pl.BlockSpec((1, 1, tb, d_inner), lambda bi, k: (bi, 0, cur(k), col_xs)),
        pl.BlockSpec((1, 1, tb, bc_w), lambda bi, k: (bi, 0, cur(k), col_bc)),
        pl.BlockSpec((1, 1, tb, d_inner), lambda bi, k: (bi, 0, cur(k), col_z)),
        pl.BlockSpec((1, tb, LANES), lambda bi, k: (bi, cur(k), 0)),
    ] + [_resident(w.shape) for w in weights]
    return pl.pallas_call(
        functools.partial(_tail_kernel, ffn_chunk=ffn_chunk, ssd_stages_per_matmul=ssd_stages_per_matmul,
                          final_norm=final_norm),
        grid=(b, n_blocks + 1),
        in_specs=in_specs,
        out_specs=pl.BlockSpec((1, tb, d), lambda bi, k: (bi, prev(k), 0)),
        out_shape=jax.ShapeDtypeStruct((b, s, d), F32),
        scratch_shapes=[pltpu.VMEM((N_SSM_GROUPS * HEADS_PER_SSM_GROUP * SSM_HEAD_DIM, D_STATE), F32),
                        pltpu.VMEM((conv_ch // LANES, SUBLANES + SSD_CHUNK, LANES), F32),
                        pltpu.VMEM((tb, d_inner), BF16),
                        pltpu.VMEM((tb, d_inner), BF16),
                        pltpu.VMEM((len(flat), GROUP_WIDTH // LANES, tb, LANES), F32)],
        compiler_params=_compiler_params(("parallel", "arbitrary")),
        name="layer_tail",
    )(*flat, main, h3, main, main, main, dt3, *weights)


def kernel(x, norm1_w, w_in, conv_w, conv_b, dt_bias, a_log, d_skip, ssm_norm_w, w_attn_branch, w_ssm_branch,
           w_out, norm2_w, w_ffn_in, w_ffn_out, rel_bias, final_norm_w):
    b, s, d = x.shape
    t = b * s
    depth = w_in.shape[0]
    d_inner = ssm_norm_w.shape[1]
    n_heads = dt_bias.shape[1]
    bc_w = 2 * N_SSM_GROUPS * D_STATE
    d_ff = w_ffn_out.shape[1]
    assert d_inner == N_SSM_GROUPS * HEADS_PER_SSM_GROUP * SSM_HEAD_DIM and n_heads <= LANES
    assert s % (DILATED_GROUPS[-1][1] * ATTN_BLOCK) == 0 and s % SSD_CHUNK == 0

    o_z = 3 * ATTN_WIDTH
    o_xbc = o_z + d_inner
    o_dt = o_xbc + d_inner + bc_w
    o_gate = o_dt + n_heads

    def qkv_cols(g):
        return [w_in[:, :, part * ATTN_WIDTH + g * GROUP_WIDTH:part * ATTN_WIDTH + (g + 1) * GROUP_WIDTH]
                for part in range(3)]

    w_main = jnp.concatenate([w_in[:, :, o_z:o_xbc], w_in[:, :, o_gate:], w_in[:, :, o_xbc:o_dt]] + qkv_cols(0),
                             axis=2).astype(BF16)
    w_qkv = [None] + [jnp.concatenate(qkv_cols(g), axis=2).astype(BF16) for g in range(1, N_DIL)]
    w_dt = jnp.pad(w_in[:, :, o_dt:o_gate], ((0, 0), (0, 0), (0, LANES - n_heads))).astype(BF16)
    col_z = 0
    col_gates = d_inner // (2 * d)
    col_xs = (d_inner + 2 * d) // d_inner
    col_bc = (2 * d_inner + 2 * d) // bc_w
    col_q = (2 * d_inner + 2 * d + bc_w) // GROUP_WIDTH
    assert d_inner % (2 * d) == 0 and (d_inner + 2 * d) % d_inner == 0 and (2 * d_inner + 2 * d) % bc_w == 0
    assert (2 * d_inner + 2 * d + bc_w) % GROUP_WIDTH == 0

    pad_h = ((0, 0), (0, LANES - n_heads))
    dt_bias_p = jnp.pad(dt_bias, pad_h)
    a_log_p = jnp.pad(a_log, pad_h)
    d_skip_x = jnp.repeat(d_skip, SSM_HEAD_DIM, axis=1)
    w_attn16, w_ssm16, w_out16 = w_attn_branch.astype(BF16), w_ssm_branch.astype(BF16), w_out.astype(BF16)
    w_gate16, w_up16 = w_ffn_in[:, :, :d_ff].astype(BF16), w_ffn_in[:, :, d_ff:].astype(BF16)
    w_down16 = w_ffn_out.astype(BF16)
    bias_tables = _attention_bias_tables(rel_bias)

    h = x
    for layer in range(depth):
        nw1 = norm1_w[layer:layer + 1]
        main, dt = _in_proj(h, nw1, w_main[layer], w_dt[layer], dilation=1, tm=512, chunk=1024)

        attn_outs = []
        for g, (_, dil) in enumerate(DILATED_GROUPS):
            if dil == 1:
                qkv, cq = main, col_q
            else:
                (qkv,), cq = _in_proj(h, nw1, w_qkv[g][layer], dilation=dil, tm=1024, chunk=GROUP_WIDTH), 0
            attn_outs.append(_attention(qkv, bias_tables[g], col_q=cq, col_k=cq + 1, col_v=cq + 2,
                                        tq=min(512, s // dil)))

        weights = (w_attn16[layer], w_ssm16[layer], w_out16[layer], norm2_w[layer:layer + 1], w_gate16[layer],
                   w_up16[layer], w_down16[layer], final_norm_w[None, :], conv_w[layer], conv_b[layer:layer + 1],
                   dt_bias_p[layer:layer + 1], a_log_p[layer:layer + 1], d_skip_x[layer:layer + 1],
                   ssm_norm_w[layer:layer + 1])
        h = _tail(attn_outs, main, dt, h, weights, col_z=col_z, col_gates=col_gates, col_xs=col_xs,
                  col_bc=col_bc, tb=256, ffn_chunk=256, ssd_stages_per_matmul=2,
                  final_norm=(layer == depth - 1))
    return h
```

```python
import functools
import math

import jax
import jax.numpy as jnp
from jax import lax
from jax.experimental import pallas as pl
from jax.experimental.pallas import tpu as pltpu

HEAD_DIM = 64
DILATED_GROUPS = ((128, 1), (512, 4), (2048, 16))
N_DIL = len(DILATED_GROUPS)
HEADS_PER_GROUP = 8
GROUP_WIDTH = HEADS_PER_GROUP * HEAD_DIM
ATTN_WIDTH = N_DIL * GROUP_WIDTH
ATTN_BLOCK = 128
N_REL_BUCKETS = 32
REL_MAX_DISTANCE = 2048
SSM_HEAD_DIM = 64
N_SSM_GROUPS = 4
HEADS_PER_SSM_GROUP = 8
D_STATE = 128
CONV_WIDTH = 4
SSD_CHUNK = 128
EPS = 1e-6

LANES = 128
SUBLANES = 8
VMEM_LIMIT_BYTES = 60000 * 1024

BF16 = jnp.bfloat16
F32 = jnp.float32
NT_DIMS = (((1,), (1,)), ((), ()))


def _compiler_params(semantics):
    return pltpu.CompilerParams(dimension_semantics=semantics, vmem_limit_bytes=VMEM_LIMIT_BYTES)


def _rmsnorm(x, w):
    return x * lax.rsqrt(jnp.mean(x * x, axis=-1, keepdims=True) + EPS) * w


def _sigmoid(x):
    return 1.0 / (1.0 + jnp.exp(-x))


PERM_BLOCK = 256


def _in_proj_kernel(x_ref, nw_ref, w_ref, *rest, dilation, with_dt, chunk):
    if with_dt:
        wdt_ref, out_ref, dt_ref, xn_ref = rest
    else:
        out_ref, xn_ref = rest
    tm = x_ref.shape[1]
    n = w_ref.shape[1]
    xn = _rmsnorm(x_ref[0], nw_ref[...]).astype(BF16)
    if with_dt:
        dt_ref[0] = jnp.dot(xn, wdt_ref[...], preferred_element_type=F32)
    if dilation == 1:
        xn_ref[...] = xn
    else:
        per = PERM_BLOCK // dilation
        i = lax.broadcasted_iota(jnp.int32, (PERM_BLOCK, PERM_BLOCK), 0)
        k = lax.broadcasted_iota(jnp.int32, (PERM_BLOCK, PERM_BLOCK), 1)
        src = (i & (per - 1)) * dilation + (i >> (per.bit_length() - 1))
        perm = jnp.where(k == src, 1.0, 0.0).astype(BF16)
        for u in range(tm // PERM_BLOCK):
            y = jnp.dot(perm, xn[u * PERM_BLOCK:(u + 1) * PERM_BLOCK], preferred_element_type=F32).astype(BF16)
            for r in range(dilation):
                dst = r * (tm // dilation) + u * per
                xn_ref[dst:dst + per, :] = y[r * per:(r + 1) * per]
    for c0 in range(0, n, chunk):
        c1 = min(c0 + chunk, n)
        res = jnp.dot(xn_ref[...], w_ref[:, c0:c1], preferred_element_type=F32).astype(out_ref.dtype)
        out_ref[0, :, :, c0:c1] = res.reshape(dilation, tm // dilation, c1 - c0)


def _resident(shape):
    zeros = (0,) * len(shape)
    return pl.BlockSpec(shape, lambda *_: zeros, pipeline_mode=pl.Buffered(1))


def _in_proj(h3, norm_w, w, w_dt=None, *, dilation, tm, chunk):
    b, s, d = h3.shape
    n = w.shape[1]
    with_dt = w_dt is not None
    in_specs = [pl.BlockSpec((1, tm, d), lambda bi, i: (bi, i, 0)), _resident((1, d)), _resident((d, n))]
    out_specs = [pl.BlockSpec((1, dilation, tm // dilation, n), lambda bi, i: (bi, 0, i, 0))]
    out_shape = [jax.ShapeDtypeStruct((b, dilation, s // dilation, n), BF16)]
    args = [h3, norm_w, w]
    if with_dt:
        in_specs.append(_resident((d, LANES)))
        out_specs.append(pl.BlockSpec((1, tm, LANES), lambda bi, i: (bi, i, 0)))
        out_shape.append(jax.ShapeDtypeStruct((b, s, LANES), F32))
        args.append(w_dt)
    return pl.pallas_call(
        functools.partial(_in_proj_kernel, dilation=dilation, with_dt=with_dt, chunk=chunk),
        grid=(b, s // tm),
        in_specs=in_specs,
        out_specs=out_specs,
        out_shape=out_shape,
        scratch_shapes=[pltpu.VMEM((tm, d), BF16)],
        compiler_params=_compiler_params(("parallel", "parallel")),
        name=f"in_proj_d{dilation}",
    )(*args)


def _attention_kernel(q_ref, kp_ref, kc_ref, vp_ref, vc_ref, bias_ref, o_ref, l_ref, kext_ref, vext_ref, *, tq):
    n = pl.program_id(2)
    kext_ref[0:ATTN_BLOCK] = kp_ref[0, 0]
    kext_ref[ATTN_BLOCK:] = kc_ref[0, 0]
    vext_ref[0:ATTN_BLOCK] = vp_ref[0, 0]
    vext_ref[ATTN_BLOCK:] = vc_ref[0, 0]
    low_half = lax.broadcasted_iota(jnp.int32, (1, LANES), 1) < HEAD_DIM

    for s in range(tq // ATTN_BLOCK):
        r0 = s * ATTN_BLOCK
        qs = q_ref[0, 0, pl.ds(r0, ATTN_BLOCK), :] * (HEAD_DIM ** -0.5)
        ks = kext_ref[pl.ds(r0, 2 * ATTN_BLOCK), :]
        vs = vext_ref[pl.ds(r0, 2 * ATTN_BLOCK), :]
        bidx = jnp.where(n == 0, 0, 1) if s == 0 else 1
        for p in range(GROUP_WIDTH // LANES):
            cols = slice(p * LANES, (p + 1) * LANES)
            qp, kp, vp = qs[:, cols], ks[:, cols], vs[:, cols]
            outs, lses = [], []
            for e in range(2):
                qe = jnp.where(low_half if e == 0 else jnp.logical_not(low_half), qp, jnp.zeros_like(qp))
                sc = lax.dot_general(qe, kp, NT_DIMS, preferred_element_type=F32) + bias_ref[bidx, 2 * p + e]
                m = jnp.max(sc, axis=1, keepdims=True)
                pe = jnp.exp(sc - m)
                den = jnp.sum(pe, axis=1, keepdims=True)
                num = jnp.dot(pe.astype(BF16), vp, preferred_element_type=F32)
                outs.append(num / den)
                lses.append(jnp.broadcast_to(m + jnp.log(den), (ATTN_BLOCK, LANES)))
            o_ref[0, 0, pl.ds(r0, ATTN_BLOCK), cols] = jnp.where(low_half, outs[0], outs[1])
            l_ref[0, 0, pl.ds(r0, ATTN_BLOCK), cols] = jnp.where(low_half, lses[0], lses[1])


def _attention(qkv, bias_g, *, col_q, col_k, col_v, tq):
    b, dilation, seg, _ = qkv.shape
    blocks_per_tile = tq // ATTN_BLOCK

    def cur(col):
        return pl.BlockSpec((1, 1, tq, GROUP_WIDTH), lambda bi, r, n: (bi, r, n, col))

    def prev(col):
        return pl.BlockSpec((1, 1, ATTN_BLOCK, GROUP_WIDTH),
                            lambda bi, r, n: (bi, r, jnp.maximum(n * blocks_per_tile - 1, 0), col))

    out_spec = pl.BlockSpec((1, 1, tq, GROUP_WIDTH), lambda bi, r, n: (bi, r, n, 0))
    out_sds = jax.ShapeDtypeStruct((b, dilation, seg, GROUP_WIDTH), F32)
    return pl.pallas_call(
        functools.partial(_attention_kernel, tq=tq),
        grid=(b, dilation, seg // tq),
        in_specs=[cur(col_q), prev(col_k), cur(col_k), prev(col_v), cur(col_v),
                  pl.BlockSpec(bias_g.shape, lambda bi, r, n: (0, 0, 0, 0))],
        out_specs=[out_spec, out_spec],
        out_shape=[out_sds, out_sds],
        scratch_shapes=[pltpu.VMEM((tq + ATTN_BLOCK, GROUP_WIDTH), BF16),
                        pltpu.VMEM((tq + ATTN_BLOCK, GROUP_WIDTH), BF16)],
        compiler_params=_compiler_params(("parallel", "parallel", "arbitrary")),
        name=f"attention_d{dilation}",
    )(qkv, qkv, qkv, qkv, qkv, bias_g)


def _t5_causal_bucket(dist):
    max_exact = N_REL_BUCKETS // 2
    d_f = jnp.maximum(dist, 1).astype(F32)
    large = max_exact + (jnp.log(d_f / max_exact) / math.log(REL_MAX_DISTANCE / max_exact)
                         * (N_REL_BUCKETS - max_exact)).astype(jnp.int32)
    large = jnp.minimum(large, N_REL_BUCKETS - 1)
    return jnp.where(dist < max_exact, dist, large)


def _attention_bias_tables(rel_bias):
    q, q2 = ATTN_BLOCK, 2 * ATTN_BLOCK
    steps = jnp.arange(q2) - (q - 1)
    in_prev = (jnp.arange(q2) < q)[None, None, :]
    tables = []
    for g, (window, dil) in enumerate(DILATED_GROUPS):
        n_steps = window // dil
        assert n_steps <= q
        rel_g = rel_bias[:, g * HEADS_PER_GROUP:(g + 1) * HEADS_PER_GROUP].astype(F32)
        vals = rel_g[_t5_causal_bucket(jnp.clip(steps, 0, n_steps) * dil)]
        vec = jnp.where(((steps >= 0) & (steps <= n_steps))[:, None], vals, -jnp.inf).T
        skew = jnp.tile(vec, (1, q + 1))[:, :q * (q2 + 1)].reshape(HEADS_PER_GROUP, q, q2 + 1)[:, :, :q2]
        rest = skew[:, :, ::-1]
        first = jnp.where(in_prev, -jnp.inf, rest)
        tables.append(jnp.stack([first, rest]))
    return jnp.stack(tables)


LOG2E = math.log2(math.e)


def _conv_silu(slab_ref, c, w_ref, b_ref):
    q = slab_ref.shape[1] - SUBLANES
    cols = slice(c * LANES, (c + 1) * LANES)
    acc = b_ref[:, cols] + slab_ref[c, pl.ds(SUBLANES, q), :] * w_ref[CONV_WIDTH - 1:CONV_WIDTH, cols]
    for s in range(1, CONV_WIDTH):
        acc = acc + slab_ref[c, pl.ds(SUBLANES - s, q), :] * w_ref[CONV_WIDTH - 1 - s:CONV_WIDTH - s, cols]
    return acc * _sigmoid(acc)


def _ssd_decays(r0, dt_ref, dtb_ref, alog_ref):
    q = SSD_CHUNK
    x_dt = dt_ref[0, r0:r0 + q, :] + dtb_ref[...]
    dt = jnp.maximum(x_dt, 0.0) + jnp.log1p(jnp.exp(-jnp.abs(x_dt)))
    d_a = dt * (-jnp.exp(alog_ref[...]))
    ri = lax.broadcasted_iota(jnp.int32, (q, q), 0)
    ci = lax.broadcasted_iota(jnp.int32, (q, q), 1)
    causal = ri >= ci
    tril = jnp.where(causal, 1.0, 0.0).astype(BF16)
    d1 = d_a.astype(BF16)
    r1 = d_a - d1.astype(F32)
    d2 = r1.astype(BF16)
    d3 = (r1 - d2.astype(F32)).astype(BF16)
    la = (jnp.dot(tril, d1, preferred_element_type=F32) + jnp.dot(tril, d2, preferred_element_type=F32)
          + jnp.dot(tril, d3, preferred_element_type=F32))
    la = la * LOG2E
    la_t = la.T
    dt_t = dt.T
    last = jnp.broadcast_to(la_t[:, q - 1:q], (q, q))
    w_t = jnp.exp2(last - la_t) * dt_t
    state_decay = jnp.exp2(last)
    return causal, la, la_t, dt_t, w_t, state_decay


def _ssd_chunk_stages(r0, decays, xs_ref, bc_ref, z_ref, cw_ref, cb_ref, dskip_ref, nw_ref,
                      ssm_ref, state_ref, slab_ref):
    causal, la, la_t, dt_t, w_t, state_decay = decays
    q = SSD_CHUNK
    rq = slice(r0, r0 + q)
    gw = HEADS_PER_SSM_GROUP * SSM_HEAD_DIM
    xs_slabs = xs_ref.shape[3] // LANES
    slabs_per_group = gw // LANES

    for c in range(xs_slabs):
        slab_ref[c, SUBLANES:, :] = xs_ref[0, 0, rq, c * LANES:(c + 1) * LANES].astype(F32)
    for c in range(bc_ref.shape[3] // LANES):
        slab_ref[xs_slabs + c, SUBLANES:, :] = bc_ref[0, 0, rq, c * LANES:(c + 1) * LANES].astype(F32)

    for g in range(N_SSM_GROUPS):
        bm = _conv_silu(slab_ref, xs_slabs + g, cw_ref, cb_ref)
        cm = _conv_silu(slab_ref, xs_slabs + N_SSM_GROUPS + g, cw_ref, cb_ref)
        bm16 = bm.astype(BF16)
        cb = lax.dot_general(cm.astype(BF16), bm16, NT_DIMS, preferred_element_type=F32)
        xs_g = jnp.concatenate([_conv_silu(slab_ref, g * slabs_per_group + c, cw_ref, cb_ref)
                                for c in range(slabs_per_group)], axis=1)
        xs_t = xs_g.T
        yield
        y_t = []
        for h in range(HEADS_PER_SSM_GROUP):
            hh = g * HEADS_PER_SSM_GROUP + h
            rows = slice(hh * SSM_HEAD_DIM, (hh + 1) * SSM_HEAD_DIM)
            la_i = jnp.broadcast_to(la[:, hh:hh + 1], (q, q))
            decay = jnp.exp2(jnp.where(causal, la_i - la_t[hh:hh + 1, :], -jnp.inf))
            lhs = jnp.concatenate([(cb * decay).astype(BF16), (cm * jnp.exp2(la_i)).astype(BF16)], axis=1)
            x_h = xs_t[h * SSM_HEAD_DIM:(h + 1) * SSM_HEAD_DIM]
            state = state_ref[rows, :]
            rhs_t = jnp.concatenate([(x_h * dt_t[hh:hh + 1, :]).astype(BF16), state.astype(BF16)], axis=1)
            y_t.append(lax.dot_general(rhs_t, lhs, NT_DIMS, preferred_element_type=F32))
            yield
        heads = range(g * HEADS_PER_SSM_GROUP, (g + 1) * HEADS_PER_SSM_GROUP)
        grows = slice(heads[0] * SSM_HEAD_DIM, (heads[-1] + 1) * SSM_HEAD_DIM)
        per_head = lambda t: jnp.concatenate(
            [jnp.broadcast_to(t[hh:hh + 1, :], (SSM_HEAD_DIM, q)) for hh in heads], axis=0)
        upd = jnp.dot((xs_t * per_head(w_t)).astype(BF16), bm16, preferred_element_type=F32)
        state_ref[grows, :] = state_ref[grows, :] * per_head(state_decay) + upd
        yield
        y = jnp.concatenate(y_t, axis=0).T
        cols = slice(g * gw, (g + 1) * gw)
        y = y + xs_g * dskip_ref[:, cols]
        z = z_ref[0, 0, rq, cols].astype(F32)
        yg = y * (z * _sigmoid(z))
        yg = yg * lax.rsqrt(jnp.mean(yg * yg, axis=-1, keepdims=True) + EPS)
        ssm_ref[rq, cols] = (yg * nw_ref[:, cols]).astype(ssm_ref.dtype)
        yield

    slab_ref[:, 0:SUBLANES, :] = slab_ref[:, q:q + SUBLANES, :]


def _merge_ffn_stages(attn_refs, ssm_ref, gate_ref, h_ref, wa_ref, ws_ref, wo_ref, nat_ref,
                      nw_ref, wg_ref, wu_ref, wd_ref, fw_ref, out_ref, *, ffn_chunk, final_norm):
    tm = h_ref.shape[1]
    n_slabs = GROUP_WIDTH // LANES
    for k, src in enumerate(attn_refs):
        dil = src.shape[1]
        for r in range(dil if dil > 1 else 0):
            for c in range(n_slabs):
                nat_ref[k, c, pl.ds(r, tm // dil, stride=dil), :] = src[0, r, :, c * LANES:(c + 1) * LANES]

    def natural(k, c):
        src = attn_refs[k]
        return src[0, 0, :, c * LANES:(c + 1) * LANES] if src.shape[1] == 1 else nat_ref[k, c]

    slabs = []
    for c in range(n_slabs):
        o = [natural(2 * g, c) for g in range(N_DIL)]
        l = [natural(2 * g + 1, c) for g in range(N_DIL)]
        mx = functools.reduce(jnp.maximum, l)
        e = [jnp.exp(lg - mx) for lg in l]
        num = functools.reduce(jnp.add, [eg * og for eg, og in zip(e, o)])
        slabs.append((num / functools.reduce(jnp.add, e)).astype(BF16))
    attn = jnp.concatenate(slabs, axis=1)
    yield
    a = jnp.dot(attn, wa_ref[...], preferred_element_type=F32)
    yield
    s = jnp.dot(ssm_ref[...], ws_ref[...], preferred_element_type=F32)
    d = a.shape[1]
    gates = _sigmoid(gate_ref[0, 0].astype(F32))
    merged = (gates[:, :d] * a + gates[:, d:] * s).astype(BF16)
    yield
    acc = h_ref[0] + jnp.dot(merged, wo_ref[...], preferred_element_type=F32)
    xn = _rmsnorm(acc, nw_ref[...]).astype(BF16)
    yield
    chunks = [slice(c0, c0 + ffn_chunk) for c0 in range(0, wd_ref.shape[0], ffn_chunk)]
    gate_up = None
    for c, cols in enumerate(chunks + [None]):
        prev_cols, prev_gate_up = (chunks[c - 1], gate_up) if c > 0 else (None, None)
        if cols is not None:
            gate = jnp.dot(xn, wg_ref[:, cols], preferred_element_type=F32)
            yield
            gate_up = (gate, jnp.dot(xn, wu_ref[:, cols], preferred_element_type=F32))
            yield
        if prev_gate_up is not None:
            act = (prev_gate_up[0] * _sigmoid(prev_gate_up[0]) * prev_gate_up[1]).astype(BF16)
            acc = acc + jnp.dot(act, wd_ref[prev_cols, :], preferred_element_type=F32)
            yield
    out_ref[0] = _rmsnorm(acc, fw_ref[...]) if final_norm else acc


def _tail_kernel(*refs, ffn_chunk, ssd_stages_per_matmul, final_norm):
    attn_refs = refs[:2 * N_DIL]
    (gate_ref, h_ref, xs_ref, bc_ref, z_ref, dt_ref, wa_ref, ws_ref, wo_ref, n2_ref, wg_ref, wu_ref, wd_ref, fw_ref,
     cw_ref, cb_ref, dtb_ref, alog_ref, dskip_ref, nw_ref, out_ref,
     state_ref, slab_ref, ssm_ref, new_ref, nat_ref) = refs[2 * N_DIL:]

    @pl.when(pl.program_id(1) == 0)
    def _():
        state_ref[...] = jnp.zeros_like(state_ref)
        slab_ref[:, 0:SUBLANES, :] = jnp.zeros((slab_ref.shape[0], SUBLANES, LANES), F32)
        ssm_ref[...] = jnp.zeros_like(ssm_ref)

    def ssd_stages():
        for r0 in range(0, xs_ref.shape[2], SSD_CHUNK):
            decays = _ssd_decays(r0, dt_ref, dtb_ref, alog_ref)
            yield
            yield from _ssd_chunk_stages(r0, decays, xs_ref, bc_ref, z_ref, cw_ref, cb_ref, dskip_ref, nw_ref,
                                         new_ref, state_ref, slab_ref)

    matmuls = _merge_ffn_stages(attn_refs, ssm_ref, gate_ref, h_ref, wa_ref, ws_ref, wo_ref, nat_ref,
                                n2_ref, wg_ref, wu_ref, wd_ref, fw_ref, out_ref,
                                ffn_chunk=ffn_chunk, final_norm=final_norm)
    scan = ssd_stages()
    live = [matmuls, scan]
    while live:
        for gen, reps in ((matmuls, 1), (scan, ssd_stages_per_matmul)):
            for _ in range(reps if gen in live else 0):
                if next(gen, StopIteration) is StopIteration:
                    live.remove(gen)
                    break
    ssm_ref[...] = new_ref[...]


def _tail(attn_outs, main, dt3, h3, weights, *, col_z, col_gates, col_xs, col_bc, tb, ffn_chunk,
          ssd_stages_per_matmul, final_norm):
    b, s, d = h3.shape
    (w_attn, w_ssm, w_out, norm2_w, w_gate, w_up, w_down, final_w, conv_w, conv_b, dt_bias, a_log, d_skip,
     norm_w) = weights
    d_inner = norm_w.shape[1]
    conv_ch = conv_w.shape[1]
    bc_w = conv_ch - d_inner
    n_blocks = s // tb

    def cur(k):
        return jnp.minimum(k, n_blocks - 1)

    def prev(k):
        return jnp.maximum(k - 1, 0)

    def residues(arr):
        dil = arr.shape[1]
        return pl.BlockSpec((1, dil, tb // dil, GROUP_WIDTH), lambda bi, k: (bi, 0, prev(k), 0))

    flat = [a for pair in attn_outs for a in pair]
    in_specs = [residues(a) for a in flat] + [
        pl.BlockSpec((1, 1, tb, 2 * d), lambda bi, k: (bi, 0, prev(k), col_gates)),
        pl.BlockSpec((1, tb, d), lambda bi, k: (bi, prev(k), 0)),
        pl.BlockSpec((1, 1, tb, d_inner), lambda bi, k: (bi, 0, cur(k), col_xs)),
        pl.BlockSpec((1, 1, tb, bc_w), lambda bi, k: (bi, 0, cur(k), col_bc)),
        pl.BlockSpec((1, 1, tb, d_inner), lambda bi, k: (bi, 0, cur(k), col_z)),
        pl.BlockSpec((1, tb, LANES), lambda bi, k: (bi, cur(k), 0)),
    ] + [_resident(w.shape) for w in weights]
    return pl.pallas_call(
        functools.partial(_tail_kernel, ffn_chunk=ffn_chunk, ssd_stages_per_matmul=ssd_stages_per_matmul,
                          final_norm=final_norm),
        grid=(b, n_blocks + 1),
        in_specs=in_specs,
        out_specs=pl.BlockSpec((1, tb, d), lambda bi, k: (bi, prev(k), 0)),
        out_shape=jax.ShapeDtypeStruct((b, s, d), F32),
        scratch_shapes=[pltpu.VMEM((N_SSM_GROUPS * HEADS_PER_SSM_GROUP * SSM_HEAD_DIM, D_STATE), F32),
                        pltpu.VMEM((conv_ch // LANES, SUBLANES + SSD_CHUNK, LANES), F32),
                        pltpu.VMEM((tb, d_inner), BF16),
                        pltpu.VMEM((tb, d_inner), BF16),
                        pltpu.VMEM((len(flat), GROUP_WIDTH // LANES, tb, LANES), F32)],
        compiler_params=_compiler_params(("parallel", "arbitrary")),
        name="layer_tail",
    )(*flat, main, h3, main, main, main, dt3, *weights)


def kernel(x, norm1_w, w_in, conv_w, conv_b, dt_bias, a_log, d_skip, ssm_norm_w, w_attn_branch, w_ssm_branch,
           w_out, norm2_w, w_ffn_in, w_ffn_out, rel_bias, final_norm_w):
    b, s, d = x.shape
    t = b * s
    depth = w_in.shape[0]
    d_inner = ssm_norm_w.shape[1]
    n_heads = dt_bias.shape[1]
    bc_w = 2 * N_SSM_GROUPS * D_STATE
    d_ff = w_ffn_out.shape[1]
    assert d_inner == N_SSM_GROUPS * HEADS_PER_SSM_GROUP * SSM_HEAD_DIM and n_heads <= LANES
    assert s % (DILATED_GROUPS[-1][1] * ATTN_BLOCK) == 0 and s % SSD_CHUNK == 0

    o_z = 3 * ATTN_WIDTH
    o_xbc = o_z + d_inner
    o_dt = o_xbc + d_inner + bc_w
    o_gate = o_dt + n_heads

    def qkv_cols(g):
        return [w_in[:, :, part * ATTN_WIDTH + g * GROUP_WIDTH:part * ATTN_WIDTH + (g + 1) * GROUP_WIDTH]
                for part in range(3)]

    w_main = jnp.concatenate([w_in[:, :, o_z:o_xbc], w_in[:, :, o_gate:], w_in[:, :, o_xbc:o_dt]] + qkv_cols(0),
                             axis=2).astype(BF16)
    w_qkv = [None] + [jnp.concatenate(qkv_cols(g), axis=2).astype(BF16) for g in range(1, N_DIL)]
    w_dt = jnp.pad(w_in[:, :, o_dt:o_gate], ((0, 0), (0, 0), (0, LANES - n_heads))).astype(BF16)
    col_z = 0
    col_gates = d_inner // (2 * d)
    col_xs = (d_inner + 2 * d) // d_inner
    col_bc = (2 * d_inner + 2 * d) // bc_w
    col_q = (2 * d_inner + 2 * d + bc_w) // GROUP_WIDTH
    assert d_inner % (2 * d) == 0 and (d_inner + 2 * d) % d_inner == 0 and (2 * d_inner + 2 * d) % bc_w == 0
    assert (2 * d_inner + 2 * d + bc_w) % GROUP_WIDTH == 0

    pad_h = ((0, 0), (0, LANES - n_heads))
    dt_bias_p = jnp.pad(dt_bias, pad_h)
    a_log_p = jnp.pad(a_log, pad_h)
    d_skip_x = jnp.repeat(d_skip, SSM_HEAD_DIM, axis=1)
    w_attn16, w_ssm16, w_out16 = w_attn_branch.astype(BF16), w_ssm_branch.astype(BF16), w_out.astype(BF16)
    w_gate16, w_up16 = w_ffn_in[:, :, :d_ff].astype(BF16), w_ffn_in[:, :, d_ff:].astype(BF16)
    w_down16 = w_ffn_out.astype(BF16)
    bias_tables = _attention_bias_tables(rel_bias)

    h = x
    for layer in range(depth):
        nw1 = norm1_w[layer:layer + 1]
        main, dt = _in_proj(h, nw1, w_main[layer], w_dt[layer], dilation=1, tm=512, chunk=1024)

        attn_outs = []
        for g, (_, dil) in enumerate(DILATED_GROUPS):
            if dil == 1:
                qkv, cq = main, col_q
            else:
                (qkv,), cq = _in_proj(h, nw1, w_qkv[g][layer], dilation=dil, tm=1024, chunk=GROUP_WIDTH), 0
            attn_outs.append(_attention(qkv, bias_tables[g], col_q=cq, col_k=cq + 1, col_v=cq + 2,
                                        tq=min(512, s // dil)))

        weights = (w_attn16[layer], w_ssm16[layer], w_out16[layer], norm2_w[layer:layer + 1], w_gate16[layer],
                   w_up16[layer], w_down16[layer], final_norm_w[None, :], conv_w[layer], conv_b[layer:layer + 1],
                   dt_bias_p[layer:layer + 1], a_log_p[layer:layer + 1], d_skip_x[layer:layer + 1],
                   ssm_norm_w[layer:layer + 1])
        h = _tail(attn_outs, main, dt, h, weights, col_z=col_z, col_gates=col_gates, col_xs=col_xs,
                  col_bc=col_bc, tb=256, ffn_chunk=256, ssd_stages_per_matmul=2,
                  final_norm=(layer == depth - 1))
    return h
```

```python
import functools
import math

import jax
import jax.numpy as jnp
from jax import lax
from jax.experimental import pallas as pl
from jax.experimental.pallas import tpu as pltpu

HEAD_DIM = 64
DILATED_GROUPS = ((128, 1), (512, 4), (2048, 16))
N_DIL = len(DILATED_GROUPS)
HEADS_PER_GROUP = 8
GROUP_WIDTH = HEADS_PER_GROUP * HEAD_DIM
ATTN_WIDTH = N_DIL * GROUP_WIDTH
ATTN_BLOCK = 128
N_REL_BUCKETS = 32
REL_MAX_DISTANCE = 2048
SSM_HEAD_DIM = 64
N_SSM_GROUPS = 4
HEADS_PER_SSM_GROUP = 8
D_STATE = 128
CONV_WIDTH = 4
SSD_CHUNK = 128
EPS = 1e-6

LANES = 128
SUBLANES = 8
VMEM_LIMIT_BYTES = 60000 * 1024

BF16 = jnp.bfloat16
F32 = jnp.float32
NT_DIMS = (((1,), (1,)), ((), ()))


def _compiler_params(semantics):
    return pltpu.CompilerParams(dimension_semantics=semantics, vmem_limit_bytes=VMEM_LIMIT_BYTES)


def _rmsnorm(x, w):
    return x * lax.rsqrt(jnp.mean(x * x, axis=-1, keepdims=True) + EPS) * w


def _sigmoid(x):
    return 1.0 / (1.0 + jnp.exp(-x))


PERM_BLOCK = 256


def _in_proj_kernel(x_ref, nw_ref, w_ref, *rest, dilation, with_dt, chunk):
    if with_dt:
        wdt_ref, out_ref, dt_ref, xn_ref = rest
    else:
        out_ref, xn_ref = rest
    tm = x_ref.shape[1]
    n = w_ref.shape[1]
    xn = _rmsnorm(x_ref[0], nw_ref[...]).astype(BF16)
    if with_dt:
        dt_ref[0] = jnp.dot(xn, wdt_ref[...], preferred_element_type=F32)
    if dilation == 1:
        xn_ref[...] = xn
    else:
        per = PERM_BLOCK // dilation
        i = lax.broadcasted_iota(jnp.int32, (PERM_BLOCK, PERM_BLOCK), 0)
        k = lax.broadcasted_iota(jnp.int32, (PERM_BLOCK, PERM_BLOCK), 1)
        src = (i & (per - 1)) * dilation + (i >> (per.bit_length() - 1))
        perm = jnp.where(k == src, 1.0, 0.0).astype(BF16)
        for u in range(tm // PERM_BLOCK):
            y = jnp.dot(perm, xn[u * PERM_BLOCK:(u + 1) * PERM_BLOCK], preferred_element_type=F32).astype(BF16)
            for r in range(dilation):
                dst = r * (tm // dilation) + u * per
                xn_ref[dst:dst + per, :] = y[r * per:(r + 1) * per]
    for c0 in range(0, n, chunk):
        c1 = min(c0 + chunk, n)
        res = jnp.dot(xn_ref[...], w_ref[:, c0:c1], preferred_element_type=F32).astype(out_ref.dtype)
        out_ref[0, :, :, c0:c1] = res.reshape(dilation, tm // dilation, c1 - c0)


def _resident(shape):
    zeros = (0,) * len(shape)
    return pl.BlockSpec(shape, lambda *_: zeros, pipeline_mode=pl.Buffered(1))


def _in_proj(h3, norm_w, w, w_dt=None, *, dilation, tm, chunk):
    b, s, d = h3.shape
    n = w.shape[1]
    with_dt = w_dt is not None
    in_specs = [pl.BlockSpec((1, tm, d), lambda bi, i: (bi, i, 0)), _resident((1, d)), _resident((d, n))]
    out_specs = [pl.BlockSpec((1, dilation, tm // dilation, n), lambda bi, i: (bi, 0, i, 0))]
    out_shape = [jax.ShapeDtypeStruct((b, dilation, s // dilation, n), BF16)]
    args = [h3, norm_w, w]
    if with_dt:
        in_specs.append(_resident((d, LANES)))
        out_specs.append(pl.BlockSpec((1, tm, LANES), lambda bi, i: (bi, i, 0)))
        out_shape.append(jax.ShapeDtypeStruct((b, s, LANES), F32))
        args.append(w_dt)
    return pl.pallas_call(
        functools.partial(_in_proj_kernel, dilation=dilation, with_dt=with_dt, chunk=chunk),
        grid=(b, s // tm),
        in_specs=in_specs,
        out_specs=out_specs,
        out_shape=out_shape,
        scratch_shapes=[pltpu.VMEM((tm, d), BF16)],
        compiler_params=_compiler_params(("parallel", "parallel")),
        name=f"in_proj_d{dilation}",
    )(*args)


def _attention_kernel(q_ref, kp_ref, kc_ref, vp_ref, vc_ref, bias_ref, o_ref, l_ref, kext_ref, vext_ref, *, tq):
    n = pl.program_id(2)
    kext_ref[0:ATTN_BLOCK] = kp_ref[0, 0]
    kext_ref[ATTN_BLOCK:] = kc_ref[0, 0]
    vext_ref[0:ATTN_BLOCK] = vp_ref[0, 0]
    vext_ref[ATTN_BLOCK:] = vc_ref[0, 0]
    low_half = lax.broadcasted_iota(jnp.int32, (1, LANES), 1) < HEAD_DIM

    for s in range(tq // ATTN_BLOCK):
        r0 = s * ATTN_BLOCK
        qs = q_ref[0, 0, pl.ds(r0, ATTN_BLOCK), :] * (HEAD_DIM ** -0.5)
        ks = kext_ref[pl.ds(r0, 2 * ATTN_BLOCK), :]
        vs = vext_ref[pl.ds(r0, 2 * ATTN_BLOCK), :]
        bidx = jnp.where(n == 0, 0, 1) if s == 0 else 1
        for p in range(GROUP_WIDTH // LANES):
            cols = slice(p * LANES, (p + 1) * LANES)
            qp, kp, vp = qs[:, cols], ks[:, cols], vs[:, cols]
            outs, lses = [], []
            for e in range(2):
                qe = jnp.where(low_half if e == 0 else jnp.logical_not(low_half), qp, jnp.zeros_like(qp))
                sc = lax.dot_general(qe, kp, NT_DIMS, preferred_element_type=F32) + bias_ref[bidx, 2 * p + e]
                m = jnp.max(sc, axis=1, keepdims=True)
                pe = jnp.exp(sc - m)
                den = jnp.sum(pe, axis=1, keepdims=True)
                num = jnp.dot(pe.astype(BF16), vp, preferred_element_type=F32)
                outs.append(num / den)
                lses.append(jnp.broadcast_to(m + jnp.log(den), (ATTN_BLOCK, LANES)))
            o_ref[0, 0, pl.ds(r0, ATTN_BLOCK), cols] = jnp.where(low_half, outs[0], outs[1])
            l_ref[0, 0, pl.ds(r0, ATTN_BLOCK), cols] = jnp.where(low_half, lses[0], lses[1])


def _attention(qkv, bias_g, *, col_q, col_k, col_v, tq):
    b, dilation, seg, _ = qkv.shape
    blocks_per_tile = tq // ATTN_BLOCK

    def cur(col):
        return pl.BlockSpec((1, 1, tq, GROUP_WIDTH), lambda bi, r, n: (bi, r, n, col))

    def prev(col):
        return pl.BlockSpec((1, 1, ATTN_BLOCK, GROUP_WIDTH),
                            lambda bi, r, n: (bi, r, jnp.maximum(n * blocks_per_tile - 1, 0), col))

    out_spec = pl.BlockSpec((1, 1, tq, GROUP_WIDTH), lambda bi, r, n: (bi, r, n, 0))
    out_sds = jax.ShapeDtypeStruct((b, dilation, seg, GROUP_WIDTH), F32)
    return pl.pallas_call(
        functools.partial(_attention_kernel, tq=tq),
        grid=(b, dilation, seg // tq),
        in_specs=[cur(col_q), prev(col_k), cur(col_k), prev(col_v), cur(col_v),
                  pl.BlockSpec(bias_g.shape, lambda bi, r, n: (0, 0, 0, 0))],
        out_specs=[out_spec, out_spec],
        out_shape=[out_sds, out_sds],
        scratch_shapes=[pltpu.VMEM((tq + ATTN_BLOCK, GROUP_WIDTH), BF16),
                        pltpu.VMEM((tq + ATTN_BLOCK, GROUP_WIDTH), BF16)],
        compiler_params=_compiler_params(("parallel", "parallel", "arbitrary")),
        name=f"attention_d{dilation}",
    )(qkv, qkv, qkv, qkv, qkv, bias_g)


def _t5_causal_bucket(dist):
    max_exact = N_REL_BUCKETS // 2
    d_f = jnp.maximum(dist, 1).astype(F32)
    large = max_exact + (jnp.log(d_f / max_exact) / math.log(REL_MAX_DISTANCE / max_exact)
                         * (N_REL_BUCKETS - max_exact)).astype(jnp.int32)
    large = jnp.minimum(large, N_REL_BUCKETS - 1)
    return jnp.where(dist < max_exact, dist, large)


def _attention_bias_tables(rel_bias):
    q, q2 = ATTN_BLOCK, 2 * ATTN_BLOCK
    steps = jnp.arange(q2) - (q - 1)
    in_prev = (jnp.arange(q2) < q)[None, None, :]
    tables = []
    for g, (window, dil) in enumerate(DILATED_GROUPS):
        n_steps = window // dil
        assert n_steps <= q
        rel_g = rel_bias[:, g * HEADS_PER_GROUP:(g + 1) * HEADS_PER_GROUP].astype(F32)
        vals = rel_g[_t5_causal_bucket(jnp.clip(steps, 0, n_steps) * dil)]
        vec = jnp.where(((steps >= 0) & (steps <= n_steps))[:, None], vals, -jnp.inf).T
        skew = jnp.tile(vec, (1, q + 1))[:, :q * (q2 + 1)].reshape(HEADS_PER_GROUP, q, q2 + 1)[:, :, :q2]
        rest = skew[:, :, ::-1]
        first = jnp.where(in_prev, -jnp.inf, rest)
        tables.append(jnp.stack([first, rest]))
    return jnp.stack(tables)


LOG2E = math.log2(math.e)


def _conv_silu(slab_ref, c, w_ref, b_ref):
    q = slab_ref.shape[1] - SUBLANES
    cols = slice(c * LANES, (c + 1) * LANES)
    acc = b_ref[:, cols] + slab_ref[c, pl.ds(SUBLANES, q), :] * w_ref[CONV_WIDTH - 1:CONV_WIDTH, cols]
    for s in range(1, CONV_WIDTH):
        acc = acc + slab_ref[c, pl.ds(SUBLANES - s, q), :] * w_ref[CONV_WIDTH - 1 - s:CONV_WIDTH - s, cols]
    return acc * _sigmoid(acc)


def _ssd_decays(r0, dt_ref, dtb_ref, alog_ref):
    q = SSD_CHUNK
    x_dt = dt_ref[0, r0:r0 + q, :] + dtb_ref[...]
    dt = jnp.maximum(x_dt, 0.0) + jnp.log(1.0 + jnp.exp(-jnp.abs(x_dt)))
    d_a = dt * (-jnp.exp(alog_ref[...]))
    ri = lax.broadcasted_iota(jnp.int32, (q, q), 0)
    ci = lax.broadcasted_iota(jnp.int32, (q, q), 1)
    causal_t = ri <= ci
    tril = jnp.where(ri >= ci, 1.0, 0.0).astype(BF16)
    d1 = d_a.astype(BF16)
    r1 = d_a - d1.astype(F32)
    d2 = r1.astype(BF16)
    d3 = (r1 - d2.astype(F32)).astype(BF16)
    la = (jnp.dot(tril, d1, preferred_element_type=F32) + jnp.dot(tril, d2, preferred_element_type=F32)
          + jnp.dot(tril, d3, preferred_element_type=F32))
    la = la * LOG2E
    la_t = la.T
    dt_t = dt.T
    last = jnp.broadcast_to(la_t[:, q - 1:q], (q, q))
    w_t = jnp.exp2(last - la_t) * dt_t
    state_decay = jnp.exp2(last)
    return causal_t, la, la_t, dt_t, w_t, state_decay


def _ssd_chunk_stages(r0, decays, xs_ref, bc_ref, z_ref, cw_ref, cb_ref, dskip_ref, nw_ref,
                      ssm_ref, state_ref, slab_ref):
    causal_t, la, la_t, dt_t, w_t, state_decay = decays
    q = SSD_CHUNK
    rq = slice(r0, r0 + q)
    gw = HEADS_PER_SSM_GROUP * SSM_HEAD_DIM
    xs_slabs = xs_ref.shape[3] // LANES
    slabs_per_group = gw // LANES

    for c in range(xs_slabs):
        slab_ref[c, SUBLANES:, :] = xs_ref[0, 0, rq, c * LANES:(c + 1) * LANES].astype(F32)
    for c in range(bc_ref.shape[3] // LANES):
        slab_ref[xs_slabs + c, SUBLANES:, :] = bc_ref[0, 0, rq, c * LANES:(c + 1) * LANES].astype(F32)

    for g in range(N_SSM_GROUPS):
        bm = _conv_silu(slab_ref, xs_slabs + g, cw_ref, cb_ref)
        cm = _conv_silu(slab_ref, xs_slabs + N_SSM_GROUPS + g, cw_ref, cb_ref)
        bm16 = bm.astype(BF16)
        cb_t = lax.dot_general(bm16, cm.astype(BF16), NT_DIMS, preferred_element_type=F32)
        cm_t = cm.T
        xs_g = jnp.concatenate([_conv_silu(slab_ref, g * slabs_per_group + c, cw_ref, cb_ref)
                                for c in range(slabs_per_group)], axis=1)
        xs_t = xs_g.T
        yield
        y_t = []
        for h in range(HEADS_PER_SSM_GROUP):
            hh = g * HEADS_PER_SSM_GROUP + h
            rows = slice(hh * SSM_HEAD_DIM, (hh + 1) * SSM_HEAD_DIM)
            la_i = la_t[hh:hh + 1, :]
            la_j = jnp.broadcast_to(la[:, hh:hh + 1], (q, q))
            decay_t = jnp.exp2(jnp.where(causal_t, la_i - la_j, -jnp.inf))
            rhs = jnp.concatenate([(cb_t * decay_t).astype(BF16), (cm_t * jnp.exp2(la_i)).astype(BF16)], axis=0)
            x_h = xs_t[h * SSM_HEAD_DIM:(h + 1) * SSM_HEAD_DIM]
            state = state_ref[rows, :]
            lhs = jnp.concatenate([(x_h * dt_t[hh:hh + 1, :]).astype(BF16), state.astype(BF16)], axis=1)
            y_t.append(jnp.dot(lhs, rhs, preferred_element_type=F32))
            yield
        heads = range(g * HEADS_PER_SSM_GROUP, (g + 1) * HEADS_PER_SSM_GROUP)
        grows = slice(heads[0] * SSM_HEAD_DIM, (heads[-1] + 1) * SSM_HEAD_DIM)
        per_head = lambda t: jnp.concatenate(
            [jnp.broadcast_to(t[hh:hh + 1, :], (SSM_HEAD_DIM, q)) for hh in heads], axis=0)
        upd = jnp.dot((xs_t * per_head(w_t)).astype(BF16), bm16, preferred_element_type=F32)
        state_ref[grows, :] = state_ref[grows, :] * per_head(state_decay) + upd
        yield
        y = jnp.concatenate(y_t, axis=0).T
        cols = slice(g * gw, (g + 1) * gw)
        y = y + xs_g * dskip_ref[:, cols]
        z = z_ref[0, 0, rq, cols].astype(F32)
        yg = y * (z * _sigmoid(z))
        yg = yg * lax.rsqrt(jnp.mean(yg * yg, axis=-1, keepdims=True) + EPS)
        ssm_ref[rq, cols] = (yg * nw_ref[:, cols]).astype(ssm_ref.dtype)
        yield

    slab_ref[:, 0:SUBLANES, :] = slab_ref[:, q:q + SUBLANES, :]


def _merge_ffn_stages(r0, tm, attn_refs, ssm_ref, gate_ref, h_ref, wa_ref, ws_ref, wo_ref, nat_ref,
                      nw_ref, wg_ref, wu_ref, wd_ref, fw_ref, out_ref, *, ffn_chunk, final_norm):
    rows = slice(r0, r0 + tm)
    n_slabs = GROUP_WIDTH // LANES
    for k, src in enumerate(attn_refs):
        dil = src.shape[1]
        for r in range(dil if dil > 1 else 0):
            for c in range(n_slabs):
                nat_ref[k, c, pl.ds(r0 + r, tm // dil, stride=dil), :] = (
                    src[0, r, r0 // dil:(r0 + tm) // dil, c * LANES:(c + 1) * LANES])

    def natural(k, c):
        src = attn_refs[k]
        return src[0, 0, rows, c * LANES:(c + 1) * LANES] if src.shape[1] == 1 else nat_ref[k, c, rows, :]

    slabs = []
    for c in range(n_slabs):
        o = [natural(2 * g, c) for g in range(N_DIL)]
        l = [natural(2 * g + 1, c) for g in range(N_DIL)]
        mx = functools.reduce(jnp.maximum, l)
        e = [jnp.exp(lg - mx) for lg in l]
        num = functools.reduce(jnp.add, [eg * og for eg, og in zip(e, o)])
        slabs.append((num / functools.reduce(jnp.add, e)).astype(BF16))
    attn = jnp.concatenate(slabs, axis=1)
    yield
    a = jnp.dot(attn, wa_ref[...], preferred_element_type=F32)
    yield
    s = jnp.dot(ssm_ref[rows, :], ws_ref[...], preferred_element_type=F32)
    d = a.shape[1]
    gates = _sigmoid(gate_ref[0, 0, rows, :].astype(F32))
    merged = (gates[:, :d] * a + gates[:, d:] * s).astype(BF16)
    yield
    acc = h_ref[0, rows, :] + jnp.dot(merged, wo_ref[...], preferred_element_type=F32)
    xn = _rmsnorm(acc, nw_ref[...]).astype(BF16)
    yield
    chunks = [slice(c0, c0 + ffn_chunk) for c0 in range(0, wd_ref.shape[0], ffn_chunk)]
    gate_up = None
    for c, cols in enumerate(chunks + [None]):
        prev_cols, prev_gate_up = (chunks[c - 1], gate_up) if c > 0 else (None, None)
        if cols is not None:
            gate = jnp.dot(xn, wg_ref[:, cols], preferred_element_type=F32)
            yield
            gate_up = (gate, jnp.dot(xn, wu_ref[:, cols], preferred_element_type=F32))
            yield
        if prev_gate_up is not None:
            act = (prev_gate_up[0] * _sigmoid(prev_gate_up[0]) * prev_gate_up[1]).astype(BF16)
            acc = acc + jnp.dot(act, wd_ref[prev_cols, :], preferred_element_type=F32)
            yield
    out_ref[0, rows, :] = _rmsnorm(acc, fw_ref[...]) if final_norm else acc


def _tail_kernel(*refs, ffn_chunk, mf_rows, ssd_stages_per_matmul, ssd_lead, final_norm):
    attn_refs = refs[:2 * N_DIL]
    (gate_ref, h_ref, xs_ref, bc_ref, z_ref, dt_ref, wa_ref, ws_ref, wo_ref, n2_ref, wg_ref, wu_ref, wd_ref, fw_ref,
     cw_ref, cb_ref, dtb_ref, alog_ref, dskip_ref, nw_ref, out_ref,
     state_ref, slab_ref, ssm_ref, new_ref, nat_ref) = refs[2 * N_DIL:]

    @pl.when(pl.program_id(1) == 0)
    def _():
        state_ref[...] = jnp.zeros_like(state_ref)
        slab_ref[:, 0:SUBLANES, :] = jnp.zeros((slab_ref.shape[0], SUBLANES, LANES), F32)
        ssm_ref[...] = jnp.zeros_like(ssm_ref)

    def ssd_stages():
        for r0 in range(0, xs_ref.shape[2], SSD_CHUNK):
            decays = _ssd_decays(r0, dt_ref, dtb_ref, alog_ref)
            yield
            yield from _ssd_chunk_stages(r0, decays, xs_ref, bc_ref, z_ref, cw_ref, cb_ref, dskip_ref, nw_ref,
                                         new_ref, state_ref, slab_ref)

    tb = h_ref.shape[1]
    matmuls = [_merge_ffn_stages(r0, mf_rows, attn_refs, ssm_ref, gate_ref, h_ref, wa_ref, ws_ref, wo_ref,
                                 nat_ref, n2_ref, wg_ref, wu_ref, wd_ref, fw_ref, out_ref,
                                 ffn_chunk=ffn_chunk, final_norm=final_norm) for r0 in range(0, tb, mf_rows)]
    scan = ssd_stages()
    for _ in range(ssd_lead):
        next(scan)
    turns = [(gen, 1) for gen in matmuls] + [(scan, ssd_stages_per_matmul)]
    live = [gen for gen, _ in turns]
    while live:
        for gen, reps in turns:
            for _ in range(reps if gen in live else 0):
                if next(gen, StopIteration) is StopIteration:
                    live.remove(gen)
                    break
    ssm_ref[...] = new_ref[...]


def _tail(attn_outs, main, dt3, h3, weights, *, col_z, col_gates, col_xs, col_bc, tb, ffn_chunk,
          mf_rows, ssd_stages_per_matmul, ssd_lead, final_norm):
    b, s, d = h3.shape
    (w_attn, w_ssm, w_out, norm2_w, w_gate, w_up, w_down, final_w, conv_w, conv_b, dt_bias, a_log, d_skip,
     norm_w) = weights
    d_inner = norm_w.shape[1]
    conv_ch = conv_w.shape[1]
    bc_w = conv_ch - d_inner
    n_blocks = s // tb

    def cur(k):
        return jnp.minimum(k, n_blocks - 1)

    def prev(k):
        return jnp.maximum(k - 1, 0)

    def residues(arr):
        dil = arr.shape[1]
        return pl.BlockSpec((1, dil, tb // dil, GROUP_WIDTH), lambda bi, k: (bi, 0, prev(k), 0))

    flat = [a for pair in attn_outs for a in pair]
    in_specs = [residues(a) for a in flat] + [
        pl.BlockSpec((1, 1, tb, 2 * d), lambda bi, k: (bi, 0, prev(k), col_gates)),
        pl.BlockSpec((1, tb, d), lambda bi, k: (bi, prev(k), 0)),
        pl.BlockSpec((1, 1, tb, d_inner), lambda bi, k: (bi, 0, cur(k), col_xs)),
        pl.BlockSpec((1, 1, tb, bc_w), lambda bi, k: (bi, 0, cur(k), col_bc)),
        pl.BlockSpec((1, 1, tb, d_inner), lambda bi, k: (bi, 0, cur(k), col_z)),
        pl.BlockSpec((1, tb, LANES), lambda bi, k: (bi, cur(k), 0)),
    ] + [_resident(w.shape) for w in weights]
    return pl.pallas_call(
        functools.partial(_tail_kernel, ffn_chunk=ffn_chunk, ssd_stages_per_matmul=ssd_stages_per_matmul,
                          ssd_lead=ssd_lead, mf_rows=mf_rows, final_norm=final_norm),
        grid=(b, n_blocks + 1),
        in_specs=in_specs,
        out_specs=pl.BlockSpec((1, tb, d), lambda bi, k: (bi, prev(k), 0)),
        out_shape=jax.ShapeDtypeStruct((b, s, d), F32),
        scratch_shapes=[pltpu.VMEM((N_SSM_GROUPS * HEADS_PER_SSM_GROUP * SSM_HEAD_DIM, D_STATE), F32),
                        pltpu.VMEM((conv_ch // LANES, SUBLANES + SSD_CHUNK, LANES), F32),
                        pltpu.VMEM((tb, d_inner), BF16),
                        pltpu.VMEM((tb, d_inner), BF16),
                        pltpu.VMEM((len(flat), GROUP_WIDTH // LANES, tb, LANES), F32)],
        compiler_params=_compiler_params(("parallel", "arbitrary")),
        name="layer_tail",
    )(*flat, main, h3, main, main, main, dt3, *weights)


def kernel(x, norm1_w, w_in, conv_w, conv_b, dt_bias, a_log, d_skip, ssm_norm_w, w_attn_branch, w_ssm_branch,
           w_out, norm2_w, w_ffn_in, w_ffn_out, rel_bias, final_norm_w):
    b, s, d = x.shape
    t = b * s
    depth = w_in.shape[0]
    d_inner = ssm_norm_w.shape[1]
    n_heads = dt_bias.shape[1]
    bc_w = 2 * N_SSM_GROUPS * D_STATE
    d_ff = w_ffn_out.shape[1]
    assert d_inner == N_SSM_GROUPS * HEADS_PER_SSM_GROUP * SSM_HEAD_DIM and n_heads <= LANES
    assert s % (DILATED_GROUPS[-1][1] * ATTN_BLOCK) == 0 and s % SSD_CHUNK == 0

    o_z = 3 * ATTN_WIDTH
    o_xbc = o_z + d_inner
    o_dt = o_xbc + d_inner + bc_w
    o_gate = o_dt + n_heads

    def qkv_cols(g):
        return [w_in[:, :, part * ATTN_WIDTH + g * GROUP_WIDTH:part * ATTN_WIDTH + (g + 1) * GROUP_WIDTH]
                for part in range(3)]

    w_main = jnp.concatenate([w_in[:, :, o_z:o_xbc], w_in[:, :, o_gate:], w_in[:, :, o_xbc:o_dt]] + qkv_cols(0),
                             axis=2).astype(BF16)
    w_qkv = [None] + [jnp.concatenate(qkv_cols(g), axis=2).astype(BF16) for g in range(1, N_DIL)]
    w_dt = jnp.pad(w_in[:, :, o_dt:o_gate], ((0, 0), (0, 0), (0, LANES - n_heads))).astype(BF16)
    col_z = 0
    col_gates = d_inner // (2 * d)
    col_xs = (d_inner + 2 * d) // d_inner
    col_bc = (2 * d_inner + 2 * d) // bc_w
    col_q = (2 * d_inner + 2 * d + bc_w) // GROUP_WIDTH
    assert d_inner % (2 * d) == 0 and (d_inner + 2 * d) % d_inner == 0 and (2 * d_inner + 2 * d) % bc_w == 0
    assert (2 * d_inner + 2 * d + bc_w) % GROUP_WIDTH == 0

    pad_h = ((0, 0), (0, LANES - n_heads))
    dt_bias_p = jnp.pad(dt_bias, pad_h)
    a_log_p = jnp.pad(a_log, pad_h)
    d_skip_x = jnp.repeat(d_skip, SSM_HEAD_DIM, axis=1)
    w_attn16, w_ssm16, w_out16 = w_attn_branch.astype(BF16), w_ssm_branch.astype(BF16), w_out.astype(BF16)
    w_gate16, w_up16 = w_ffn_in[:, :, :d_ff].astype(BF16), w_ffn_in[:, :, d_ff:].astype(BF16)
    w_down16 = w_ffn_out.astype(BF16)
    bias_tables = _attention_bias_tables(rel_bias)

    h = x
    for layer in range(depth):
        nw1 = norm1_w[layer:layer + 1]
        main, dt = _in_proj(h, nw1, w_main[layer], w_dt[layer], dilation=1, tm=512, chunk=1024)

        attn_outs = []
        for g, (_, dil) in enumerate(DILATED_GROUPS):
            if dil == 1:
                qkv, cq = main, col_q
            else:
                (qkv,), cq = _in_proj(h, nw1, w_qkv[g][layer], dilation=dil, tm=1024, chunk=GROUP_WIDTH), 0
            attn_outs.append(_attention(qkv, bias_tables[g], col_q=cq, col_k=cq + 1, col_v=cq + 2,
                                        tq=min(1024, s // dil)))

        weights = (w_attn16[layer], w_ssm16[layer], w_out16[layer], norm2_w[layer:layer + 1], w_gate16[layer],
                   w_up16[layer], w_down16[layer], final_norm_w[None, :], conv_w[layer], conv_b[layer:layer + 1],
                   dt_bias_p[layer:layer + 1], a_log_p[layer:layer + 1], d_skip_x[layer:layer + 1],
                   ssm_norm_w[layer:layer + 1])
        h = _tail(attn_outs, main, dt, h, weights, col_z=col_z, col_gates=col_gates, col_xs=col_xs,
                  col_bc=col_bc, tb=256, ffn_chunk=256, mf_rows=256, ssd_stages_per_matmul=2, ssd_lead=0,
                  final_norm=(layer == depth - 1))
    return h
```

```python
import functools
import math

import jax
import jax.numpy as jnp
from jax import lax
from jax.experimental import pallas as pl
from jax.experimental.pallas import tpu as pltpu

HEAD_DIM = 64
DILATED_GROUPS = ((128, 1), (512, 4), (2048, 16))
N_DIL = len(DILATED_GROUPS)
HEADS_PER_GROUP = 8
GROUP_WIDTH = HEADS_PER_GROUP * HEAD_DIM
ATTN_WIDTH = N_DIL * GROUP_WIDTH
ATTN_BLOCK = 128
N_REL_BUCKETS = 32
REL_MAX_DISTANCE = 2048
SSM_HEAD_DIM = 64
N_SSM_GROUPS = 4
HEADS_PER_SSM_GROUP = 8
D_STATE = 128
CONV_WIDTH = 4
SSD_CHUNK = 128
EPS = 1e-6

LANES = 128
SUBLANES = 8
VMEM_LIMIT_BYTES = 60000 * 1024

BF16 = jnp.bfloat16
F32 = jnp.float32
NT_DIMS = (((1,), (1,)), ((), ()))


def _compiler_params(semantics):
    return pltpu.CompilerParams(dimension_semantics=semantics, vmem_limit_bytes=VMEM_LIMIT_BYTES)


def _rmsnorm(x, w):
    return x * lax.rsqrt(jnp.mean(x * x, axis=-1, keepdims=True) + EPS) * w


def _sigmoid(x):
    return 1.0 / (1.0 + jnp.exp(-x))


PERM_BLOCK = 256


def _in_proj_kernel(x_ref, nw_ref, w_ref, *rest, dilation, with_dt, chunk):
    if with_dt:
        wdt_ref, out_ref, dt_ref, xn_ref = rest
    else:
        out_ref, xn_ref = rest
    tm = x_ref.shape[1]
    n = w_ref.shape[1]
    xn = _rmsnorm(x_ref[0], nw_ref[...]).astype(BF16)
    if with_dt:
        dt_ref[0] = jnp.dot(xn, wdt_ref[...], preferred_element_type=F32)
    if dilation == 1:
        xn_ref[...] = xn
    else:
        per = PERM_BLOCK // dilation
        i = lax.broadcasted_iota(jnp.int32, (PERM_BLOCK, PERM_BLOCK), 0)
        k = lax.broadcasted_iota(jnp.int32, (PERM_BLOCK, PERM_BLOCK), 1)
        src = (i & (per - 1)) * dilation + (i >> (per.bit_length() - 1))
        perm = jnp.where(k == src, 1.0, 0.0).astype(BF16)
        for u in range(tm // PERM_BLOCK):
            y = jnp.dot(perm, xn[u * PERM_BLOCK:(u + 1) * PERM_BLOCK], preferred_element_type=F32).astype(BF16)
            for r in range(dilation):
                dst = r * (tm // dilation) + u * per
                xn_ref[dst:dst + per, :] = y[r * per:(r + 1) * per]
    for c0 in range(0, n, chunk):
        c1 = min(c0 + chunk, n)
        res = jnp.dot(xn_ref[...], w_ref[:, c0:c1], preferred_element_type=F32).astype(out_ref.dtype)
        out_ref[0, :, :, c0:c1] = res.reshape(dilation, tm // dilation, c1 - c0)


def _resident(shape):
    zeros = (0,) * len(shape)
    return pl.BlockSpec(shape, lambda *_: zeros, pipeline_mode=pl.Buffered(1))


def _in_proj(h3, norm_w, w, w_dt=None, *, dilation, tm, chunk):
    b, s, d = h3.shape
    n = w.shape[1]
    with_dt = w_dt is not None
    in_specs = [pl.BlockSpec((1, tm, d), lambda bi, i: (bi, i, 0)), _resident((1, d)), _resident((d, n))]
    out_specs = [pl.BlockSpec((1, dilation, tm // dilation, n), lambda bi, i: (bi, 0, i, 0))]
    out_shape = [jax.ShapeDtypeStruct((b, dilation, s // dilation, n), BF16)]
    args = [h3, norm_w, w]
    if with_dt:
        in_specs.append(_resident((d, LANES)))
        out_specs.append(pl.BlockSpec((1, tm, LANES), lambda bi, i: (bi, i, 0)))
        out_shape.append(jax.ShapeDtypeStruct((b, s, LANES), F32))
        args.append(w_dt)
    return pl.pallas_call(
        functools.partial(_in_proj_kernel, dilation=dilation, with_dt=with_dt, chunk=chunk),
        grid=(b, s // tm),
        in_specs=in_specs,
        out_specs=out_specs,
        out_shape=out_shape,
        scratch_shapes=[pltpu.VMEM((tm, d), BF16)],
        compiler_params=_compiler_params(("parallel", "parallel")),
        name=f"in_proj_d{dilation}",
    )(*args)


def _attention_kernel(q_ref, kp_ref, kc_ref, vp_ref, vc_ref, bias_ref, o_ref, l_ref, kext_ref, vext_ref, *, tq):
    n = pl.program_id(2)
    kext_ref[0:ATTN_BLOCK] = kp_ref[0, 0]
    kext_ref[ATTN_BLOCK:] = kc_ref[0, 0]
    vext_ref[0:ATTN_BLOCK] = vp_ref[0, 0]
    vext_ref[ATTN_BLOCK:] = vc_ref[0, 0]
    low_half = lax.broadcasted_iota(jnp.int32, (1, LANES), 1) < HEAD_DIM

    for s in range(tq // ATTN_BLOCK):
        r0 = s * ATTN_BLOCK
        qs = q_ref[0, 0, pl.ds(r0, ATTN_BLOCK), :] * (HEAD_DIM ** -0.5)
        ks = kext_ref[pl.ds(r0, 2 * ATTN_BLOCK), :]
        vs = vext_ref[pl.ds(r0, 2 * ATTN_BLOCK), :]
        bidx = jnp.where(n == 0, 0, 1) if s == 0 else 1
        for p in range(GROUP_WIDTH // LANES):
            cols = slice(p * LANES, (p + 1) * LANES)
            qp, kp, vp = qs[:, cols], ks[:, cols], vs[:, cols]
            q2 = jnp.concatenate([jnp.where(low_half, qp, jnp.zeros_like(qp)),
                                  jnp.where(low_half, jnp.zeros_like(qp), qp)], axis=0)
            bias2 = jnp.concatenate([bias_ref[bidx, 2 * p], bias_ref[bidx, 2 * p + 1]], axis=0)
            sc = lax.dot_general(q2, kp, NT_DIMS, preferred_element_type=F32) + bias2
            m = jnp.max(sc, axis=1, keepdims=True)
            pe = jnp.exp(sc - m).astype(BF16)
            nd = jnp.dot(pe, jnp.concatenate([vp, jnp.ones_like(vp)], axis=1), preferred_element_type=F32)
            num, den = nd[:, :LANES], nd[:, LANES:]
            o2 = num / den
            l2 = m + jnp.log(den)
            outs = [o2[:ATTN_BLOCK], o2[ATTN_BLOCK:]]
            lses = [l2[:ATTN_BLOCK], l2[ATTN_BLOCK:]]
            o_ref[0, 0, pl.ds(r0, ATTN_BLOCK), cols] = jnp.where(low_half, outs[0], outs[1])
            l_ref[0, 0, pl.ds(r0, ATTN_BLOCK), cols] = jnp.where(low_half, lses[0], lses[1])


def _attention(qkv, bias_g, *, col_q, col_k, col_v, tq):
    b, dilation, seg, _ = qkv.shape
    blocks_per_tile = tq // ATTN_BLOCK

    def cur(col):
        return pl.BlockSpec((1, 1, tq, GROUP_WIDTH), lambda bi, r, n: (bi, r, n, col))

    def prev(col):
        return pl.BlockSpec((1, 1, ATTN_BLOCK, GROUP_WIDTH),
                            lambda bi, r, n: (bi, r, jnp.maximum(n * blocks_per_tile - 1, 0), col))

    out_spec = pl.BlockSpec((1, 1, tq, GROUP_WIDTH), lambda bi, r, n: (bi, r, n, 0))
    out_sds = jax.ShapeDtypeStruct((b, dilation, seg, GROUP_WIDTH), F32)
    return pl.pallas_call(
        functools.partial(_attention_kernel, tq=tq),
        grid=(b, dilation, seg // tq),
        in_specs=[cur(col_q), prev(col_k), cur(col_k), prev(col_v), cur(col_v),
                  pl.BlockSpec(bias_g.shape, lambda bi, r, n: (0, 0, 0, 0))],
        out_specs=[out_spec, out_spec],
        out_shape=[out_sds, out_sds],
        scratch_shapes=[pltpu.VMEM((tq + ATTN_BLOCK, GROUP_WIDTH), BF16),
                        pltpu.VMEM((tq + ATTN_BLOCK, GROUP_WIDTH), BF16)],
        compiler_params=_compiler_params(("parallel", "parallel", "arbitrary")),
        name=f"attention_d{dilation}",
    )(qkv, qkv, qkv, qkv, qkv, bias_g)


def _t5_causal_bucket(dist):
    max_exact = N_REL_BUCKETS // 2
    d_f = jnp.maximum(dist, 1).astype(F32)
    large = max_exact + (jnp.log(d_f / max_exact) / math.log(REL_MAX_DISTANCE / max_exact)
                         * (N_REL_BUCKETS - max_exact)).astype(jnp.int32)
    large = jnp.minimum(large, N_REL_BUCKETS - 1)
    return jnp.where(dist < max_exact, dist, large)


def _attention_bias_tables(rel_bias):
    q, q2 = ATTN_BLOCK, 2 * ATTN_BLOCK
    steps = jnp.arange(q2) - (q - 1)
    in_prev = (jnp.arange(q2) < q)[None, None, :]
    tables = []
    for g, (window, dil) in enumerate(DILATED_GROUPS):
        n_steps = window // dil
        assert n_steps <= q
        rel_g = rel_bias[:, g * HEADS_PER_GROUP:(g + 1) * HEADS_PER_GROUP].astype(F32)
        vals = rel_g[_t5_causal_bucket(jnp.clip(steps, 0, n_steps) * dil)]
        vec = jnp.where(((steps >= 0) & (steps <= n_steps))[:, None], vals, -jnp.inf).T
        skew = jnp.tile(vec, (1, q + 1))[:, :q * (q2 + 1)].reshape(HEADS_PER_GROUP, q, q2 + 1)[:, :, :q2]
        rest = skew[:, :, ::-1]
        first = jnp.where(in_prev, -jnp.inf, rest)
        tables.append(jnp.stack([first, rest]))
    return jnp.stack(tables)


LOG2E = math.log2(math.e)


def _conv_silu(slab_ref, c, w_ref, b_ref):
    q = slab_ref.shape[1] - SUBLANES
    cols = slice(c * LANES, (c + 1) * LANES)
    acc = b_ref[:, cols] + slab_ref[c, pl.ds(SUBLANES, q), :] * w_ref[CONV_WIDTH - 1:CONV_WIDTH, cols]
    for s in range(1, CONV_WIDTH):
        acc = acc + slab_ref[c, pl.ds(SUBLANES - s, q), :] * w_ref[CONV_WIDTH - 1 - s:CONV_WIDTH - s, cols]
    return acc * _sigmoid(acc)


def _ssd_decays(r0, dt_ref, dtb_ref, alog_ref):
    q = SSD_CHUNK
    x_dt = dt_ref[0, r0:r0 + q, :] + dtb_ref[...]
    dt = jnp.maximum(x_dt, 0.0) + jnp.log(1.0 + jnp.exp(-jnp.abs(x_dt)))
    d_a = dt * (-jnp.exp(alog_ref[...]))
    ri = lax.broadcasted_iota(jnp.int32, (q, q), 0)
    ci = lax.broadcasted_iota(jnp.int32, (q, q), 1)
    causal_t = ri <= ci
    tril = jnp.where(ri >= ci, 1.0, 0.0).astype(BF16)
    d1 = d_a.astype(BF16)
    r1 = d_a - d1.astype(F32)
    d2 = r1.astype(BF16)
    d3 = (r1 - d2.astype(F32)).astype(BF16)
    la = (jnp.dot(tril, d1, preferred_element_type=F32) + jnp.dot(tril, d2, preferred_element_type=F32)
          + jnp.dot(tril, d3, preferred_element_type=F32))
    la = la * LOG2E
    la_t = la.T
    dt_t = dt.T
    last = jnp.broadcast_to(la_t[:, q - 1:q], (q, q))
    w_t = jnp.exp2(last - la_t) * dt_t
    state_decay = jnp.exp2(last)
    return causal_t, la, la_t, dt_t, w_t, state_decay


def _ssd_chunk_stages(r0, decays, xs_ref, bc_ref, z_ref, cw_ref, cb_ref, dskip_ref, nw_ref,
                      ssm_ref, state_ref, slab_ref):
    causal_t, la, la_t, dt_t, w_t, state_decay = decays
    q = SSD_CHUNK
    rq = slice(r0, r0 + q)
    gw = HEADS_PER_SSM_GROUP * SSM_HEAD_DIM
    xs_slabs = xs_ref.shape[3] // LANES
    slabs_per_group = gw // LANES

    for c in range(xs_slabs):
        slab_ref[c, SUBLANES:, :] = xs_ref[0, 0, rq, c * LANES:(c + 1) * LANES].astype(F32)
    for c in range(bc_ref.shape[3] // LANES):
        slab_ref[xs_slabs + c, SUBLANES:, :] = bc_ref[0, 0, rq, c * LANES:(c + 1) * LANES].astype(F32)

    for g in range(N_SSM_GROUPS):
        bm = _conv_silu(slab_ref, xs_slabs + g, cw_ref, cb_ref)
        cm = _conv_silu(slab_ref, xs_slabs + N_SSM_GROUPS + g, cw_ref, cb_ref)
        bm16 = bm.astype(BF16)
        cb_t = lax.dot_general(bm16, cm.astype(BF16), NT_DIMS, preferred_element_type=F32)
        cm_t = cm.T
        xs_g = jnp.concatenate([_conv_silu(slab_ref, g * slabs_per_group + c, cw_ref, cb_ref)
                                for c in range(slabs_per_group)], axis=1)
        xs_t = xs_g.T
        yield
        y_t = []
        for h in range(HEADS_PER_SSM_GROUP):
            hh = g * HEADS_PER_SSM_GROUP + h
            rows = slice(hh * SSM_HEAD_DIM, (hh + 1) * SSM_HEAD_DIM)
            la_i = la_t[hh:hh + 1, :]
            la_j = jnp.broadcast_to(la[:, hh:hh + 1], (q, q))
            decay_t = jnp.exp2(jnp.where(causal_t, la_i - la_j, -jnp.inf))
            rhs = jnp.concatenate([(cb_t * decay_t).astype(BF16), (cm_t * jnp.exp2(la_i)).astype(BF16)], axis=0)
            x_h = xs_t[h * SSM_HEAD_DIM:(h + 1) * SSM_HEAD_DIM]
            state = state_ref[rows, :]
            lhs = jnp.concatenate([(x_h * dt_t[hh:hh + 1, :]).astype(BF16), state.astype(BF16)], axis=1)
            y_t.append(jnp.dot(lhs, rhs, preferred_element_type=F32))
            yield
        heads = range(g * HEADS_PER_SSM_GROUP, (g + 1) * HEADS_PER_SSM_GROUP)
        grows = slice(heads[0] * SSM_HEAD_DIM, (heads[-1] + 1) * SSM_HEAD_DIM)
        per_head = lambda t: jnp.concatenate(
            [jnp.broadcast_to(t[hh:hh + 1, :], (SSM_HEAD_DIM, q)) for hh in heads], axis=0)
        upd = jnp.dot((xs_t * per_head(w_t)).astype(BF16), bm16, preferred_element_type=F32)
        state_ref[grows, :] = state_ref[grows, :] * per_head(state_decay) + upd
        yield
        y = jnp.concatenate(y_t, axis=0).T
        cols = slice(g * gw, (g + 1) * gw)
        y = y + xs_g * dskip_ref[:, cols]
        z = z_ref[0, 0, rq, cols].astype(F32)
        yg = y * (z * _sigmoid(z))
        yg = yg * lax.rsqrt(jnp.mean(yg * yg, axis=-1, keepdims=True) + EPS)
        ssm_ref[rq, cols] = (yg * nw_ref[:, cols]).astype(ssm_ref.dtype)
        yield

    slab_ref[:, 0:SUBLANES, :] = slab_ref[:, q:q + SUBLANES, :]


def _merge_ffn_stages(r0, tm, attn_refs, ssm_ref, gate_ref, h_ref, wa_ref, ws_ref, wo_ref, nat_ref,
                      nw_ref, wg_ref, wu_ref, wd_ref, fw_ref, out_ref, *, ffn_chunk, final_norm):
    rows = slice(r0, r0 + tm)
    n_slabs = GROUP_WIDTH // LANES
    for k, src in enumerate(attn_refs):
        dil = src.shape[1]
        for r in range(dil if dil > 1 else 0):
            for c in range(n_slabs):
                nat_ref[k, c, pl.ds(r0 + r, tm // dil, stride=dil), :] = (
                    src[0, r, r0 // dil:(r0 + tm) // dil, c * LANES:(c + 1) * LANES])

    def natural(k, c):
        src = attn_refs[k]
        return src[0, 0, rows, c * LANES:(c + 1) * LANES] if src.shape[1] == 1 else nat_ref[k, c, rows, :]

    slabs = []
    for c in range(n_slabs):
        o = [natural(2 * g, c) for g in range(N_DIL)]
        l = [natural(2 * g + 1, c) for g in range(N_DIL)]
        mx = functools.reduce(jnp.maximum, l)
        e = [jnp.exp(lg - mx) for lg in l]
        num = functools.reduce(jnp.add, [eg * og for eg, og in zip(e, o)])
        slabs.append((num / functools.reduce(jnp.add, e)).astype(BF16))
    attn = jnp.concatenate(slabs, axis=1)
    yield
    a = jnp.dot(attn, wa_ref[...], preferred_element_type=F32)
    yield
    s = jnp.dot(ssm_ref[rows, :], ws_ref[...], preferred_element_type=F32)
    d = a.shape[1]
    gates = _sigmoid(gate_ref[0, 0, rows, :].astype(F32))
    merged = (gates[:, :d] * a + gates[:, d:] * s).astype(BF16)
    yield
    acc = h_ref[0, rows, :] + jnp.dot(merged, wo_ref[...], preferred_element_type=F32)
    xn = _rmsnorm(acc, nw_ref[...]).astype(BF16)
    yield
    chunks = [slice(c0, c0 + ffn_chunk) for c0 in range(0, wd_ref.shape[0], ffn_chunk)]
    gate_up = None
    for c, cols in enumerate(chunks + [None]):
        prev_cols, prev_gate_up = (chunks[c - 1], gate_up) if c > 0 else (None, None)
        if cols is not None:
            gate = jnp.dot(xn, wg_ref[:, cols], preferred_element_type=F32)
            yield
            gate_up = (gate, jnp.dot(xn, wu_ref[:, cols], preferred_element_type=F32))
            yield
        if prev_gate_up is not None:
            act = (prev_gate_up[0] * _sigmoid(prev_gate_up[0]) * prev_gate_up[1]).astype(BF16)
            acc = acc + jnp.dot(act, wd_ref[prev_cols, :], preferred_element_type=F32)
            yield
    out_ref[0, rows, :] = _rmsnorm(acc, fw_ref[...]) if final_norm else acc


def _tail_kernel(*refs, ffn_chunk, mf_rows, ssd_stages_per_matmul, ssd_lead, final_norm):
    attn_refs = refs[:2 * N_DIL]
    (gate_ref, h_ref, xs_ref, bc_ref, z_ref, dt_ref, wa_ref, ws_ref, wo_ref, n2_ref, wg_ref, wu_ref, wd_ref, fw_ref,
     cw_ref, cb_ref, dtb_ref, alog_ref, dskip_ref, nw_ref, out_ref,
     state_ref, slab_ref, ssm_ref, new_ref, nat_ref) = refs[2 * N_DIL:]

    @pl.when(pl.program_id(1) == 0)
    def _():
        state_ref[...] = jnp.zeros_like(state_ref)
        slab_ref[:, 0:SUBLANES, :] = jnp.zeros((slab_ref.shape[0], SUBLANES, LANES), F32)
        ssm_ref[...] = jnp.zeros_like(ssm_ref)

    def ssd_stages():
        for r0 in range(0, xs_ref.shape[2], SSD_CHUNK):
            decays = _ssd_decays(r0, dt_ref, dtb_ref, alog_ref)
            yield
            yield from _ssd_chunk_stages(r0, decays, xs_ref, bc_ref, z_ref, cw_ref, cb_ref, dskip_ref, nw_ref,
                                         new_ref, state_ref, slab_ref)

    tb = h_ref.shape[1]
    matmuls = [_merge_ffn_stages(r0, mf_rows, attn_refs, ssm_ref, gate_ref, h_ref, wa_ref, ws_ref, wo_ref,
                                 nat_ref, n2_ref, wg_ref, wu_ref, wd_ref, fw_ref, out_ref,
                                 ffn_chunk=ffn_chunk, final_norm=final_norm) for r0 in range(0, tb, mf_rows)]
    scan = ssd_stages()
    for _ in range(ssd_lead):
        next(scan)
    turns = [(gen, 1) for gen in matmuls] + [(scan, ssd_stages_per_matmul)]
    live = [gen for gen, _ in turns]
    while live:
        for gen, reps in turns:
            for _ in range(reps if gen in live else 0):
                if next(gen, StopIteration) is StopIteration:
                    live.remove(gen)
                    break
    ssm_ref[...] = new_ref[...]


def _tail(attn_outs, main, dt3, h3, weights, *, col_z, col_gates, col_xs, col_bc, tb, ffn_chunk,
          mf_rows, ssd_stages_per_matmul, ssd_lead, final_norm):
    b, s, d = h3.shape
    (w_attn, w_ssm, w_out, norm2_w, w_gate, w_up, w_down, final_w, conv_w, conv_b, dt_bias, a_log, d_skip,
     norm_w) = weights
    d_inner = norm_w.shape[1]
    conv_ch = conv_w.shape[1]
    bc_w = conv_ch - d_inner
    n_blocks = s // tb

    def cur(k):
        return jnp.minimum(k, n_blocks - 1)

    def prev(k):
        return jnp.maximum(k - 1, 0)

    def residues(arr):
        dil = arr.shape[1]
        return pl.BlockSpec((1, dil, tb // dil, GROUP_WIDTH), lambda bi, k: (bi, 0, prev(k), 0))

    flat = [a for pair in attn_outs for a in pair]
    in_specs = [residues(a) for a in flat] + [
        pl.BlockSpec((1, 1, tb, 2 * d), lambda bi, k: (bi, 0, prev(k), col_gates)),
        pl.BlockSpec((1, tb, d), lambda bi, k: (bi, prev(k), 0)),
        pl.BlockSpec((1, 1, tb, d_inner), lambda bi, k: (bi, 0, cur(k), col_xs)),
        pl.BlockSpec((1, 1, tb, bc_w), lambda bi, k: (bi, 0, cur(k), col_bc)),
        pl.BlockSpec((1, 1, tb, d_inner), lambda bi, k: (bi, 0, cur(k), col_z)),
        pl.BlockSpec((1, tb, LANES), lambda bi, k: (bi, cur(k), 0)),
    ] + [_resident(w.shape) for w in weights]
    return pl.pallas_call(
        functools.partial(_tail_kernel, ffn_chunk=ffn_chunk, ssd_stages_per_matmul=ssd_stages_per_matmul,
                          ssd_lead=ssd_lead, mf_rows=mf_rows, final_norm=final_norm),
        grid=(b, n_blocks + 1),
        in_specs=in_specs,
        out_specs=pl.BlockSpec((1, tb, d), lambda bi, k: (bi, prev(k), 0)),
        out_shape=jax.ShapeDtypeStruct((b, s, d), F32),
        scratch_shapes=[pltpu.VMEM((N_SSM_GROUPS * HEADS_PER_SSM_GROUP * SSM_HEAD_DIM, D_STATE), F32),
                        pltpu.VMEM((conv_ch // LANES, SUBLANES + SSD_CHUNK, LANES), F32),
                        pltpu.VMEM((tb, d_inner), BF16),
                        pltpu.VMEM((tb, d_inner), BF16),
                        pltpu.VMEM((len(flat), GROUP_WIDTH // LANES, tb, LANES), F32)],
        compiler_params=_compiler_params(("parallel", "arbitrary")),
        name="layer_tail",
    )(*flat, main, h3, main, main, main, dt3, *weights)


def kernel(x, norm1_w, w_in, conv_w, conv_b, dt_bias, a_log, d_skip, ssm_norm_w, w_attn_branch, w_ssm_branch,
           w_out, norm2_w, w_ffn_in, w_ffn_out, rel_bias, final_norm_w):
    b, s, d = x.shape
    t = b * s
    depth = w_in.shape[0]
    d_inner = ssm_norm_w.shape[1]
    n_heads = dt_bias.shape[1]
    bc_w = 2 * N_SSM_GROUPS * D_STATE
    d_ff = w_ffn_out.shape[1]
    assert d_inner == N_SSM_GROUPS * HEADS_PER_SSM_GROUP * SSM_HEAD_DIM and n_heads <= LANES
    assert s % (DILATED_GROUPS[-1][1] * ATTN_BLOCK) == 0 and s % SSD_CHUNK == 0

    o_z = 3 * ATTN_WIDTH
    o_xbc = o_z + d_inner
    o_dt = o_xbc + d_inner + bc_w
    o_gate = o_dt + n_heads

    def qkv_cols(g):
        return [w_in[:, :, part * ATTN_WIDTH + g * GROUP_WIDTH:part * ATTN_WIDTH + (g + 1) * GROUP_WIDTH]
                for part in range(3)]

    w_main = jnp.concatenate([w_in[:, :, o_z:o_xbc], w_in[:, :, o_gate:], w_in[:, :, o_xbc:o_dt]] + qkv_cols(0),
                             axis=2).astype(BF16)
    w_qkv = [None] + [jnp.concatenate(qkv_cols(g), axis=2).astype(BF16) for g in range(1, N_DIL)]
    w_dt = jnp.pad(w_in[:, :, o_dt:o_gate], ((0, 0), (0, 0), (0, LANES - n_heads))).astype(BF16)
    col_z = 0
    col_gates = d_inner // (2 * d)
    col_xs = (d_inner + 2 * d) // d_inner
    col_bc = (2 * d_inner + 2 * d) // bc_w
    col_q = (2 * d_inner + 2 * d + bc_w) // GROUP_WIDTH
    assert d_inner % (2 * d) == 0 and (d_inner + 2 * d) % d_inner == 0 and (2 * d_inner + 2 * d) % bc_w == 0
    assert (2 * d_inner + 2 * d + bc_w) % GROUP_WIDTH == 0

    pad_h = ((0, 0), (0, LANES - n_heads))
    dt_bias_p = jnp.pad(dt_bias, pad_h)
    a_log_p = jnp.pad(a_log, pad_h)
    d_skip_x = jnp.repeat(d_skip, SSM_HEAD_DIM, axis=1)
    w_attn16, w_ssm16, w_out16 = w_attn_branch.astype(BF16), w_ssm_branch.astype(BF16), w_out.astype(BF16)
    w_gate16, w_up16 = w_ffn_in[:, :, :d_ff].astype(BF16), w_ffn_in[:, :, d_ff:].astype(BF16)
    w_down16 = w_ffn_out.astype(BF16)
    bias_tables = _attention_bias_tables(rel_bias)

    h = x
    for layer in range(depth):
        nw1 = norm1_w[layer:layer + 1]
        main, dt = _in_proj(h, nw1, w_main[layer], w_dt[layer], dilation=1, tm=512, chunk=1024)

        attn_outs = []
        for g, (_, dil) in enumerate(DILATED_GROUPS):
            if dil == 1:
                qkv, cq = main, col_q
            else:
                (qkv,), cq = _in_proj(h, nw1, w_qkv[g][layer], dilation=dil, tm=1024, chunk=GROUP_WIDTH), 0
            attn_outs.append(_attention(qkv, bias_tables[g], col_q=cq, col_k=cq + 1, col_v=cq + 2,
                                        tq=min(1024, s // dil)))

        weights = (w_attn16[layer], w_ssm16[layer], w_out16[layer], norm2_w[layer:layer + 1], w_gate16[layer],
                   w_up16[layer], w_down16[layer], final_norm_w[None, :], conv_w[layer], conv_b[layer:layer + 1],
                   dt_bias_p[layer:layer + 1], a_log_p[layer:layer + 1], d_skip_x[layer:layer + 1],
                   ssm_norm_w[layer:layer + 1])
        h = _tail(attn_outs, main, dt, h, weights, col_z=col_z, col_gates=col_gates, col_xs=col_xs,
                  col_bc=col_bc, tb=256, ffn_chunk=256, mf_rows=256, ssd_stages_per_matmul=2, ssd_lead=0,
                  final_norm=(layer == depth - 1))
    return h
```

```python
import functools
import math

import jax
import jax.numpy as jnp
from jax import lax
from jax.experimental import pallas as pl
from jax.experimental.pallas import tpu as pltpu

HEAD_DIM = 64
DILATED_GROUPS = ((128, 1), (512, 4), (2048, 16))
N_DIL = len(DILATED_GROUPS)
HEADS_PER_GROUP = 8
GROUP_WIDTH = HEADS_PER_GROUP * HEAD_DIM
ATTN_WIDTH = N_DIL * GROUP_WIDTH
ATTN_BLOCK = 128
N_REL_BUCKETS = 32
REL_MAX_DISTANCE = 2048
SSM_HEAD_DIM = 64
N_SSM_GROUPS = 4
HEADS_PER_SSM_GROUP = 8
D_STATE = 128
CONV_WIDTH = 4
SSD_CHUNK = 128
EPS = 1e-6

LANES = 128
SUBLANES = 8
VMEM_LIMIT_BYTES = 60000 * 1024

BF16 = jnp.bfloat16
F32 = jnp.float32
NT_DIMS = (((1,), (1,)), ((), ()))


def _compiler_params(semantics):
    return pltpu.CompilerParams(dimension_semantics=semantics, vmem_limit_bytes=VMEM_LIMIT_BYTES)


def _rmsnorm(x, w):
    return x * lax.rsqrt(jnp.mean(x * x, axis=-1, keepdims=True) + EPS) * w


def _sigmoid(x):
    return 1.0 / (1.0 + jnp.exp(-x))


PERM_BLOCK = 256


def _in_proj_kernel(x_ref, nw_ref, w_ref, *rest, dilation, with_dt, chunk):
    if with_dt:
        wdt_ref, out_ref, dt_ref, xn_ref = rest
    else:
        out_ref, xn_ref = rest
    tm = x_ref.shape[1]
    n = w_ref.shape[1]
    xn = _rmsnorm(x_ref[0], nw_ref[...]).astype(BF16)
    if with_dt:
        dt_ref[0] = jnp.dot(xn, wdt_ref[...], preferred_element_type=F32)
    if dilation == 1:
        xn_ref[...] = xn
    else:
        per = PERM_BLOCK // dilation
        i = lax.broadcasted_iota(jnp.int32, (PERM_BLOCK, PERM_BLOCK), 0)
        k = lax.broadcasted_iota(jnp.int32, (PERM_BLOCK, PERM_BLOCK), 1)
        src = (i & (per - 1)) * dilation + (i >> (per.bit_length() - 1))
        perm = jnp.where(k == src, 1.0, 0.0).astype(BF16)
        for u in range(tm // PERM_BLOCK):
            y = jnp.dot(perm, xn[u * PERM_BLOCK:(u + 1) * PERM_BLOCK], preferred_element_type=F32).astype(BF16)
            for r in range(dilation):
                dst = r * (tm // dilation) + u * per
                xn_ref[dst:dst + per, :] = y[r * per:(r + 1) * per]
    for c0 in range(0, n, chunk):
        c1 = min(c0 + chunk, n)
        res = jnp.dot(xn_ref[...], w_ref[:, c0:c1], preferred_element_type=F32).astype(out_ref.dtype)
        out_ref[0, :, :, c0:c1] = res.reshape(dilation, tm // dilation, c1 - c0)


def _resident(shape):
    zeros = (0,) * len(shape)
    return pl.BlockSpec(shape, lambda *_: zeros, pipeline_mode=pl.Buffered(1))


def _in_proj(h3, norm_w, w, w_dt=None, *, dilation, tm, chunk):
    b, s, d = h3.shape
    n = w.shape[1]
    with_dt = w_dt is not None
    in_specs = [pl.BlockSpec((1, tm, d), lambda bi, i: (bi, i, 0)), _resident((1, d)), _resident((d, n))]
    out_specs = [pl.BlockSpec((1, dilation, tm // dilation, n), lambda bi, i: (bi, 0, i, 0))]
    out_shape = [jax.ShapeDtypeStruct((b, dilation, s // dilation, n), BF16)]
    args = [h3, norm_w, w]
    if with_dt:
        in_specs.append(_resident((d, LANES)))
        out_specs.append(pl.BlockSpec((1, tm, LANES), lambda bi, i: (bi, i, 0)))
        out_shape.append(jax.ShapeDtypeStruct((b, s, LANES), F32))
        args.append(w_dt)
    return pl.pallas_call(
        functools.partial(_in_proj_kernel, dilation=dilation, with_dt=with_dt, chunk=chunk),
        grid=(b, s // tm),
        in_specs=in_specs,
        out_specs=out_specs,
        out_shape=out_shape,
        scratch_shapes=[pltpu.VMEM((tm, d), BF16)],
        compiler_params=_compiler_params(("parallel", "parallel")),
        name=f"in_proj_d{dilation}",
    )(*args)


def _attention_kernel(q_ref, kp_ref, kc_ref, vp_ref, vc_ref, bias_ref, o_ref, l_ref, kext_ref, vext_ref, *, tq):
    n = pl.program_id(2)
    kext_ref[0:ATTN_BLOCK] = kp_ref[0, 0]
    kext_ref[ATTN_BLOCK:] = kc_ref[0, 0]
    vext_ref[0:ATTN_BLOCK] = vp_ref[0, 0]
    vext_ref[ATTN_BLOCK:] = vc_ref[0, 0]
    low_half = lax.broadcasted_iota(jnp.int32, (1, LANES), 1) < HEAD_DIM

    for s in range(tq // ATTN_BLOCK):
        r0 = s * ATTN_BLOCK
        qs = q_ref[0, 0, pl.ds(r0, ATTN_BLOCK), :] * (HEAD_DIM ** -0.5)
        ks = kext_ref[pl.ds(r0, 2 * ATTN_BLOCK), :]
        vs = vext_ref[pl.ds(r0, 2 * ATTN_BLOCK), :]
        bidx = jnp.where(n == 0, 0, 1) if s == 0 else 1
        for p in range(GROUP_WIDTH // LANES):
            cols = slice(p * LANES, (p + 1) * LANES)
            qp, kp, vp = qs[:, cols], ks[:, cols], vs[:, cols]
            q2 = jnp.concatenate([jnp.where(low_half, qp, jnp.zeros_like(qp)),
                                  jnp.where(low_half, jnp.zeros_like(qp), qp)], axis=0)
            bias2 = jnp.concatenate([bias_ref[bidx, 2 * p], bias_ref[bidx, 2 * p + 1]], axis=0)
            sc = lax.dot_general(q2, kp, NT_DIMS, preferred_element_type=F32) + bias2
            m = jnp.max(sc, axis=1, keepdims=True)
            pe = jnp.exp(sc - m).astype(BF16)
            nd = jnp.dot(pe, jnp.concatenate([vp, jnp.ones_like(vp)], axis=1), preferred_element_type=F32)
            num, den = nd[:, :LANES], nd[:, LANES:]
            o2 = num / den
            l2 = m + jnp.log(den)
            outs = [o2[:ATTN_BLOCK], o2[ATTN_BLOCK:]]
            lses = [l2[:ATTN_BLOCK], l2[ATTN_BLOCK:]]
            o_ref[0, 0, pl.ds(r0, ATTN_BLOCK), cols] = jnp.where(low_half, outs[0], outs[1])
            l_ref[0, 0, pl.ds(r0, ATTN_BLOCK), cols] = jnp.where(low_half, lses[0], lses[1])


def _attention(qkv, bias_g, *, col_q, col_k, col_v, tq):
    b, dilation, seg, _ = qkv.shape
    blocks_per_tile = tq // ATTN_BLOCK

    def cur(col):
        return pl.BlockSpec((1, 1, tq, GROUP_WIDTH), lambda bi, r, n: (bi, r, n, col))

    def prev(col):
        return pl.BlockSpec((1, 1, ATTN_BLOCK, GROUP_WIDTH),
                            lambda bi, r, n: (bi, r, jnp.maximum(n * blocks_per_tile - 1, 0), col))

    out_spec = pl.BlockSpec((1, 1, tq, GROUP_WIDTH), lambda bi, r, n: (bi, r, n, 0))
    out_sds = jax.ShapeDtypeStruct((b, dilation, seg, GROUP_WIDTH), F32)
    return pl.pallas_call(
        functools.partial(_attention_kernel, tq=tq),
        grid=(b, dilation, seg // tq),
        in_specs=[cur(col_q), prev(col_k), cur(col_k), prev(col_v), cur(col_v),
                  pl.BlockSpec(bias_g.shape, lambda bi, r, n: (0, 0, 0, 0))],
        out_specs=[out_spec, out_spec],
        out_shape=[out_sds, out_sds],
        scratch_shapes=[pltpu.VMEM((tq + ATTN_BLOCK, GROUP_WIDTH), BF16),
                        pltpu.VMEM((tq + ATTN_BLOCK, GROUP_WIDTH), BF16)],
        compiler_params=_compiler_params(("parallel", "parallel", "arbitrary")),
        name=f"attention_d{dilation}",
    )(qkv, qkv, qkv, qkv, qkv, bias_g)


def _t5_causal_bucket(dist):
    max_exact = N_REL_BUCKETS // 2
    d_f = jnp.maximum(dist, 1).astype(F32)
    large = max_exact + (jnp.log(d_f / max_exact) / math.log(REL_MAX_DISTANCE / max_exact)
                         * (N_REL_BUCKETS - max_exact)).astype(jnp.int32)
    large = jnp.minimum(large, N_REL_BUCKETS - 1)
    return jnp.where(dist < max_exact, dist, large)


def _attention_bias_tables(rel_bias):
    q, q2 = ATTN_BLOCK, 2 * ATTN_BLOCK
    steps = jnp.arange(q2) - (q - 1)
    in_prev = (jnp.arange(q2) < q)[None, None, :]
    tables = []
    for g, (window, dil) in enumerate(DILATED_GROUPS):
        n_steps = window // dil
        assert n_steps <= q
        rel_g = rel_bias[:, g * HEADS_PER_GROUP:(g + 1) * HEADS_PER_GROUP].astype(F32)
        vals = rel_g[_t5_causal_bucket(jnp.clip(steps, 0, n_steps) * dil)]
        vec = jnp.where(((steps >= 0) & (steps <= n_steps))[:, None], vals, -jnp.inf).T
        skew = jnp.tile(vec, (1, q + 1))[:, :q * (q2 + 1)].reshape(HEADS_PER_GROUP, q, q2 + 1)[:, :, :q2]
        rest = skew[:, :, ::-1]
        first = jnp.where(in_prev, -jnp.inf, rest)
        tables.append(jnp.stack([first, rest]))
    return jnp.stack(tables)


LOG2E = math.log2(math.e)


def _conv_silu(slab_ref, c, w_ref, b_ref):
    q = slab_ref.shape[1] - SUBLANES
    cols = slice(c * LANES, (c + 1) * LANES)
    acc = b_ref[:, cols] + slab_ref[c, pl.ds(SUBLANES, q), :] * w_ref[CONV_WIDTH - 1:CONV_WIDTH, cols]
    for s in range(1, CONV_WIDTH):
        acc = acc + slab_ref[c, pl.ds(SUBLANES - s, q), :] * w_ref[CONV_WIDTH - 1 - s:CONV_WIDTH - s, cols]
    return acc * _sigmoid(acc)


def _ssd_decays(r0, dt_ref, dtb_ref, alog_ref):
    q = SSD_CHUNK
    x_dt = dt_ref[0, r0:r0 + q, :] + dtb_ref[...]
    dt = jnp.maximum(x_dt, 0.0) + jnp.log(1.0 + jnp.exp(-jnp.abs(x_dt)))
    d_a = dt * (-jnp.exp(alog_ref[...]))
    ri = lax.broadcasted_iota(jnp.int32, (q, q), 0)
    ci = lax.broadcasted_iota(jnp.int32, (q, q), 1)
    causal_t = ri <= ci
    tril = jnp.where(ri >= ci, 1.0, 0.0).astype(BF16)
    d1 = d_a.astype(BF16)
    r1 = d_a - d1.astype(F32)
    d2 = r1.astype(BF16)
    d3 = (r1 - d2.astype(F32)).astype(BF16)
    la = (jnp.dot(tril, d1, preferred_element_type=F32) + jnp.dot(tril, d2, preferred_element_type=F32)
          + jnp.dot(tril, d3, preferred_element_type=F32))
    la = la * LOG2E
    la_t = la.T
    dt_t = dt.T
    last = jnp.broadcast_to(la_t[:, q - 1:q], (q, q))
    w_t = jnp.exp2(last - la_t) * dt_t
    state_decay = jnp.exp2(last)
    return causal_t, la, la_t, dt_t, w_t, state_decay


def _ssd_chunk_stages(r0, decays, xs_ref, bc_ref, z_ref, cw_ref, cb_ref, dskip_ref, nw_ref,
                      ssm_ref, state_ref, slab_ref):
    causal_t, la, la_t, dt_t, w_t, state_decay = decays
    q = SSD_CHUNK
    rq = slice(r0, r0 + q)
    gw = HEADS_PER_SSM_GROUP * SSM_HEAD_DIM
    xs_slabs = xs_ref.shape[3] // LANES
    slabs_per_group = gw // LANES

    for c in range(xs_slabs):
        slab_ref[c, SUBLANES:, :] = xs_ref[0, 0, rq, c * LANES:(c + 1) * LANES].astype(F32)
    for c in range(bc_ref.shape[3] // LANES):
        slab_ref[xs_slabs + c, SUBLANES:, :] = bc_ref[0, 0, rq, c * LANES:(c + 1) * LANES].astype(F32)

    for g in range(N_SSM_GROUPS):
        bm = _conv_silu(slab_ref, xs_slabs + g, cw_ref, cb_ref)
        cm = _conv_silu(slab_ref, xs_slabs + N_SSM_GROUPS + g, cw_ref, cb_ref)
        bm16 = bm.astype(BF16)
        cb_t = lax.dot_general(bm16, cm.astype(BF16), NT_DIMS, preferred_element_type=F32)
        cm_t = cm.T
        xs_g = jnp.concatenate([_conv_silu(slab_ref, g * slabs_per_group + c, cw_ref, cb_ref)
                                for c in range(slabs_per_group)], axis=1)
        xs_t = xs_g.T
        yield
        y_t = []
        for h in range(HEADS_PER_SSM_GROUP):
            hh = g * HEADS_PER_SSM_GROUP + h
            rows = slice(hh * SSM_HEAD_DIM, (hh + 1) * SSM_HEAD_DIM)
            la_i = la_t[hh:hh + 1, :]
            la_j = jnp.broadcast_to(la[:, hh:hh + 1], (q, q))
            decay_t = jnp.exp2(jnp.where(causal_t, la_i - la_j, -jnp.inf))
            rhs = jnp.concatenate([(cb_t * decay_t).astype(BF16), (cm_t * jnp.exp2(la_i)).astype(BF16)], axis=0)
            x_h = xs_t[h * SSM_HEAD_DIM:(h + 1) * SSM_HEAD_DIM]
            state = state_ref[rows, :]
            lhs = jnp.concatenate([(x_h * dt_t[hh:hh + 1, :]).astype(BF16), state.astype(BF16)], axis=1)
            y_t.append(jnp.dot(lhs, rhs, preferred_element_type=F32))
            yield
        heads = range(g * HEADS_PER_SSM_GROUP, (g + 1) * HEADS_PER_SSM_GROUP)
        grows = slice(heads[0] * SSM_HEAD_DIM, (heads[-1] + 1) * SSM_HEAD_DIM)
        def scale_heads(x, t):
            hp = x.reshape(HEADS_PER_SSM_GROUP, SSM_HEAD_DIM, q) * t[heads[0]:heads[-1] + 1][:, None, :]
            return hp.reshape(x.shape)
        upd = jnp.dot(scale_heads(xs_t, w_t).astype(BF16), bm16, preferred_element_type=F32)
        state_ref[grows, :] = scale_heads(state_ref[grows, :], state_decay) + upd
        yield
        y = jnp.concatenate(y_t, axis=0).T
        cols = slice(g * gw, (g + 1) * gw)
        y = y + xs_g * dskip_ref[:, cols]
        z = z_ref[0, 0, rq, cols].astype(F32)
        yg = y * (z * _sigmoid(z))
        yg = yg * lax.rsqrt(jnp.mean(yg * yg, axis=-1, keepdims=True) + EPS)
        ssm_ref[rq, cols] = (yg * nw_ref[:, cols]).astype(ssm_ref.dtype)
        yield

    slab_ref[:, 0:SUBLANES, :] = slab_ref[:, q:q + SUBLANES, :]


def _merge_ffn_stages(r0, tm, attn_refs, ssm_ref, gate_ref, h_ref, wa_ref, ws_ref, wo_ref, nat_ref,
                      nw_ref, wg_ref, wu_ref, wd_ref, fw_ref, out_ref, *, ffn_chunk, final_norm):
    rows = slice(r0, r0 + tm)
    n_slabs = GROUP_WIDTH // LANES
    for k, src in enumerate(attn_refs):
        dil = src.shape[1]
        for r in range(dil if dil > 1 else 0):
            for c in range(n_slabs):
                nat_ref[k, c, pl.ds(r0 + r, tm // dil, stride=dil), :] = (
                    src[0, r, r0 // dil:(r0 + tm) // dil, c * LANES:(c + 1) * LANES])

    def natural(k, c):
        src = attn_refs[k]
        return src[0, 0, rows, c * LANES:(c + 1) * LANES] if src.shape[1] == 1 else nat_ref[k, c, rows, :]

    slabs = []
    for c in range(n_slabs):
        o = [natural(2 * g, c) for g in range(N_DIL)]
        l = [natural(2 * g + 1, c) for g in range(N_DIL)]
        mx = functools.reduce(jnp.maximum, l)
        e = [jnp.exp(lg - mx) for lg in l]
        num = functools.reduce(jnp.add, [eg * og for eg, og in zip(e, o)])
        slabs.append((num / functools.reduce(jnp.add, e)).astype(BF16))
    attn = jnp.concatenate(slabs, axis=1)
    yield
    a = jnp.dot(attn, wa_ref[...], preferred_element_type=F32)
    yield
    s = jnp.dot(ssm_ref[rows, :], ws_ref[...], preferred_element_type=F32)
    d = a.shape[1]
    gates = _sigmoid(gate_ref[0, 0, rows, :].astype(F32))
    merged = (gates[:, :d] * a + gates[:, d:] * s).astype(BF16)
    yield
    acc = h_ref[0, rows, :] + jnp.dot(merged, wo_ref[...], preferred_element_type=F32)
    xn = _rmsnorm(acc, nw_ref[...]).astype(BF16)
    yield
    chunks = [slice(c0, c0 + ffn_chunk) for c0 in range(0, wd_ref.shape[0], ffn_chunk)]
    gate_up = None
    for c, cols in enumerate(chunks + [None]):
        prev_cols, prev_gate_up = (chunks[c - 1], gate_up) if c > 0 else (None, None)
        if cols is not None:
            gate = jnp.dot(xn, wg_ref[:, cols], preferred_element_type=F32)
            yield
            gate_up = (gate, jnp.dot(xn, wu_ref[:, cols], preferred_element_type=F32))
            yield
        if prev_gate_up is not None:
            act = (prev_gate_up[0] * _sigmoid(prev_gate_up[0]) * prev_gate_up[1]).astype(BF16)
            acc = acc + jnp.dot(act, wd_ref[prev_cols, :], preferred_element_type=F32)
            yield
    out_ref[0, rows, :] = _rmsnorm(acc, fw_ref[...]) if final_norm else acc


def _tail_kernel(*refs, ffn_chunk, mf_rows, ssd_stages_per_matmul, ssd_lead, final_norm):
    attn_refs = refs[:2 * N_DIL]
    (gate_ref, h_ref, xs_ref, bc_ref, z_ref, dt_ref, wa_ref, ws_ref, wo_ref, n2_ref, wg_ref, wu_ref, wd_ref, fw_ref,
     cw_ref, cb_ref, dtb_ref, alog_ref, dskip_ref, nw_ref, out_ref,
     state_ref, slab_ref, ssm_ref, new_ref, nat_ref) = refs[2 * N_DIL:]

    @pl.when(pl.program_id(1) == 0)
    def _():
        state_ref[...] = jnp.zeros_like(state_ref)
        slab_ref[:, 0:SUBLANES, :] = jnp.zeros((slab_ref.shape[0], SUBLANES, LANES), F32)
        ssm_ref[...] = jnp.zeros_like(ssm_ref)

    def ssd_stages():
        for r0 in range(0, xs_ref.shape[2], SSD_CHUNK):
            decays = _ssd_decays(r0, dt_ref, dtb_ref, alog_ref)
            yield
            yield from _ssd_chunk_stages(r0, decays, xs_ref, bc_ref, z_ref, cw_ref, cb_ref, dskip_ref, nw_ref,
                                         new_ref, state_ref, slab_ref)

    tb = h_ref.shape[1]
    matmuls = [_merge_ffn_stages(r0, mf_rows, attn_refs, ssm_ref, gate_ref, h_ref, wa_ref, ws_ref, wo_ref,
                                 nat_ref, n2_ref, wg_ref, wu_ref, wd_ref, fw_ref, out_ref,
                                 ffn_chunk=ffn_chunk, final_norm=final_norm) for r0 in range(0, tb, mf_rows)]
    scan = ssd_stages()
    for _ in range(ssd_lead):
        next(scan)
    turns = [(gen, 1) for gen in matmuls] + [(scan, ssd_stages_per_matmul)]
    live = [gen for gen, _ in turns]
    while live:
        for gen, reps in turns:
            for _ in range(reps if gen in live else 0):
                if next(gen, StopIteration) is StopIteration:
                    live.remove(gen)
                    break
    ssm_ref[...] = new_ref[...]


def _tail(attn_outs, main, dt3, h3, weights, *, col_z, col_gates, col_xs, col_bc, tb, ffn_chunk,
          mf_rows, ssd_stages_per_matmul, ssd_lead, final_norm):
    b, s, d = h3.shape
    (w_attn, w_ssm, w_out, norm2_w, w_gate, w_up, w_down, final_w, conv_w, conv_b, dt_bias, a_log, d_skip,
     norm_w) = weights
    d_inner = norm_w.shape[1]
    conv_ch = conv_w.shape[1]
    bc_w = conv_ch - d_inner
    n_blocks = s // tb

    def cur(k):
        return jnp.minimum(k, n_blocks - 1)

    def prev(k):
        return jnp.maximum(k - 1, 0)

    def residues(arr):
        dil = arr.shape[1]
        return pl.BlockSpec((1, dil, tb // dil, GROUP_WIDTH), lambda bi, k: (bi, 0, prev(k), 0))

    flat = [a for pair in attn_outs for a in pair]
    in_specs = [residues(a) for a in flat] + [
        pl.BlockSpec((1, 1, tb, 2 * d), lambda bi, k: (bi, 0, prev(k), col_gates)),
        pl.BlockSpec((1, tb, d), lambda bi, k: (bi, prev(k), 0)),
        pl.BlockSpec((1, 1, tb, d_inner), lambda bi, k: (bi, 0, cur(k), col_xs)),
        pl.BlockSpec((1, 1, tb, bc_w), lambda bi, k: (bi, 0, cur(k), col_bc)),
        pl.BlockSpec((1, 1, tb, d_inner), lambda bi, k: (bi, 0, cur(k), col_z)),
        pl.BlockSpec((1, tb, LANES), lambda bi, k: (bi, cur(k), 0)),
    ] + [_resident(w.shape) for w in weights]
    return pl.pallas_call(
        functools.partial(_tail_kernel, ffn_chunk=ffn_chunk, ssd_stages_per_matmul=ssd_stages_per_matmul,
                          ssd_lead=ssd_lead, mf_rows=mf_rows, final_norm=final_norm),
        grid=(b, n_blocks + 1),
        in_specs=in_specs,
        out_specs=pl.BlockSpec((1, tb, d), lambda bi, k: (bi, prev(k), 0)),
        out_shape=jax.ShapeDtypeStruct((b, s, d), F32),
        scratch_shapes=[pltpu.VMEM((N_SSM_GROUPS * HEADS_PER_SSM_GROUP * SSM_HEAD_DIM, D_STATE), F32),
                        pltpu.VMEM((conv_ch // LANES, SUBLANES + SSD_CHUNK, LANES), F32),
                        pltpu.VMEM((tb, d_inner), BF16),
                        pltpu.VMEM((tb, d_inner), BF16),
                        pltpu.VMEM((len(flat), GROUP_WIDTH // LANES, tb, LANES), F32)],
        compiler_params=_compiler_params(("parallel", "arbitrary")),
        name="layer_tail",
    )(*flat, main, h3, main, main, main, dt3, *weights)


def kernel(x, norm1_w, w_in, conv_w, conv_b, dt_bias, a_log, d_skip, ssm_norm_w, w_attn_branch, w_ssm_branch,
           w_out, norm2_w, w_ffn_in, w_ffn_out, rel_bias, final_norm_w):
    b, s, d = x.shape
    t = b * s
    depth = w_in.shape[0]
    d_inner = ssm_norm_w.shape[1]
    n_heads = dt_bias.shape[1]
    bc_w = 2 * N_SSM_GROUPS * D_STATE
    d_ff = w_ffn_out.shape[1]
    assert d_inner == N_SSM_GROUPS * HEADS_PER_SSM_GROUP * SSM_HEAD_DIM and n_heads <= LANES
    assert s % (DILATED_GROUPS[-1][1] * ATTN_BLOCK) == 0 and s % SSD_CHUNK == 0

    o_z = 3 * ATTN_WIDTH
    o_xbc = o_z + d_inner
    o_dt = o_xbc + d_inner + bc_w
    o_gate = o_dt + n_heads

    def qkv_cols(g):
        return [w_in[:, :, part * ATTN_WIDTH + g * GROUP_WIDTH:part * ATTN_WIDTH + (g + 1) * GROUP_WIDTH]
                for part in range(3)]

    w_main = jnp.concatenate([w_in[:, :, o_z:o_xbc], w_in[:, :, o_gate:], w_in[:, :, o_xbc:o_dt]] + qkv_cols(0),
                             axis=2).astype(BF16)
    w_qkv = [None] + [jnp.concatenate(qkv_cols(g), axis=2).astype(BF16) for g in range(1, N_DIL)]
    w_dt = jnp.pad(w_in[:, :, o_dt:o_gate], ((0, 0), (0, 0), (0, LANES - n_heads))).astype(BF16)
    col_z = 0
    col_gates = d_inner // (2 * d)
    col_xs = (d_inner + 2 * d) // d_inner
    col_bc = (2 * d_inner + 2 * d) // bc_w
    col_q = (2 * d_inner + 2 * d + bc_w) // GROUP_WIDTH
    assert d_inner % (2 * d) == 0 and (d_inner + 2 * d) % d_inner == 0 and (2 * d_inner + 2 * d) % bc_w == 0
    assert (2 * d_inner + 2 * d + bc_w) % GROUP_WIDTH == 0

    pad_h = ((0, 0), (0, LANES - n_heads))
    dt_bias_p = jnp.pad(dt_bias, pad_h)
    a_log_p = jnp.pad(a_log, pad_h)
    d_skip_x = jnp.repeat(d_skip, SSM_HEAD_DIM, axis=1)
    w_attn16, w_ssm16, w_out16 = w_attn_branch.astype(BF16), w_ssm_branch.astype(BF16), w_out.astype(BF16)
    w_gate16, w_up16 = w_ffn_in[:, :, :d_ff].astype(BF16), w_ffn_in[:, :, d_ff:].astype(BF16)
    w_down16 = w_ffn_out.astype(BF16)
    bias_tables = _attention_bias_tables(rel_bias)

    h = x
    for layer in range(depth):
        nw1 = norm1_w[layer:layer + 1]
        main, dt = _in_proj(h, nw1, w_main[layer], w_dt[layer], dilation=1, tm=512, chunk=1024)

        attn_outs = []
        for g, (_, dil) in enumerate(DILATED_GROUPS):
            if dil == 1:
                qkv, cq = main, col_q
            else:
                (qkv,), cq = _in_proj(h, nw1, w_qkv[g][layer], dilation=dil, tm=1024, chunk=GROUP_WIDTH), 0
            attn_outs.append(_attention(qkv, bias_tables[g], col_q=cq, col_k=cq + 1, col_v=cq + 2,
                                        tq=min(2048, s // dil)))

        weights = (w_attn16[layer], w_ssm16[layer], w_out16[layer], norm2_w[layer:layer + 1], w_gate16[layer],
                   w_up16[layer], w_down16[layer], final_norm_w[None, :], conv_w[layer], conv_b[layer:layer + 1],
                   dt_bias_p[layer:layer + 1], a_log_p[layer:layer + 1], d_skip_x[layer:layer + 1],
                   ssm_norm_w[layer:layer + 1])
        h = _tail(attn_outs, main, dt, h, weights, col_z=col_z, col_gates=col_gates, col_xs=col_xs,
                  col_bc=col_bc, tb=256, ffn_chunk=256, mf_rows=256, ssd_stages_per_matmul=2, ssd_lead=0,
                  final_norm=(layer == depth - 1))
    return h
```

```python
import functools
import math

import jax
import jax.numpy as jnp
from jax import lax
from jax.experimental import pallas as pl
from jax.experimental.pallas import tpu as pltpu

HEAD_DIM = 64
DILATED_GROUPS = ((128, 1), (512, 4), (2048, 16))
N_DIL = len(DILATED_GROUPS)
HEADS_PER_GROUP = 8
GROUP_WIDTH = HEADS_PER_GROUP * HEAD_DIM
ATTN_WIDTH = N_DIL * GROUP_WIDTH
ATTN_BLOCK = 128
N_REL_BUCKETS = 32
REL_MAX_DISTANCE = 2048
SSM_HEAD_DIM = 64
N_SSM_GROUPS = 4
HEADS_PER_SSM_GROUP = 8
D_STATE = 128
CONV_WIDTH = 4
SSD_CHUNK = 128
EPS = 1e-6

LANES = 128
SUBLANES = 8
VMEM_LIMIT_BYTES = 60000 * 1024

IN_PROJ_ROWS = 512
IN_PROJ_DILATED_ROWS = 1024
IN_PROJ_CHUNK = 1024
ATTN_ROWS = 2048
TAIL_ROWS = 256
FFN_CHUNK = 256
SSD_STAGES_PER_MATMUL = 2

BF16 = jnp.bfloat16
F32 = jnp.float32
NT_DIMS = (((1,), (1,)), ((), ()))


def _compiler_params(semantics):
    return pltpu.CompilerParams(dimension_semantics=semantics, vmem_limit_bytes=VMEM_LIMIT_BYTES)


def _rmsnorm(x, w):
    return x * lax.rsqrt(jnp.mean(x * x, axis=-1, keepdims=True) + EPS) * w


def _sigmoid(x):
    return 1.0 / (1.0 + jnp.exp(-x))


PERM_BLOCK = 256


def _in_proj_kernel(x_ref, nw_ref, *rest, dilation, pieces, dt_piece, chunk):
    n_w = 1 + max(p[0] for p in pieces)
    w_refs = [r.at[0] for r in rest[:n_w]]
    if dt_piece is not None:
        out_ref, dt_ref, xn_ref = rest[n_w:]
    else:
        out_ref, xn_ref = rest[n_w:]
    nw_ref = nw_ref.at[0]
    tm = x_ref.shape[1]
    xn = _rmsnorm(x_ref[0], nw_ref[...]).astype(BF16)
    if dt_piece is not None:
        src, c0, width = dt_piece
        dt_ref[0] = jnp.dot(xn, w_refs[src][:, c0:c0 + width], preferred_element_type=F32)
    if dilation == 1:
        xn_ref[...] = xn
    else:
        per = PERM_BLOCK // dilation
        i = lax.broadcasted_iota(jnp.int32, (PERM_BLOCK, PERM_BLOCK), 0)
        k = lax.broadcasted_iota(jnp.int32, (PERM_BLOCK, PERM_BLOCK), 1)
        src = (i & (per - 1)) * dilation + (i >> (per.bit_length() - 1))
        perm = jnp.where(k == src, 1.0, 0.0).astype(BF16)
        for u in range(tm // PERM_BLOCK):
            y = jnp.dot(perm, xn[u * PERM_BLOCK:(u + 1) * PERM_BLOCK], preferred_element_type=F32).astype(BF16)
            for r in range(dilation):
                dst = r * (tm // dilation) + u * per
                xn_ref[dst:dst + per, :] = y[r * per:(r + 1) * per]
    col = 0
    for src, start, width in pieces:
        for c0 in range(0, width, chunk):
            cw = min(chunk, width - c0)
            res = jnp.dot(xn_ref[...], w_refs[src][:, start + c0:start + c0 + cw], preferred_element_type=F32)
            out_ref[0, :, :, col:col + cw] = res.astype(out_ref.dtype).reshape(dilation, tm // dilation, cw)
            col += cw


def _layer_spec(arr, layer, block=None, col_block=0):
    shape = (1,) + tuple(arr.shape[1:-1]) + (block or arr.shape[-1],)
    index = (layer,) + (0,) * (arr.ndim - 2) + (col_block,)
    return pl.BlockSpec(shape, lambda *_: index, pipeline_mode=pl.Buffered(1))


def _in_proj(h3, norm_w, weights, pieces, dt_piece=None, *, layer, dilation, tm, chunk):
    b, s, d = h3.shape
    n = sum(p[2] for p in pieces)
    in_specs = [pl.BlockSpec((1, tm, d), lambda bi, i: (bi, i, 0)), _layer_spec(norm_w, layer)]
    in_specs += [_layer_spec(arr, layer, blk, cb) for arr, blk, cb in weights]
    out_specs = [pl.BlockSpec((1, dilation, tm // dilation, n), lambda bi, i: (bi, 0, i, 0))]
    out_shape = [jax.ShapeDtypeStruct((b, dilation, s // dilation, n), BF16)]
    if dt_piece is not None:
        out_specs.append(pl.BlockSpec((1, tm, LANES), lambda bi, i: (bi, i, 0)))
        out_shape.append(jax.ShapeDtypeStruct((b, s, LANES), F32))
    return pl.pallas_call(
        functools.partial(_in_proj_kernel, dilation=dilation, pieces=pieces, dt_piece=dt_piece, chunk=chunk),
        grid=(b, s // tm),
        in_specs=in_specs,
        out_specs=out_specs,
        out_shape=out_shape,
        scratch_shapes=[pltpu.VMEM((tm, d), BF16)],
        compiler_params=_compiler_params(("parallel", "parallel")),
        name=f"in_proj_d{dilation}",
    )(h3, norm_w, *[arr for arr, _, _ in weights])


def _attention_kernel(q_ref, kp_ref, kc_ref, vp_ref, vc_ref, bias_ref, o_ref, l_ref, kext_ref, vext_ref, *, tq):
    n = pl.program_id(2)
    kext_ref[0:ATTN_BLOCK] = kp_ref[0, 0]
    kext_ref[ATTN_BLOCK:] = kc_ref[0, 0]
    vext_ref[0:ATTN_BLOCK] = vp_ref[0, 0]
    vext_ref[ATTN_BLOCK:] = vc_ref[0, 0]
    low_half = lax.broadcasted_iota(jnp.int32, (1, LANES), 1) < HEAD_DIM

    for s in range(tq // ATTN_BLOCK):
        r0 = s * ATTN_BLOCK
        qs = q_ref[0, 0, pl.ds(r0, ATTN_BLOCK), :] * (HEAD_DIM ** -0.5)
        ks = kext_ref[pl.ds(r0, 2 * ATTN_BLOCK), :]
        vs = vext_ref[pl.ds(r0, 2 * ATTN_BLOCK), :]
        bidx = jnp.where(n == 0, 0, 1) if s == 0 else 1
        for p in range(GROUP_WIDTH // LANES):
            cols = slice(p * LANES, (p + 1) * LANES)
            qp, kp, vp = qs[:, cols], ks[:, cols], vs[:, cols]
            q2 = jnp.concatenate([jnp.where(low_half, qp, jnp.zeros_like(qp)),
                                  jnp.where(low_half, jnp.zeros_like(qp), qp)], axis=0)
            bias2 = jnp.concatenate([bias_ref[bidx, 2 * p], bias_ref[bidx, 2 * p + 1]], axis=0)
            sc = lax.dot_general(q2, kp, NT_DIMS, preferred_element_type=F32) + bias2
            m = jnp.max(sc, axis=1, keepdims=True)
            pe = jnp.exp(sc - m).astype(BF16)
            nd = jnp.dot(pe, jnp.concatenate([vp, jnp.ones_like(vp)], axis=1), preferred_element_type=F32)
            num, den = nd[:, :LANES], nd[:, LANES:]
            o2 = num / den
            l2 = m + jnp.log(den)
            outs = [o2[:ATTN_BLOCK], o2[ATTN_BLOCK:]]
            lses = [l2[:ATTN_BLOCK], l2[ATTN_BLOCK:]]
            o_ref[0, 0, pl.ds(r0, ATTN_BLOCK), cols] = jnp.where(low_half, outs[0], outs[1])
            l_ref[0, 0, pl.ds(r0, ATTN_BLOCK), cols] = jnp.where(low_half, lses[0], lses[1])


def _attention(qkv, bias_g, *, col_q, col_k, col_v, tq):
    b, dilation, seg, _ = qkv.shape
    blocks_per_tile = tq // ATTN_BLOCK

    def cur(col):
        return pl.BlockSpec((1, 1, tq, GROUP_WIDTH), lambda bi, r, n: (bi, r, n, col))

    def prev(col):
        return pl.BlockSpec((1, 1, ATTN_BLOCK, GROUP_WIDTH),
                            lambda bi, r, n: (bi, r, jnp.maximum(n * blocks_per_tile - 1, 0), col))

    out_spec = pl.BlockSpec((1, 1, tq, GROUP_WIDTH), lambda bi, r, n: (bi, r, n, 0))
    out_sds = jax.ShapeDtypeStruct((b, dilation, seg, GROUP_WIDTH), F32)
    return pl.pallas_call(
        functools.partial(_attention_kernel, tq=tq),
        grid=(b, dilation, seg // tq),
        in_specs=[cur(col_q), prev(col_k), cur(col_k), prev(col_v), cur(col_v),
                  pl.BlockSpec(bias_g.shape, lambda bi, r, n: (0, 0, 0, 0))],
        out_specs=[out_spec, out_spec],
        out_shape=[out_sds, out_sds],
        scratch_shapes=[pltpu.VMEM((tq + ATTN_BLOCK, GROUP_WIDTH), BF16),
                        pltpu.VMEM((tq + ATTN_BLOCK, GROUP_WIDTH), BF16)],
        compiler_params=_compiler_params(("parallel", "parallel", "arbitrary")),
        name=f"attention_d{dilation}",
    )(qkv, qkv, qkv, qkv, qkv, bias_g)


def _t5_causal_bucket(dist):
    max_exact = N_REL_BUCKETS // 2
    d_f = jnp.maximum(dist, 1).astype(F32)
    large = max_exact + (jnp.log(d_f / max_exact) / math.log(REL_MAX_DISTANCE / max_exact)
                         * (N_REL_BUCKETS - max_exact)).astype(jnp.int32)
    large = jnp.minimum(large, N_REL_BUCKETS - 1)
    return jnp.where(dist < max_exact, dist, large)


def _attention_bias_tables(rel_bias):
    q, q2 = ATTN_BLOCK, 2 * ATTN_BLOCK
    steps = jnp.arange(q2) - (q - 1)
    in_prev = (jnp.arange(q2) < q)[None, None, :]
    tables = []
    for g, (window, dil) in enumerate(DILATED_GROUPS):
        n_steps = window // dil
        assert n_steps <= q
        rel_g = rel_bias[:, g * HEADS_PER_GROUP:(g + 1) * HEADS_PER_GROUP].astype(F32)
        vals = rel_g[_t5_causal_bucket(jnp.clip(steps, 0, n_steps) * dil)]
        vec = jnp.where(((steps >= 0) & (steps <= n_steps))[:, None], vals, -jnp.inf).T
        skew = jnp.tile(vec, (1, q + 1))[:, :q * (q2 + 1)].reshape(HEADS_PER_GROUP, q, q2 + 1)[:, :, :q2]
        rest = skew[:, :, ::-1]
        first = jnp.where(in_prev, -jnp.inf, rest)
        tables.append(jnp.stack([first, rest]))
    return jnp.stack(tables)


LOG2E = math.log2(math.e)


def _conv_silu(slab_ref, c, w_ref, b_ref):
    q = slab_ref.shape[1] - SUBLANES
    cols = slice(c * LANES, (c + 1) * LANES)
    acc = b_ref[:, cols] + slab_ref[c, pl.ds(SUBLANES, q), :] * w_ref[CONV_WIDTH - 1:CONV_WIDTH, cols]
    for s in range(1, CONV_WIDTH):
        acc = acc + slab_ref[c, pl.ds(SUBLANES - s, q), :] * w_ref[CONV_WIDTH - 1 - s:CONV_WIDTH - s, cols]
    return acc * _sigmoid(acc)


def _ssd_decays(r0, dt_ref, dtb_ref, alog_ref):
    q = SSD_CHUNK
    x_dt = dt_ref[0, r0:r0 + q, :] + dtb_ref[...]
    dt = jnp.maximum(x_dt, 0.0) + jnp.log(1.0 + jnp.exp(-jnp.abs(x_dt)))
    d_a = dt * (-jnp.exp(alog_ref[...]))
    ri = lax.broadcasted_iota(jnp.int32, (q, q), 0)
    ci = lax.broadcasted_iota(jnp.int32, (q, q), 1)
    causal_t = ri <= ci
    tril = jnp.where(ri >= ci, 1.0, 0.0).astype(BF16)
    d1 = d_a.astype(BF16)
    r1 = d_a - d1.astype(F32)
    d2 = r1.astype(BF16)
    d3 = (r1 - d2.astype(F32)).astype(BF16)
    la = (jnp.dot(tril, d1, preferred_element_type=F32) + jnp.dot(tril, d2, preferred_element_type=F32)
          + jnp.dot(tril, d3, preferred_element_type=F32))
    la = la * LOG2E
    la_t = la.T
    dt_t = dt.T
    last = jnp.broadcast_to(la_t[:, q - 1:q], (q, q))
    w_t = jnp.exp2(last - la_t) * dt_t
    state_decay = jnp.exp2(last)
    return causal_t, la, la_t, dt_t, w_t, state_decay


def _ssd_chunk_stages(r0, decays, xs_ref, bc_ref, z_ref, cw_ref, cb_ref, dskip_ref, nw_ref,
                      ssm_ref, state_ref, slab_ref):
    causal_t, la, la_t, dt_t, w_t, state_decay = decays
    q = SSD_CHUNK
    rq = slice(r0, r0 + q)
    gw = HEADS_PER_SSM_GROUP * SSM_HEAD_DIM
    xs_slabs = xs_ref.shape[3] // LANES
    slabs_per_group = gw // LANES

    for c in range(xs_slabs):
        slab_ref[c, SUBLANES:, :] = xs_ref[0, 0, rq, c * LANES:(c + 1) * LANES].astype(F32)
    for c in range(bc_ref.shape[3] // LANES):
        slab_ref[xs_slabs + c, SUBLANES:, :] = bc_ref[0, 0, rq, c * LANES:(c + 1) * LANES].astype(F32)

    for g in range(N_SSM_GROUPS):
        bm = _conv_silu(slab_ref, xs_slabs + g, cw_ref, cb_ref)
        cm = _conv_silu(slab_ref, xs_slabs + N_SSM_GROUPS + g, cw_ref, cb_ref)
        bm16 = bm.astype(BF16)
        cb_t = lax.dot_general(bm16, cm.astype(BF16), NT_DIMS, preferred_element_type=F32)
        cm_t = cm.T
        xs_g = jnp.concatenate([_conv_silu(slab_ref, g * slabs_per_group + c, cw_ref, cb_ref)
                                for c in range(slabs_per_group)], axis=1)
        xs_t = xs_g.T
        yield
        y_t = []
        for h in range(HEADS_PER_SSM_GROUP):
            hh = g * HEADS_PER_SSM_GROUP + h
            rows = slice(hh * SSM_HEAD_DIM, (hh + 1) * SSM_HEAD_DIM)
            la_i = la_t[hh:hh + 1, :]
            la_j = jnp.broadcast_to(la[:, hh:hh + 1], (q, q))
            decay_t = jnp.exp2(jnp.where(causal_t, la_i - la_j, -jnp.inf))
            rhs = jnp.concatenate([(cb_t * decay_t).astype(BF16), (cm_t * jnp.exp2(la_i)).astype(BF16)], axis=0)
            x_h = xs_t[h * SSM_HEAD_DIM:(h + 1) * SSM_HEAD_DIM]
            state = state_ref[rows, :]
            lhs = jnp.concatenate([(x_h * dt_t[hh:hh + 1, :]).astype(BF16), state.astype(BF16)], axis=1)
            y_t.append(jnp.dot(lhs, rhs, preferred_element_type=F32))
            yield
        heads = range(g * HEADS_PER_SSM_GROUP, (g + 1) * HEADS_PER_SSM_GROUP)
        grows = slice(heads[0] * SSM_HEAD_DIM, (heads[-1] + 1) * SSM_HEAD_DIM)
        def scale_heads(x, t):
            hp = x.reshape(HEADS_PER_SSM_GROUP, SSM_HEAD_DIM, q) * t[heads[0]:heads[-1] + 1][:, None, :]
            return hp.reshape(x.shape)
        upd = jnp.dot(scale_heads(xs_t, w_t).astype(BF16), bm16, preferred_element_type=F32)
        state_ref[grows, :] = scale_heads(state_ref[grows, :], state_decay) + upd
        yield
        y = jnp.concatenate(y_t, axis=0).T
        cols = slice(g * gw, (g + 1) * gw)
        y = y + xs_g * dskip_ref[:, cols]
        z = z_ref[0, 0, rq, cols].astype(F32)
        yg = y * (z * _sigmoid(z))
        yg = yg * lax.rsqrt(jnp.mean(yg * yg, axis=-1, keepdims=True) + EPS)
        ssm_ref[rq, cols] = (yg * nw_ref[:, cols]).astype(ssm_ref.dtype)
        yield

    slab_ref[:, 0:SUBLANES, :] = slab_ref[:, q:q + SUBLANES, :]


def _merge_ffn_stages(r0, tm, attn_refs, ssm_ref, gate_ref, h_ref, wa_ref, ws_ref, wo_ref, nat_ref,
                      nw_ref, wgu_ref, wd_ref, fw_ref, out_ref, *, ffn_chunk, final_norm):
    rows = slice(r0, r0 + tm)
    n_slabs = GROUP_WIDTH // LANES
    for k, src in enumerate(attn_refs):
        dil = src.shape[1]
        for r in range(dil if dil > 1 else 0):
            for c in range(n_slabs):
                nat_ref[k, c, pl.ds(r0 + r, tm // dil, stride=dil), :] = (
                    src[0, r, r0 // dil:(r0 + tm) // dil, c * LANES:(c + 1) * LANES])

    def natural(k, c):
        src = attn_refs[k]
        return src[0, 0, rows, c * LANES:(c + 1) * LANES] if src.shape[1] == 1 else nat_ref[k, c, rows, :]

    slabs = []
    for c in range(n_slabs):
        o = [natural(2 * g, c) for g in range(N_DIL)]
        l = [natural(2 * g + 1, c) for g in range(N_DIL)]
        mx = functools.reduce(jnp.maximum, l)
        e = [jnp.exp(lg - mx) for lg in l]
        num = functools.reduce(jnp.add, [eg * og for eg, og in zip(e, o)])
        slabs.append((num / functools.reduce(jnp.add, e)).astype(BF16))
    attn = jnp.concatenate(slabs, axis=1)
    yield
    a = jnp.dot(attn, wa_ref[...], preferred_element_type=F32)
    yield
    s = jnp.dot(ssm_ref[rows, :], ws_ref[...], preferred_element_type=F32)
    d = a.shape[1]
    gates = _sigmoid(gate_ref[0, 0, rows, :].astype(F32))
    merged = (gates[:, :d] * a + gates[:, d:] * s).astype(BF16)
    yield
    acc = h_ref[0, rows, :] + jnp.dot(merged, wo_ref[...], preferred_element_type=F32)
    xn = _rmsnorm(acc, nw_ref[...]).astype(BF16)
    yield
    d_ff = wd_ref.shape[0]
    chunks = [slice(c0, c0 + ffn_chunk) for c0 in range(0, d_ff, ffn_chunk)]
    gate_up = None
    for c, cols in enumerate(chunks + [None]):
        prev_cols, prev_gate_up = (chunks[c - 1], gate_up) if c > 0 else (None, None)
        if cols is not None:
            up_cols = slice(d_ff + cols.start, d_ff + cols.stop)
            gate = jnp.dot(xn, wgu_ref[:, cols], preferred_element_type=F32)
            yield
            gate_up = (gate, jnp.dot(xn, wgu_ref[:, up_cols], preferred_element_type=F32))
            yield
        if prev_gate_up is not None:
            act = (prev_gate_up[0] * _sigmoid(prev_gate_up[0]) * prev_gate_up[1]).astype(BF16)
            acc = acc + jnp.dot(act, wd_ref[prev_cols, :], preferred_element_type=F32)
            yield
    out_ref[0, rows, :] = _rmsnorm(acc, fw_ref[...]) if final_norm else acc


N_TAIL_WEIGHTS = 13


def _tail_kernel(*refs, ffn_chunk, ssd_stages_per_matmul, final_norm):
    attn_refs = refs[:2 * N_DIL]
    gate_ref, h_ref, xs_ref, bc_ref, z_ref, dt_ref = refs[2 * N_DIL:2 * N_DIL + 6]
    weight_refs = refs[2 * N_DIL + 6:2 * N_DIL + 6 + N_TAIL_WEIGHTS]
    (wa_ref, ws_ref, wo_ref, n2_ref, wgu_ref, wd_ref, fw_ref,
     cw_ref, cb_ref, dtb_ref, alog_ref, dskip_ref, nw_ref) = [r.at[0] for r in weight_refs]
    out_ref, state_ref, slab_ref, ssm_ref, new_ref, nat_ref = refs[2 * N_DIL + 6 + N_TAIL_WEIGHTS:]

    @pl.when(pl.program_id(1) == 0)
    def _():
        state_ref[...] = jnp.zeros_like(state_ref)
        slab_ref[:, 0:SUBLANES, :] = jnp.zeros((slab_ref.shape[0], SUBLANES, LANES), F32)
        ssm_ref[...] = jnp.zeros_like(ssm_ref)

    def ssd_stages():
        for r0 in range(0, xs_ref.shape[2], SSD_CHUNK):
            decays = _ssd_decays(r0, dt_ref, dtb_ref, alog_ref)
            yield
            yield from _ssd_chunk_stages(r0, decays, xs_ref, bc_ref, z_ref, cw_ref, cb_ref, dskip_ref, nw_ref,
                                         new_ref, state_ref, slab_ref)

    matmuls = _merge_ffn_stages(0, h_ref.shape[1], attn_refs, ssm_ref, gate_ref, h_ref, wa_ref, ws_ref, wo_ref,
                                nat_ref, n2_ref, wgu_ref, wd_ref, fw_ref, out_ref,
                                ffn_chunk=ffn_chunk, final_norm=final_norm)
    turns = [(matmuls, 1), (ssd_stages(), ssd_stages_per_matmul)]
    live = [gen for gen, _ in turns]
    while live:
        for gen, reps in turns:
            for _ in range(reps if gen in live else 0):
                if next(gen, StopIteration) is StopIteration:
                    live.remove(gen)
                    break
    ssm_ref[...] = new_ref[...]


def _tail(attn_outs, main, dt3, h3, weights, *, layer, col_z, col_gates, col_xs, col_bc, tb, ffn_chunk,
          ssd_stages_per_matmul, final_norm):
    b, s, d = h3.shape
    assert len(weights) == N_TAIL_WEIGHTS
    conv_w, norm_w = weights[7], weights[12]
    d_inner = norm_w.shape[-1]
    conv_ch = conv_w.shape[-1]
    bc_w = conv_ch - d_inner
    n_blocks = s // tb

    def cur(k):
        return jnp.minimum(k, n_blocks - 1)

    def prev(k):
        return jnp.maximum(k - 1, 0)

    def residues(arr):
        dil = arr.shape[1]
        return pl.BlockSpec((1, dil, tb // dil, GROUP_WIDTH), lambda bi, k: (bi, 0, prev(k), 0))

    flat = [a for pair in attn_outs for a in pair]
    in_specs = [residues(a) for a in flat] + [
        pl.BlockSpec((1, 1, tb, 2 * d), lambda bi, k: (bi, 0, prev(k), col_gates)),
        pl.BlockSpec((1, tb, d), lambda bi, k: (bi, prev(k), 0)),
        pl.BlockSpec((1, 1, tb, d_inner), lambda bi, k: (bi, 0, cur(k), col_xs)),
        pl.BlockSpec((1, 1, tb, bc_w), lambda bi, k: (bi, 0, cur(k), col_bc)),
        pl.BlockSpec((1, 1, tb, d_inner), lambda bi, k: (bi, 0, cur(k), col_z)),
        pl.BlockSpec((1, tb, LANES), lambda bi, k: (bi, cur(k), 0)),
    ] + [_layer_spec(w, min(layer, w.shape[0] - 1)) for w in weights]
    return pl.pallas_call(
        functools.partial(_tail_kernel, ffn_chunk=ffn_chunk, ssd_stages_per_matmul=ssd_stages_per_matmul,
                          final_norm=final_norm),
        grid=(b, n_blocks + 1),
        in_specs=in_specs,
        out_specs=pl.BlockSpec((1, tb, d), lambda bi, k: (bi, prev(k), 0)),
        out_shape=jax.ShapeDtypeStruct((b, s, d), F32),
        scratch_shapes=[pltpu.VMEM((N_SSM_GROUPS * HEADS_PER_SSM_GROUP * SSM_HEAD_DIM, D_STATE), F32),
                        pltpu.VMEM((conv_ch // LANES, SUBLANES + SSD_CHUNK, LANES), F32),
                        pltpu.VMEM((tb, d_inner), BF16),
                        pltpu.VMEM((tb, d_inner), BF16),
                        pltpu.VMEM((len(flat), GROUP_WIDTH // LANES, tb, LANES), F32)],
        compiler_params=_compiler_params(("parallel", "arbitrary")),
        name="layer_tail",
    )(*flat, main, h3, main, main, main, dt3, *weights)


def kernel(x, norm1_w, w_in, conv_w, conv_b, dt_bias, a_log, d_skip, ssm_norm_w, w_attn_branch, w_ssm_branch,
           w_out, norm2_w, w_ffn_in, w_ffn_out, rel_bias, final_norm_w):
    b, s, d = x.shape
    depth = w_in.shape[0]
    d_inner = ssm_norm_w.shape[1]
    n_heads = dt_bias.shape[1]
    bc_w = 2 * N_SSM_GROUPS * D_STATE
    assert d_inner == N_SSM_GROUPS * HEADS_PER_SSM_GROUP * SSM_HEAD_DIM and n_heads <= LANES
    assert s % (DILATED_GROUPS[-1][1] * ATTN_BLOCK) == 0 and s % SSD_CHUNK == 0

    o_z = 3 * ATTN_WIDTH
    o_xbc = o_z + d_inner
    o_dt = o_xbc + d_inner + bc_w
    o_gate = o_dt + n_heads
    assert o_dt % LANES == 0 and o_dt + LANES <= w_in.shape[2]
    w_in16 = w_in.astype(BF16)
    w_gates16 = w_in[:, :, o_gate:].astype(BF16)
    qkv_start = lambda part, g: part * ATTN_WIDTH + g * GROUP_WIDTH
    main_weights = [(w_in16, None, 0), (w_gates16, None, 0)]
    main_pieces = ((0, o_z, d_inner), (1, 0, 2 * d), (0, o_xbc, d_inner + bc_w)) + tuple(
        (0, qkv_start(part, 0), GROUP_WIDTH) for part in range(3))
    col_z = 0
    col_gates = d_inner // (2 * d)
    col_xs = (d_inner + 2 * d) // d_inner
    col_bc = (2 * d_inner + 2 * d) // bc_w
    col_q = (2 * d_inner + 2 * d + bc_w) // GROUP_WIDTH
    assert d_inner % (2 * d) == 0 and (d_inner + 2 * d) % d_inner == 0 and (2 * d_inner + 2 * d) % bc_w == 0
    assert (2 * d_inner + 2 * d + bc_w) % GROUP_WIDTH == 0

    pad_h = ((0, 0), (0, 0), (0, LANES - n_heads))
    row3 = lambda a: a[:, None, :]
    tail_weights = (w_attn_branch.astype(BF16), w_ssm_branch.astype(BF16), w_out.astype(BF16), row3(norm2_w),
                    w_ffn_in.astype(BF16), w_ffn_out.astype(BF16), final_norm_w[None, None, :],
                    conv_w, row3(conv_b), jnp.pad(row3(dt_bias), pad_h), jnp.pad(row3(a_log), pad_h),
                    row3(jnp.repeat(d_skip, SSM_HEAD_DIM, axis=1)), row3(ssm_norm_w))
    norm1_w3 = row3(norm1_w)
    bias_tables = _attention_bias_tables(rel_bias)

    h = x
    for layer in range(depth):
        main, dt = _in_proj(h, norm1_w3, main_weights, main_pieces, (0, o_dt, LANES), layer=layer,
                            dilation=1, tm=IN_PROJ_ROWS, chunk=IN_PROJ_CHUNK)
        attn_outs = []
        for g, (_, dil) in enumerate(DILATED_GROUPS):
            if dil == 1:
                qkv, cq = main, col_q
            else:
                weights = [(w_in16, GROUP_WIDTH, qkv_start(part, g) // GROUP_WIDTH) for part in range(3)]
                pieces = tuple((part, 0, GROUP_WIDTH) for part in range(3))
                (qkv,), cq = _in_proj(h, norm1_w3, weights, pieces, layer=layer, dilation=dil,
                                      tm=IN_PROJ_DILATED_ROWS, chunk=IN_PROJ_CHUNK), 0
            attn_outs.append(_attention(qkv, bias_tables[g], col_q=cq, col_k=cq + 1, col_v=cq + 2,
                                        tq=min(ATTN_ROWS, s // dil)))
        h = _tail(attn_outs, main, dt, h, tail_weights, layer=layer, col_z=col_z, col_gates=col_gates,
                  col_xs=col_xs, col_bc=col_bc, tb=TAIL_ROWS, ffn_chunk=FFN_CHUNK,
                  ssd_stages_per_matmul=SSD_STAGES_PER_MATMUL, final_norm=(layer == depth - 1))
    return h
```

```python
import functools
import math

import jax
import jax.numpy as jnp
from jax import lax
from jax.experimental import pallas as pl
from jax.experimental.pallas import tpu as pltpu

HEAD_DIM = 64
DILATED_GROUPS = ((128, 1), (512, 4), (2048, 16))
N_DIL = len(DILATED_GROUPS)
HEADS_PER_GROUP = 8
GROUP_WIDTH = HEADS_PER_GROUP * HEAD_DIM
ATTN_WIDTH = N_DIL * GROUP_WIDTH
ATTN_BLOCK = 128
N_REL_BUCKETS = 32
REL_MAX_DISTANCE = 2048
SSM_HEAD_DIM = 64
N_SSM_GROUPS = 4
HEADS_PER_SSM_GROUP = 8
D_STATE = 128
CONV_WIDTH = 4
SSD_CHUNK = 128
EPS = 1e-6

LANES = 128
SUBLANES = 8
VMEM_LIMIT_BYTES = 60000 * 1024

IN_PROJ_ROWS = 512
IN_PROJ_DILATED_ROWS = 1024
IN_PROJ_CHUNK = 1024
ATTN_ROWS = 2048
TAIL_ROWS = 256
FFN_CHUNK = 256
SSD_STAGES_PER_MATMUL = 2

BF16 = jnp.bfloat16
F32 = jnp.float32
NT_DIMS = (((1,), (1,)), ((), ()))


def _compiler_params(semantics):
    return pltpu.CompilerParams(dimension_semantics=semantics, vmem_limit_bytes=VMEM_LIMIT_BYTES)


def _rmsnorm(x, w):
    return x * lax.rsqrt(jnp.mean(x * x, axis=-1, keepdims=True) + EPS) * w


def _sigmoid(x):
    return 1.0 / (1.0 + jnp.exp(-x))


PERM_BLOCK = 256


def _in_proj_kernel(x_ref, nw_ref, *rest, dilation, outputs, dt_piece, chunk):
    n_w = 1 + max(p[0] for pieces in outputs for p in pieces)
    w_refs = [r.at[0] for r in rest[:n_w]]
    out_refs = rest[n_w:n_w + len(outputs)]
    if dt_piece is not None:
        dt_ref, xn_ref = rest[n_w + len(outputs):]
    else:
        (xn_ref,) = rest[n_w + len(outputs):]
    nw_ref = nw_ref.at[0]
    tm = x_ref.shape[1]
    xn = _rmsnorm(x_ref[0], nw_ref[...]).astype(BF16)
    if dt_piece is not None:
        src, c0, width = dt_piece
        dt_ref[0] = jnp.dot(xn, w_refs[src][:, c0:c0 + width], preferred_element_type=F32)
    if dilation == 1:
        xn_ref[...] = xn
    else:
        per = PERM_BLOCK // dilation
        i = lax.broadcasted_iota(jnp.int32, (PERM_BLOCK, PERM_BLOCK), 0)
        k = lax.broadcasted_iota(jnp.int32, (PERM_BLOCK, PERM_BLOCK), 1)
        src = (i & (per - 1)) * dilation + (i >> (per.bit_length() - 1))
        perm = jnp.where(k == src, 1.0, 0.0).astype(BF16)
        for u in range(tm // PERM_BLOCK):
            y = jnp.dot(perm, xn[u * PERM_BLOCK:(u + 1) * PERM_BLOCK], preferred_element_type=F32).astype(BF16)
            for r in range(dilation):
                dst = r * (tm // dilation) + u * per
                xn_ref[dst:dst + per, :] = y[r * per:(r + 1) * per]
    for out_ref, pieces in zip(out_refs, outputs):
        col = 0
        for src, start, width in pieces:
            for c0 in range(0, width, chunk):
                cw = min(chunk, width - c0)
                res = jnp.dot(xn_ref[...], w_refs[src][:, start + c0:start + c0 + cw], preferred_element_type=F32)
                out_ref[0, :, :, col:col + cw] = res.astype(out_ref.dtype).reshape(dilation, tm // dilation, cw)
                col += cw


def _layer_spec(arr, layer, block=None, col_block=0):
    shape = (1,) + tuple(arr.shape[1:-1]) + (block or arr.shape[-1],)
    index = (layer,) + (0,) * (arr.ndim - 2) + (col_block,)
    return pl.BlockSpec(shape, lambda *_: index, pipeline_mode=pl.Buffered(1))


def _in_proj(h3, norm_w, weights, outputs, dt_piece=None, *, layer, dilation, tm, chunk):
    b, s, d = h3.shape
    in_specs = [pl.BlockSpec((1, tm, d), lambda bi, i: (bi, i, 0)), _layer_spec(norm_w, layer)]
    in_specs += [_layer_spec(arr, layer, blk, cb) for arr, blk, cb in weights]
    widths = [sum(p[2] for p in pieces) for pieces in outputs]
    out_specs = [pl.BlockSpec((1, dilation, tm // dilation, n), lambda bi, i: (bi, 0, i, 0)) for n in widths]
    out_shape = [jax.ShapeDtypeStruct((b, dilation, s // dilation, n), BF16) for n in widths]
    if dt_piece is not None:
        out_specs.append(pl.BlockSpec((1, tm, LANES), lambda bi, i: (bi, i, 0)))
        out_shape.append(jax.ShapeDtypeStruct((b, s, LANES), F32))
    return pl.pallas_call(
        functools.partial(_in_proj_kernel, dilation=dilation, outputs=outputs, dt_piece=dt_piece, chunk=chunk),
        grid=(b, s // tm),
        in_specs=in_specs,
        out_specs=out_specs,
        out_shape=out_shape,
        scratch_shapes=[pltpu.VMEM((tm, d), BF16)],
        compiler_params=_compiler_params(("parallel", "parallel")),
        name=f"in_proj_d{dilation}",
    )(h3, norm_w, *[arr for arr, _, _ in weights])


def _attention_kernel(q_ref, kp_ref, kc_ref, vp_ref, vc_ref, bias_ref, o_ref, l_ref, kext_ref, vext_ref, *, tq):
    n = pl.program_id(2)
    kext_ref[0:ATTN_BLOCK] = kp_ref[0, 0]
    kext_ref[ATTN_BLOCK:] = kc_ref[0, 0]
    vext_ref[0:ATTN_BLOCK] = vp_ref[0, 0]
    vext_ref[ATTN_BLOCK:] = vc_ref[0, 0]
    low_half = lax.broadcasted_iota(jnp.int32, (1, LANES), 1) < HEAD_DIM

    for s in range(tq // ATTN_BLOCK):
        r0 = s * ATTN_BLOCK
        qs = q_ref[0, 0, pl.ds(r0, ATTN_BLOCK), :] * (HEAD_DIM ** -0.5)
        ks = kext_ref[pl.ds(r0, 2 * ATTN_BLOCK), :]
        vs = vext_ref[pl.ds(r0, 2 * ATTN_BLOCK), :]
        bidx = jnp.where(n == 0, 0, 1) if s == 0 else 1
        for p in range(GROUP_WIDTH // LANES):
            cols = slice(p * LANES, (p + 1) * LANES)
            qp, kp, vp = qs[:, cols], ks[:, cols], vs[:, cols]
            q2 = jnp.concatenate([jnp.where(low_half, qp, jnp.zeros_like(qp)),
                                  jnp.where(low_half, jnp.zeros_like(qp), qp)], axis=0)
            bias2 = jnp.concatenate([bias_ref[bidx, 2 * p], bias_ref[bidx, 2 * p + 1]], axis=0)
            sc = lax.dot_general(q2, kp, NT_DIMS, preferred_element_type=F32) + bias2
            m = jnp.max(sc, axis=1, keepdims=True)
            pe = jnp.exp(sc - m).astype(BF16)
            nd = jnp.dot(pe, jnp.concatenate([vp, jnp.ones_like(vp)], axis=1), preferred_element_type=F32)
            num, den = nd[:, :LANES], nd[:, LANES:]
            o2 = num / den
            l2 = m + jnp.log(den)
            outs = [o2[:ATTN_BLOCK], o2[ATTN_BLOCK:]]
            lses = [l2[:ATTN_BLOCK], l2[ATTN_BLOCK:]]
            o_ref[0, 0, pl.ds(r0, ATTN_BLOCK), cols] = jnp.where(low_half, outs[0], outs[1])
            l_ref[0, 0, pl.ds(r0, ATTN_BLOCK), cols] = jnp.where(low_half, lses[0], lses[1])


def _attention(qkv, bias_g, *, col_q, col_k, col_v, tq):
    b, dilation, seg, _ = qkv.shape
    blocks_per_tile = tq // ATTN_BLOCK

    def cur(col):
        return pl.BlockSpec((1, 1, tq, GROUP_WIDTH), lambda bi, r, n: (bi, r, n, col))

    def prev(col):
        return pl.BlockSpec((1, 1, ATTN_BLOCK, GROUP_WIDTH),
                            lambda bi, r, n: (bi, r, jnp.maximum(n * blocks_per_tile - 1, 0), col))

    out_spec = pl.BlockSpec((1, 1, tq, GROUP_WIDTH), lambda bi, r, n: (bi, r, n, 0))
    out_sds = jax.ShapeDtypeStruct((b, dilation, seg, GROUP_WIDTH), F32)
    return pl.pallas_call(
        functools.partial(_attention_kernel, tq=tq),
        grid=(b, dilation, seg // tq),
        in_specs=[cur(col_q), prev(col_k), cur(col_k), prev(col_v), cur(col_v),
                  pl.BlockSpec(bias_g.shape, lambda bi, r, n: (0, 0, 0, 0))],
        out_specs=[out_spec, out_spec],
        out_shape=[out_sds, out_sds],
        scratch_shapes=[pltpu.VMEM((tq + ATTN_BLOCK, GROUP_WIDTH), BF16),
                        pltpu.VMEM((tq + ATTN_BLOCK, GROUP_WIDTH), BF16)],
        compiler_params=_compiler_params(("parallel", "parallel", "arbitrary")),
        name=f"attention_d{dilation}",
    )(qkv, qkv, qkv, qkv, qkv, bias_g)


def _t5_causal_bucket(dist):
    max_exact = N_REL_BUCKETS // 2
    d_f = jnp.maximum(dist, 1).astype(F32)
    large = max_exact + (jnp.log(d_f / max_exact) / math.log(REL_MAX_DISTANCE / max_exact)
                         * (N_REL_BUCKETS - max_exact)).astype(jnp.int32)
    large = jnp.minimum(large, N_REL_BUCKETS - 1)
    return jnp.where(dist < max_exact, dist, large)


def _attention_bias_tables(rel_bias):
    q, q2 = ATTN_BLOCK, 2 * ATTN_BLOCK
    steps = jnp.arange(q2) - (q - 1)
    in_prev = (jnp.arange(q2) < q)[None, None, :]
    tables = []
    for g, (window, dil) in enumerate(DILATED_GROUPS):
        n_steps = window // dil
        assert n_steps <= q
        rel_g = rel_bias[:, g * HEADS_PER_GROUP:(g + 1) * HEADS_PER_GROUP].astype(F32)
        vals = rel_g[_t5_causal_bucket(jnp.clip(steps, 0, n_steps) * dil)]
        vec = jnp.where(((steps >= 0) & (steps <= n_steps))[:, None], vals, -jnp.inf).T
        skew = jnp.tile(vec, (1, q + 1))[:, :q * (q2 + 1)].reshape(HEADS_PER_GROUP, q, q2 + 1)[:, :, :q2]
        rest = skew[:, :, ::-1]
        first = jnp.where(in_prev, -jnp.inf, rest)
        tables.append(jnp.stack([first, rest]))
    return jnp.stack(tables)


LOG2E = math.log2(math.e)


def _conv_silu(slab_ref, c, w_ref, b_ref):
    q = slab_ref.shape[1] - SUBLANES
    cols = slice(c * LANES, (c + 1) * LANES)
    acc = b_ref[:, cols] + slab_ref[c, pl.ds(SUBLANES, q), :] * w_ref[CONV_WIDTH - 1:CONV_WIDTH, cols]
    for s in range(1, CONV_WIDTH):
        acc = acc + slab_ref[c, pl.ds(SUBLANES - s, q), :] * w_ref[CONV_WIDTH - 1 - s:CONV_WIDTH - s, cols]
    return acc * _sigmoid(acc)


def _ssd_decays(r0, dt_ref, dtb_ref, alog_ref):
    q = SSD_CHUNK
    x_dt = dt_ref[0, r0:r0 + q, :] + dtb_ref[...]
    dt = jnp.maximum(x_dt, 0.0) + jnp.log(1.0 + jnp.exp(-jnp.abs(x_dt)))
    d_a = dt * (-jnp.exp(alog_ref[...]))
    ri = lax.broadcasted_iota(jnp.int32, (q, q), 0)
    ci = lax.broadcasted_iota(jnp.int32, (q, q), 1)
    causal_t = ri <= ci
    tril = jnp.where(ri >= ci, 1.0, 0.0).astype(BF16)
    d1 = d_a.astype(BF16)
    r1 = d_a - d1.astype(F32)
    d2 = r1.astype(BF16)
    d3 = (r1 - d2.astype(F32)).astype(BF16)
    la = (jnp.dot(tril, d1, preferred_element_type=F32) + jnp.dot(tril, d2, preferred_element_type=F32)
          + jnp.dot(tril, d3, preferred_element_type=F32))
    la = la * LOG2E
    la_t = la.T
    dt_t = dt.T
    last = jnp.broadcast_to(la_t[:, q - 1:q], (q, q))
    w_t = jnp.exp2(last - la_t) * dt_t
    state_decay = jnp.exp2(last)
    return causal_t, la, la_t, dt_t, w_t, state_decay


def _ssd_chunk_stages(r0, decays, xs_ref, bc_ref, z_ref, cw_ref, cb_ref, dskip_ref, nw_ref,
                      ssm_ref, state_ref, slab_ref):
    causal_t, la, la_t, dt_t, w_t, state_decay = decays
    q = SSD_CHUNK
    rq = slice(r0, r0 + q)
    gw = HEADS_PER_SSM_GROUP * SSM_HEAD_DIM
    xs_slabs = xs_ref.shape[3] // LANES
    slabs_per_group = gw // LANES

    for c in range(xs_slabs):
        slab_ref[c, SUBLANES:, :] = xs_ref[0, 0, rq, c * LANES:(c + 1) * LANES].astype(F32)
    for c in range(bc_ref.shape[3] // LANES):
        slab_ref[xs_slabs + c, SUBLANES:, :] = bc_ref[0, 0, rq, c * LANES:(c + 1) * LANES].astype(F32)

    for g in range(N_SSM_GROUPS):
        bm = _conv_silu(slab_ref, xs_slabs + g, cw_ref, cb_ref)
        cm = _conv_silu(slab_ref, xs_slabs + N_SSM_GROUPS + g, cw_ref, cb_ref)
        bm16 = bm.astype(BF16)
        cb_t = lax.dot_general(bm16, cm.astype(BF16), NT_DIMS, preferred_element_type=F32)
        cm_t = cm.T
        xs_g = jnp.concatenate([_conv_silu(slab_ref, g * slabs_per_group + c, cw_ref, cb_ref)
                                for c in range(slabs_per_group)], axis=1)
        xs_t = xs_g.T
        yield
        y_t = []
        for h in range(HEADS_PER_SSM_GROUP):
            hh = g * HEADS_PER_SSM_GROUP + h
            rows = slice(hh * SSM_HEAD_DIM, (hh + 1) * SSM_HEAD_DIM)
            la_i = la_t[hh:hh + 1, :]
            la_j = jnp.broadcast_to(la[:, hh:hh + 1], (q, q))
            decay_t = jnp.exp2(jnp.where(causal_t, la_i - la_j, -jnp.inf))
            rhs = jnp.concatenate([(cb_t * decay_t).astype(BF16), (cm_t * jnp.exp2(la_i)).astype(BF16)], axis=0)
            x_h = xs_t[h * SSM_HEAD_DIM:(h + 1) * SSM_HEAD_DIM]
            state = state_ref[rows, :]
            lhs = jnp.concatenate([(x_h * dt_t[hh:hh + 1, :]).astype(BF16), state.astype(BF16)], axis=1)
            y_t.append(jnp.dot(lhs, rhs, preferred_element_type=F32))
            yield
        heads = range(g * HEADS_PER_SSM_GROUP, (g + 1) * HEADS_PER_SSM_GROUP)
        grows = slice(heads[0] * SSM_HEAD_DIM, (heads[-1] + 1) * SSM_HEAD_DIM)
        def scale_heads(x, t):
            hp = x.reshape(HEADS_PER_SSM_GROUP, SSM_HEAD_DIM, q) * t[heads[0]:heads[-1] + 1][:, None, :]
            return hp.reshape(x.shape)
        upd = jnp.dot(scale_heads(xs_t, w_t).astype(BF16), bm16, preferred_element_type=F32)
        state_ref[grows, :] = scale_heads(state_ref[grows, :], state_decay) + upd
        yield
        y = jnp.concatenate(y_t, axis=0).T
        cols = slice(g * gw, (g + 1) * gw)
        y = y + xs_g * dskip_ref[:, cols]
        z = z_ref[0, 0, rq, cols].astype(F32)
        yg = y * (z * _sigmoid(z))
        yg = yg * lax.rsqrt(jnp.mean(yg * yg, axis=-1, keepdims=True) + EPS)
        ssm_ref[rq, cols] = (yg * nw_ref[:, cols]).astype(ssm_ref.dtype)
        yield

    slab_ref[:, 0:SUBLANES, :] = slab_ref[:, q:q + SUBLANES, :]


def _merge_ffn_stages(r0, tm, attn_refs, ssm_ref, gate_ref, h_ref, wa_ref, ws_ref, wo_ref, nat_ref,
                      nw_ref, wgu_ref, wd_ref, fw_ref, out_ref, *, ffn_chunk, final_norm):
    rows = slice(r0, r0 + tm)
    n_slabs = GROUP_WIDTH // LANES
    for k, src in enumerate(attn_refs):
        dil = src.shape[1]
        for r in range(dil if dil > 1 else 0):
            for c in range(n_slabs):
                nat_ref[k, c, pl.ds(r0 + r, tm // dil, stride=dil), :] = (
                    src[0, r, r0 // dil:(r0 + tm) // dil, c * LANES:(c + 1) * LANES])

    def natural(k, c):
        src = attn_refs[k]
        return src[0, 0, rows, c * LANES:(c + 1) * LANES] if src.shape[1] == 1 else nat_ref[k, c, rows, :]

    slabs = []
    for c in range(n_slabs):
        o = [natural(2 * g, c) for g in range(N_DIL)]
        l = [natural(2 * g + 1, c) for g in range(N_DIL)]
        mx = functools.reduce(jnp.maximum, l)
        e = [jnp.exp(lg - mx) for lg in l]
        num = functools.reduce(jnp.add, [eg * og for eg, og in zip(e, o)])
        slabs.append((num / functools.reduce(jnp.add, e)).astype(BF16))
    attn = jnp.concatenate(slabs, axis=1)
    yield
    a = jnp.dot(attn, wa_ref[...], preferred_element_type=F32)
    yield
    s = jnp.dot(ssm_ref[rows, :], ws_ref[...], preferred_element_type=F32)
    d = a.shape[1]
    gates = _sigmoid(gate_ref[0, 0, rows, :].astype(F32))
    merged = (gates[:, :d] * a + gates[:, d:] * s).astype(BF16)
    yield
    acc = h_ref[0, rows, :] + jnp.dot(merged, wo_ref[...], preferred_element_type=F32)
    xn = _rmsnorm(acc, nw_ref[...]).astype(BF16)
    yield
    d_ff = wd_ref.shape[0]
    chunks = [slice(c0, c0 + ffn_chunk) for c0 in range(0, d_ff, ffn_chunk)]
    gate_up = None
    for c, cols in enumerate(chunks + [None]):
        prev_cols, prev_gate_up = (chunks[c - 1], gate_up) if c > 0 else (None, None)
        if cols is not None:
            up_cols = slice(d_ff + cols.start, d_ff + cols.stop)
            gate = jnp.dot(xn, wgu_ref[:, cols], preferred_element_type=F32)
            yield
            gate_up = (gate, jnp.dot(xn, wgu_ref[:, up_cols], preferred_element_type=F32))
            yield
        if prev_gate_up is not None:
            act = (prev_gate_up[0] * _sigmoid(prev_gate_up[0]) * prev_gate_up[1]).astype(BF16)
            acc = acc + jnp.dot(act, wd_ref[prev_cols, :], preferred_element_type=F32)
            yield
    out_ref[0, rows, :] = _rmsnorm(acc, fw_ref[...]) if final_norm else acc


N_TAIL_WEIGHTS = 13


def _tail_kernel(*refs, ffn_chunk, ssd_stages_per_matmul, final_norm):
    attn_refs = refs[:2 * N_DIL]
    gate_ref, h_ref, xs_ref, bc_ref, z_ref, dt_ref = refs[2 * N_DIL:2 * N_DIL + 6]
    weight_refs = refs[2 * N_DIL + 6:2 * N_DIL + 6 + N_TAIL_WEIGHTS]
    (wa_ref, ws_ref, wo_ref, n2_ref, wgu_ref, wd_ref, fw_ref,
     cw_ref, cb_ref, dtb_ref, alog_ref, dskip_ref, nw_ref) = [r.at[0] for r in weight_refs]
    out_ref, state_ref, slab_ref, ssm_ref, new_ref, nat_ref = refs[2 * N_DIL + 6 + N_TAIL_WEIGHTS:]

    @pl.when(pl.program_id(1) == 0)
    def _():
        state_ref[...] = jnp.zeros_like(state_ref)
        slab_ref[:, 0:SUBLANES, :] = jnp.zeros((slab_ref.shape[0], SUBLANES, LANES), F32)
        ssm_ref[...] = jnp.zeros_like(ssm_ref)

    def ssd_stages():
        for r0 in range(0, xs_ref.shape[2], SSD_CHUNK):
            decays = _ssd_decays(r0, dt_ref, dtb_ref, alog_ref)
            yield
            yield from _ssd_chunk_stages(r0, decays, xs_ref, bc_ref, z_ref, cw_ref, cb_ref, dskip_ref, nw_ref,
                                         new_ref, state_ref, slab_ref)

    matmuls = _merge_ffn_stages(0, h_ref.shape[1], attn_refs, ssm_ref, gate_ref, h_ref, wa_ref, ws_ref, wo_ref,
                                nat_ref, n2_ref, wgu_ref, wd_ref, fw_ref, out_ref,
                                ffn_chunk=ffn_chunk, final_norm=final_norm)
    turns = [(matmuls, 1), (ssd_stages(), ssd_stages_per_matmul)]
    live = [gen for gen, _ in turns]
    while live:
        for gen, reps in turns:
            for _ in range(reps if gen in live else 0):
                if next(gen, StopIteration) is StopIteration:
                    live.remove(gen)
                    break
    ssm_ref[...] = new_ref[...]


def _tail(attn_outs, gates, xs, bc, z, dt3, h3, weights, *, layer, tb, ffn_chunk, ssd_stages_per_matmul,
          final_norm):
    b, s, d = h3.shape
    assert len(weights) == N_TAIL_WEIGHTS
    conv_ch = weights[7].shape[-1]
    d_inner = xs.shape[-1]
    n_blocks = s // tb

    def cur(k):
        return jnp.minimum(k, n_blocks - 1)

    def prev(k):
        return jnp.maximum(k - 1, 0)

    def residues(arr):
        dil = arr.shape[1]
        return pl.BlockSpec((1, dil, tb // dil, GROUP_WIDTH), lambda bi, k: (bi, 0, prev(k), 0))

    def rows_of(arr, block_index):
        return pl.BlockSpec((1, 1, tb, arr.shape[-1]), lambda bi, k: (bi, 0, block_index(k), 0))

    flat = [a for pair in attn_outs for a in pair]
    in_specs = [residues(a) for a in flat] + [
        rows_of(gates, prev),
        pl.BlockSpec((1, tb, d), lambda bi, k: (bi, prev(k), 0)),
        rows_of(xs, cur), rows_of(bc, cur), rows_of(z, cur),
        pl.BlockSpec((1, tb, LANES), lambda bi, k: (bi, cur(k), 0)),
    ] + [_layer_spec(w, min(layer, w.shape[0] - 1)) for w in weights]
    return pl.pallas_call(
        functools.partial(_tail_kernel, ffn_chunk=ffn_chunk, ssd_stages_per_matmul=ssd_stages_per_matmul,
                          final_norm=final_norm),
        grid=(b, n_blocks + 1),
        in_specs=in_specs,
        out_specs=pl.BlockSpec((1, tb, d), lambda bi, k: (bi, prev(k), 0)),
        out_shape=jax.ShapeDtypeStruct((b, s, d), F32),
        scratch_shapes=[pltpu.VMEM((N_SSM_GROUPS * HEADS_PER_SSM_GROUP * SSM_HEAD_DIM, D_STATE), F32),
                        pltpu.VMEM((conv_ch // LANES, SUBLANES + SSD_CHUNK, LANES), F32),
                        pltpu.VMEM((tb, d_inner), BF16),
                        pltpu.VMEM((tb, d_inner), BF16),
                        pltpu.VMEM((len(flat), GROUP_WIDTH // LANES, tb, LANES), F32)],
        compiler_params=_compiler_params(("parallel", "arbitrary")),
        name="layer_tail",
    )(*flat, gates, h3, xs, bc, z, dt3, *weights)


def kernel(x, norm1_w, w_in, conv_w, conv_b, dt_bias, a_log, d_skip, ssm_norm_w, w_attn_branch, w_ssm_branch,
           w_out, norm2_w, w_ffn_in, w_ffn_out, rel_bias, final_norm_w):
    b, s, d = x.shape
    depth = w_in.shape[0]
    d_inner = ssm_norm_w.shape[1]
    n_heads = dt_bias.shape[1]
    bc_w = 2 * N_SSM_GROUPS * D_STATE
    assert d_inner == N_SSM_GROUPS * HEADS_PER_SSM_GROUP * SSM_HEAD_DIM and n_heads <= LANES
    assert s % (DILATED_GROUPS[-1][1] * ATTN_BLOCK) == 0 and s % SSD_CHUNK == 0

    o_z = 3 * ATTN_WIDTH
    o_xbc = o_z + d_inner
    o_dt = o_xbc + d_inner + bc_w
    o_gate = o_dt + n_heads
    assert o_dt % LANES == 0 and o_dt + LANES <= w_in.shape[2]
    w_in16 = w_in.astype(BF16)
    w_gates16 = w_in[:, :, o_gate:].astype(BF16)
    qkv_start = lambda part, g: part * ATTN_WIDTH + g * GROUP_WIDTH
    main_weights = [(w_in16, None, 0), (w_gates16, None, 0)]
    main_outputs = (((0, o_z, d_inner),), ((1, 0, 2 * d),), ((0, o_xbc, d_inner),), ((0, o_xbc + d_inner, bc_w),),
                    tuple((0, qkv_start(part, 0), GROUP_WIDTH) for part in range(3)))

    pad_h = ((0, 0), (0, 0), (0, LANES - n_heads))
    row3 = lambda a: a[:, None, :]
    tail_weights = (w_attn_branch.astype(BF16), w_ssm_branch.astype(BF16), w_out.astype(BF16), row3(norm2_w),
                    w_ffn_in.astype(BF16), w_ffn_out.astype(BF16), final_norm_w[None, None, :],
                    conv_w, row3(conv_b), jnp.pad(row3(dt_bias), pad_h), jnp.pad(row3(a_log), pad_h),
                    row3(jnp.repeat(d_skip, SSM_HEAD_DIM, axis=1)), row3(ssm_norm_w))
    norm1_w3 = row3(norm1_w)
    bias_tables = _attention_bias_tables(rel_bias)

    h = x
    for layer in range(depth):
        z, gates, xs, bc, qkv0, dt = _in_proj(h, norm1_w3, main_weights, main_outputs, (0, o_dt, LANES),
                                              layer=layer, dilation=1, tm=IN_PROJ_ROWS, chunk=IN_PROJ_CHUNK)
        attn_outs = []
        for g, (_, dil) in enumerate(DILATED_GROUPS):
            if dil == 1:
                qkv = qkv0
            else:
                weights = [(w_in16, GROUP_WIDTH, qkv_start(part, g) // GROUP_WIDTH) for part in range(3)]
                (qkv,) = _in_proj(h, norm1_w3, weights, (tuple((part, 0, GROUP_WIDTH) for part in range(3)),),
                                  layer=layer, dilation=dil, tm=IN_PROJ_DILATED_ROWS, chunk=IN_PROJ_CHUNK)
            attn_outs.append(_attention(qkv, bias_tables[g], col_q=0, col_k=1, col_v=2,
                                        tq=min(ATTN_ROWS, s // dil)))
        h = _tail(attn_outs, gates, xs, bc, z, dt, h, tail_weights, layer=layer, tb=TAIL_ROWS,
                  ffn_chunk=FFN_CHUNK, ssd_stages_per_matmul=SSD_STAGES_PER_MATMUL,
                  final_norm=(layer == depth - 1))
    return h
```

```python
import functools
import math

import jax
import jax.numpy as jnp
from jax import lax
from jax.experimental import pallas as pl
from jax.experimental.pallas import tpu as pltpu

HEAD_DIM = 64
DILATED_GROUPS = ((128, 1), (512, 4), (2048, 16))
N_DIL = len(DILATED_GROUPS)
HEADS_PER_GROUP = 8
GROUP_WIDTH = HEADS_PER_GROUP * HEAD_DIM
ATTN_WIDTH = N_DIL * GROUP_WIDTH
ATTN_BLOCK = 128
N_REL_BUCKETS = 32
REL_MAX_DISTANCE = 2048
SSM_HEAD_DIM = 64
N_SSM_GROUPS = 4
HEADS_PER_SSM_GROUP = 8
D_STATE = 128
CONV_WIDTH = 4
SSD_CHUNK = 128
EPS = 1e-6

LANES = 128
SUBLANES = 8
VMEM_LIMIT_BYTES = 60000 * 1024

IN_PROJ_ROWS = 512
IN_PROJ_DILATED_ROWS = 1024
IN_PROJ_CHUNK = 1024
ATTN_ROWS = 2048
TAIL_ROWS = 256
FFN_CHUNK = 256
SSD_STAGES_PER_MATMUL = 2

BF16 = jnp.bfloat16
F32 = jnp.float32
NT_DIMS = (((1,), (1,)), ((), ()))


def _compiler_params(semantics):
    return pltpu.CompilerParams(dimension_semantics=semantics, vmem_limit_bytes=VMEM_LIMIT_BYTES)


def _rmsnorm(x, w):
    return x * lax.rsqrt(jnp.mean(x * x, axis=-1, keepdims=True) + EPS) * w


def _sigmoid(x):
    return 1.0 / (1.0 + jnp.exp(-x))


PERM_BLOCK = 256


def _in_proj_kernel(x_ref, nw_ref, *rest, dilation, outputs, dt_piece, chunk):
    n_w = 1 + max(p[0] for pieces in outputs for p in pieces)
    w_refs = [r.at[0] for r in rest[:n_w]]
    out_refs = rest[n_w:n_w + len(outputs)]
    if dt_piece is not None:
        dt_ref, xn_ref = rest[n_w + len(outputs):]
    else:
        (xn_ref,) = rest[n_w + len(outputs):]
    nw_ref = nw_ref.at[0]
    tm = x_ref.shape[1]
    xn = _rmsnorm(x_ref[0], nw_ref[...]).astype(BF16)
    if dt_piece is not None:
        src, c0, width = dt_piece
        dt_ref[0] = jnp.dot(xn, w_refs[src][:, c0:c0 + width], preferred_element_type=F32)
    if dilation == 1:
        xn_ref[...] = xn
    else:
        per = PERM_BLOCK // dilation
        i = lax.broadcasted_iota(jnp.int32, (PERM_BLOCK, PERM_BLOCK), 0)
        k = lax.broadcasted_iota(jnp.int32, (PERM_BLOCK, PERM_BLOCK), 1)
        src = (i & (per - 1)) * dilation + (i >> (per.bit_length() - 1))
        perm = jnp.where(k == src, 1.0, 0.0).astype(BF16)
        for u in range(tm // PERM_BLOCK):
            y = jnp.dot(perm, xn[u * PERM_BLOCK:(u + 1) * PERM_BLOCK], preferred_element_type=F32).astype(BF16)
            for r in range(dilation):
                dst = r * (tm // dilation) + u * per
                xn_ref[dst:dst + per, :] = y[r * per:(r + 1) * per]
    for out_ref, pieces in zip(out_refs, outputs):
        col = 0
        for src, start, width in pieces:
            for c0 in range(0, width, chunk):
                cw = min(chunk, width - c0)
                res = jnp.dot(xn_ref[...], w_refs[src][:, start + c0:start + c0 + cw], preferred_element_type=F32)
                out_ref[0, :, :, col:col + cw] = res.astype(out_ref.dtype).reshape(dilation, tm // dilation, cw)
                col += cw


def _layer_spec(arr, layer, block=None, col_block=0):
    shape = (1,) + tuple(arr.shape[1:-1]) + (block or arr.shape[-1],)
    index = (layer,) + (0,) * (arr.ndim - 2) + (col_block,)
    return pl.BlockSpec(shape, lambda *_: index, pipeline_mode=pl.Buffered(1))


def _in_proj(h3, norm_w, weights, outputs, dt_piece=None, *, layer, dilation, tm, chunk):
    b, s, d = h3.shape
    in_specs = [pl.BlockSpec((1, tm, d), lambda bi, i: (bi, i, 0)), _layer_spec(norm_w, layer)]
    in_specs += [_layer_spec(arr, layer, blk, cb) for arr, blk, cb in weights]
    widths = [sum(p[2] for p in pieces) for pieces in outputs]
    out_specs = [pl.BlockSpec((1, dilation, tm // dilation, n), lambda bi, i: (bi, 0, i, 0)) for n in widths]
    out_shape = [jax.ShapeDtypeStruct((b, dilation, s // dilation, n), BF16) for n in widths]
    if dt_piece is not None:
        out_specs.append(pl.BlockSpec((1, tm, LANES), lambda bi, i: (bi, i, 0)))
        out_shape.append(jax.ShapeDtypeStruct((b, s, LANES), F32))
    return pl.pallas_call(
        functools.partial(_in_proj_kernel, dilation=dilation, outputs=outputs, dt_piece=dt_piece, chunk=chunk),
        grid=(b, s // tm),
        in_specs=in_specs,
        out_specs=out_specs,
        out_shape=out_shape,
        scratch_shapes=[pltpu.VMEM((tm, d), BF16)],
        compiler_params=_compiler_params(("parallel", "parallel")),
        name=f"in_proj_d{dilation}",
    )(h3, norm_w, *[arr for arr, _, _ in weights])


def _attention_kernel(q_ref, kp_ref, kc_ref, vp_ref, vc_ref, bias_ref, o_ref, l_ref, kext_ref, vext_ref, *, tq):
    n = pl.program_id(2)
    kext_ref[0:ATTN_BLOCK] = kp_ref[0, 0]
    kext_ref[ATTN_BLOCK:] = kc_ref[0, 0]
    vext_ref[0:ATTN_BLOCK] = vp_ref[0, 0]
    vext_ref[ATTN_BLOCK:] = vc_ref[0, 0]
    low_half = lax.broadcasted_iota(jnp.int32, (1, LANES), 1) < HEAD_DIM

    for s in range(tq // ATTN_BLOCK):
        r0 = s * ATTN_BLOCK
        qs = q_ref[0, 0, pl.ds(r0, ATTN_BLOCK), :] * (HEAD_DIM ** -0.5)
        ks = kext_ref[pl.ds(r0, 2 * ATTN_BLOCK), :]
        vs = vext_ref[pl.ds(r0, 2 * ATTN_BLOCK), :]
        bidx = jnp.where(n == 0, 0, 1) if s == 0 else 1
        for p in range(GROUP_WIDTH // LANES):
            cols = slice(p * LANES, (p + 1) * LANES)
            qp, kp, vp = qs[:, cols], ks[:, cols], vs[:, cols]
            q2 = jnp.concatenate([jnp.where(low_half, qp, jnp.zeros_like(qp)),
                                  jnp.where(low_half, jnp.zeros_like(qp), qp)], axis=0)
            bias2 = jnp.concatenate([bias_ref[bidx, 2 * p], bias_ref[bidx, 2 * p + 1]], axis=0)
            sc = lax.dot_general(q2, kp, NT_DIMS, preferred_element_type=F32) + bias2
            m = jnp.max(sc, axis=1, keepdims=True)
            pe = jnp.exp(sc - m).astype(BF16)
            nd = jnp.dot(pe, jnp.concatenate([vp, jnp.ones_like(vp)], axis=1), preferred_element_type=F32)
            num, den = nd[:, :LANES], nd[:, LANES:]
            o2 = num / den
            l2 = m + jnp.log(den)
            outs = [o2[:ATTN_BLOCK], o2[ATTN_BLOCK:]]
            lses = [l2[:ATTN_BLOCK], l2[ATTN_BLOCK:]]
            o_ref[0, 0, pl.ds(r0, ATTN_BLOCK), cols] = jnp.where(low_half, outs[0], outs[1])
            l_ref[0, 0, pl.ds(r0, ATTN_BLOCK), cols] = jnp.where(low_half, lses[0], lses[1])


def _attention(qkv, bias_g, *, col_q, col_k, col_v, tq):
    b, dilation, seg, _ = qkv.shape
    blocks_per_tile = tq // ATTN_BLOCK

    def cur(col):
        return pl.BlockSpec((1, 1, tq, GROUP_WIDTH), lambda bi, r, n: (bi, r, n, col))

    def prev(col):
        return pl.BlockSpec((1, 1, ATTN_BLOCK, GROUP_WIDTH),
                            lambda bi, r, n: (bi, r, jnp.maximum(n * blocks_per_tile - 1, 0), col))

    out_spec = pl.BlockSpec((1, 1, tq, GROUP_WIDTH), lambda bi, r, n: (bi, r, n, 0))
    out_sds = jax.ShapeDtypeStruct((b, dilation, seg, GROUP_WIDTH), F32)
    return pl.pallas_call(
        functools.partial(_attention_kernel, tq=tq),
        grid=(b, dilation, seg // tq),
        in_specs=[cur(col_q), prev(col_k), cur(col_k), prev(col_v), cur(col_v),
                  pl.BlockSpec(bias_g.shape, lambda bi, r, n: (0, 0, 0, 0))],
        out_specs=[out_spec, out_spec],
        out_shape=[out_sds, out_sds],
        scratch_shapes=[pltpu.VMEM((tq + ATTN_BLOCK, GROUP_WIDTH), BF16),
                        pltpu.VMEM((tq + ATTN_BLOCK, GROUP_WIDTH), BF16)],
        compiler_params=_compiler_params(("parallel", "parallel", "arbitrary")),
        name=f"attention_d{dilation}",
    )(qkv, qkv, qkv, qkv, qkv, bias_g)


def _t5_causal_bucket(dist):
    max_exact = N_REL_BUCKETS // 2
    d_f = jnp.maximum(dist, 1).astype(F32)
    large = max_exact + (jnp.log(d_f / max_exact) / math.log(REL_MAX_DISTANCE / max_exact)
                         * (N_REL_BUCKETS - max_exact)).astype(jnp.int32)
    large = jnp.minimum(large, N_REL_BUCKETS - 1)
    return jnp.where(dist < max_exact, dist, large)


def _attention_bias_tables(rel_bias):
    q, q2 = ATTN_BLOCK, 2 * ATTN_BLOCK
    steps = jnp.arange(q2) - (q - 1)
    in_prev = (jnp.arange(q2) < q)[None, None, :]
    tables = []
    for g, (window, dil) in enumerate(DILATED_GROUPS):
        n_steps = window // dil
        assert n_steps <= q
        rel_g = rel_bias[:, g * HEADS_PER_GROUP:(g + 1) * HEADS_PER_GROUP].astype(F32)
        vals = rel_g[_t5_causal_bucket(jnp.clip(steps, 0, n_steps) * dil)]
        vec = jnp.where(((steps >= 0) & (steps <= n_steps))[:, None], vals, -jnp.inf).T
        skew = jnp.tile(vec, (1, q + 1))[:, :q * (q2 + 1)].reshape(HEADS_PER_GROUP, q, q2 + 1)[:, :, :q2]
        rest = skew[:, :, ::-1]
        first = jnp.where(in_prev, -jnp.inf, rest)
        tables.append(jnp.stack([first, rest]))
    return jnp.stack(tables)


LOG2E = math.log2(math.e)


def _conv_silu(slab_ref, c, w_ref, b_ref):
    q = slab_ref.shape[1] - SUBLANES
    cols = slice(c * LANES, (c + 1) * LANES)
    acc = b_ref[:, cols] + slab_ref[c, pl.ds(SUBLANES, q), :] * w_ref[CONV_WIDTH - 1:CONV_WIDTH, cols]
    for s in range(1, CONV_WIDTH):
        acc = acc + slab_ref[c, pl.ds(SUBLANES - s, q), :] * w_ref[CONV_WIDTH - 1 - s:CONV_WIDTH - s, cols]
    return acc * _sigmoid(acc)


def _ssd_decays(r0, dt_ref, dtb_ref, alog_ref):
    q = SSD_CHUNK
    x_dt = dt_ref[0, r0:r0 + q, :] + dtb_ref[...]
    dt = jnp.maximum(x_dt, 0.0) + jnp.log(1.0 + jnp.exp(-jnp.abs(x_dt)))
    d_a = dt * (-jnp.exp(alog_ref[...]))
    ri = lax.broadcasted_iota(jnp.int32, (q, q), 0)
    ci = lax.broadcasted_iota(jnp.int32, (q, q), 1)
    causal_t = ri <= ci
    tril = jnp.where(ri >= ci, 1.0, 0.0).astype(BF16)
    d1 = d_a.astype(BF16)
    r1 = d_a - d1.astype(F32)
    d2 = r1.astype(BF16)
    d3 = (r1 - d2.astype(F32)).astype(BF16)
    la = (jnp.dot(tril, d1, preferred_element_type=F32) + jnp.dot(tril, d2, preferred_element_type=F32)
          + jnp.dot(tril, d3, preferred_element_type=F32))
    la = la * LOG2E
    la_t = la.T
    dt_t = dt.T
    last = jnp.broadcast_to(la_t[:, q - 1:q], (q, q))
    w_t = jnp.exp2(last - la_t) * dt_t
    state_decay = jnp.exp2(last)
    return causal_t, la, la_t, dt_t, w_t, state_decay


def _ssd_chunk_stages(r0, decays, xs_ref, bc_ref, z_ref, cw_ref, cb_ref, dskip_ref, nw_ref,
                      ssm_ref, state_ref, slab_ref):
    causal_t, la, la_t, dt_t, w_t, state_decay = decays
    q = SSD_CHUNK
    rq = slice(r0, r0 + q)
    gw = HEADS_PER_SSM_GROUP * SSM_HEAD_DIM
    xs_slabs = xs_ref.shape[3] // LANES
    slabs_per_group = gw // LANES

    for c in range(xs_slabs):
        slab_ref[c, SUBLANES:, :] = xs_ref[0, 0, rq, c * LANES:(c + 1) * LANES].astype(F32)
    for c in range(bc_ref.shape[3] // LANES):
        slab_ref[xs_slabs + c, SUBLANES:, :] = bc_ref[0, 0, rq, c * LANES:(c + 1) * LANES].astype(F32)

    for g in range(N_SSM_GROUPS):
        bm = _conv_silu(slab_ref, xs_slabs + g, cw_ref, cb_ref)
        cm = _conv_silu(slab_ref, xs_slabs + N_SSM_GROUPS + g, cw_ref, cb_ref)
        bm16 = bm.astype(BF16)
        cb_t = lax.dot_general(bm16, cm.astype(BF16), NT_DIMS, preferred_element_type=F32)
        cm_t = cm.T
        xs_g = jnp.concatenate([_conv_silu(slab_ref, g * slabs_per_group + c, cw_ref, cb_ref)
                                for c in range(slabs_per_group)], axis=1)
        xs_t = xs_g.T
        yield
        y_t = []
        for h in range(HEADS_PER_SSM_GROUP):
            hh = g * HEADS_PER_SSM_GROUP + h
            rows = slice(hh * SSM_HEAD_DIM, (hh + 1) * SSM_HEAD_DIM)
            la_i = la_t[hh:hh + 1, :]
            la_j = jnp.broadcast_to(la[:, hh:hh + 1], (q, q))
            decay_t = jnp.exp2(jnp.where(causal_t, la_i - la_j, -jnp.inf))
            rhs = jnp.concatenate([(cb_t * decay_t).astype(BF16), (cm_t * jnp.exp2(la_i)).astype(BF16)], axis=0)
            x_h = xs_t[h * SSM_HEAD_DIM:(h + 1) * SSM_HEAD_DIM]
            state = state_ref[rows, :]
            lhs = jnp.concatenate([(x_h * dt_t[hh:hh + 1, :]).astype(BF16), state.astype(BF16)], axis=1)
            y_t.append(jnp.dot(lhs, rhs, preferred_element_type=F32))
            yield
        heads = range(g * HEADS_PER_SSM_GROUP, (g + 1) * HEADS_PER_SSM_GROUP)
        grows = slice(heads[0] * SSM_HEAD_DIM, (heads[-1] + 1) * SSM_HEAD_DIM)
        def scale_heads(x, t):
            hp = x.reshape(HEADS_PER_SSM_GROUP, SSM_HEAD_DIM, q) * t[heads[0]:heads[-1] + 1][:, None, :]
            return hp.reshape(x.shape)
        upd = jnp.dot(scale_heads(xs_t, w_t).astype(BF16), bm16, preferred_element_type=F32)
        state_ref[grows, :] = scale_heads(state_ref[grows, :], state_decay) + upd
        yield
        y = jnp.concatenate(y_t, axis=0).T
        cols = slice(g * gw, (g + 1) * gw)
        y = y + xs_g * dskip_ref[:, cols]
        z = z_ref[0, 0, rq, cols].astype(F32)
        yg = y * (z * _sigmoid(z))
        yg = yg * lax.rsqrt(jnp.mean(yg * yg, axis=-1, keepdims=True) + EPS)
        ssm_ref[rq, cols] = (yg * nw_ref[:, cols]).astype(ssm_ref.dtype)
        yield

    slab_ref[:, 0:SUBLANES, :] = slab_ref[:, q:q + SUBLANES, :]


def _merge_ffn_stages(attn_refs, ssm_ref, gate_ref, h_ref, wa_ref, ws_ref, wo_ref, nat_ref,
                      nw_ref, wgu_ref, wd_ref, fw_ref, out_ref, *, ffn_chunk, final_norm):
    tb = h_ref.shape[1]
    n_slabs = GROUP_WIDTH // LANES
    for k, src in enumerate(attn_refs):
        dil = src.shape[1]
        for r in range(dil if dil > 1 else 0):
            for c in range(n_slabs):
                nat_ref[k, c, pl.ds(r, tb // dil, stride=dil), :] = src[0, r, :, c * LANES:(c + 1) * LANES]

    def natural(k, c):
        src = attn_refs[k]
        return src[0, 0, :, c * LANES:(c + 1) * LANES] if src.shape[1] == 1 else nat_ref[k, c]

    slabs = []
    for c in range(n_slabs):
        o = [natural(2 * g, c) for g in range(N_DIL)]
        l = [natural(2 * g + 1, c) for g in range(N_DIL)]
        mx = functools.reduce(jnp.maximum, l)
        e = [jnp.exp(lg - mx) for lg in l]
        num = functools.reduce(jnp.add, [eg * og for eg, og in zip(e, o)])
        slabs.append((num / functools.reduce(jnp.add, e)).astype(BF16))
    attn = jnp.concatenate(slabs, axis=1)
    yield
    a = jnp.dot(attn, wa_ref[...], preferred_element_type=F32)
    yield
    s = jnp.dot(ssm_ref[...], ws_ref[...], preferred_element_type=F32)
    d = a.shape[1]
    gates = _sigmoid(gate_ref[0, 0].astype(F32))
    merged = (gates[:, :d] * a + gates[:, d:] * s).astype(BF16)
    yield
    acc = h_ref[0] + jnp.dot(merged, wo_ref[...], preferred_element_type=F32)
    xn = _rmsnorm(acc, nw_ref[...]).astype(BF16)
    yield
    d_ff = wd_ref.shape[0]
    chunks = [slice(c0, c0 + ffn_chunk) for c0 in range(0, d_ff, ffn_chunk)]
    gate_up = None
    for c, cols in enumerate(chunks + [None]):
        prev_cols, prev_gate_up = (chunks[c - 1], gate_up) if c > 0 else (None, None)
        if cols is not None:
            up_cols = slice(d_ff + cols.start, d_ff + cols.stop)
            gate = jnp.dot(xn, wgu_ref[:, cols], preferred_element_type=F32)
            yield
            gate_up = (gate, jnp.dot(xn, wgu_ref[:, up_cols], preferred_element_type=F32))
            yield
        if prev_gate_up is not None:
            act = (prev_gate_up[0] * _sigmoid(prev_gate_up[0]) * prev_gate_up[1]).astype(BF16)
            acc = acc + jnp.dot(act, wd_ref[prev_cols, :], preferred_element_type=F32)
            yield
    out_ref[0] = _rmsnorm(acc, fw_ref[...]) if final_norm else acc


N_TAIL_WEIGHTS = 13


def _tail_kernel(*refs, ffn_chunk, ssd_stages_per_matmul, final_norm):
    attn_refs = refs[:2 * N_DIL]
    gate_ref, h_ref, xs_ref, bc_ref, z_ref, dt_ref = refs[2 * N_DIL:2 * N_DIL + 6]
    weight_refs = refs[2 * N_DIL + 6:2 * N_DIL + 6 + N_TAIL_WEIGHTS]
    (wa_ref, ws_ref, wo_ref, n2_ref, wgu_ref, wd_ref, fw_ref,
     cw_ref, cb_ref, dtb_ref, alog_ref, dskip_ref, nw_ref) = [r.at[0] for r in weight_refs]
    out_ref, state_ref, slab_ref, ssm_ref, new_ref, nat_ref = refs[2 * N_DIL + 6 + N_TAIL_WEIGHTS:]

    @pl.when(pl.program_id(1) == 0)
    def _():
        state_ref[...] = jnp.zeros_like(state_ref)
        slab_ref[:, 0:SUBLANES, :] = jnp.zeros((slab_ref.shape[0], SUBLANES, LANES), F32)
        ssm_ref[...] = jnp.zeros_like(ssm_ref)

    def ssd_stages():
        for r0 in range(0, xs_ref.shape[2], SSD_CHUNK):
            decays = _ssd_decays(r0, dt_ref, dtb_ref, alog_ref)
            yield
            yield from _ssd_chunk_stages(r0, decays, xs_ref, bc_ref, z_ref, cw_ref, cb_ref, dskip_ref, nw_ref,
                                         new_ref, state_ref, slab_ref)

    matmuls = _merge_ffn_stages(attn_refs, ssm_ref, gate_ref, h_ref, wa_ref, ws_ref, wo_ref, nat_ref,
                                n2_ref, wgu_ref, wd_ref, fw_ref, out_ref, ffn_chunk=ffn_chunk, final_norm=final_norm)
    turns = [(matmuls, 1), (ssd_stages(), ssd_stages_per_matmul)]
    live = [gen for gen, _ in turns]
    while live:
        for gen, reps in turns:
            for _ in range(reps if gen in live else 0):
                if next(gen, StopIteration) is StopIteration:
                    live.remove(gen)
                    break
    ssm_ref[...] = new_ref[...]


def _tail(attn_outs, gates, xs, bc, z, dt3, h3, weights, *, layer, tb, ffn_chunk, ssd_stages_per_matmul,
          final_norm):
    b, s, d = h3.shape
    assert len(weights) == N_TAIL_WEIGHTS
    conv_ch = weights[7].shape[-1]
    d_inner = xs.shape[-1]
    n_blocks = s // tb

    def cur(k):
        return jnp.minimum(k, n_blocks - 1)

    def prev(k):
        return jnp.maximum(k - 1, 0)

    def residues(arr):
        dil = arr.shape[1]
        return pl.BlockSpec((1, dil, tb // dil, GROUP_WIDTH), lambda bi, k: (bi, 0, prev(k), 0))

    def rows_of(arr, block_index):
        return pl.BlockSpec((1, 1, tb, arr.shape[-1]), lambda bi, k: (bi, 0, block_index(k), 0))

    flat = [a for pair in attn_outs for a in pair]
    in_specs = [residues(a) for a in flat] + [
        rows_of(gates, prev),
        pl.BlockSpec((1, tb, d), lambda bi, k: (bi, prev(k), 0)),
        rows_of(xs, cur), rows_of(bc, cur), rows_of(z, cur),
        pl.BlockSpec((1, tb, LANES), lambda bi, k: (bi, cur(k), 0)),
    ] + [_layer_spec(w, min(layer, w.shape[0] - 1)) for w in weights]
    return pl.pallas_call(
        functools.partial(_tail_kernel, ffn_chunk=ffn_chunk, ssd_stages_per_matmul=ssd_stages_per_matmul,
                          final_norm=final_norm),
        grid=(b, n_blocks + 1),
        in_specs=in_specs,
        out_specs=pl.BlockSpec((1, tb, d), lambda bi, k: (bi, prev(k), 0)),
        out_shape=jax.ShapeDtypeStruct((b, s, d), F32),
        scratch_shapes=[pltpu.VMEM((N_SSM_GROUPS * HEADS_PER_SSM_GROUP * SSM_HEAD_DIM, D_STATE), F32),
                        pltpu.VMEM((conv_ch // LANES, SUBLANES + SSD_CHUNK, LANES), F32),
                        pltpu.VMEM((tb, d_inner), BF16),
                        pltpu.VMEM((tb, d_inner), BF16),
                        pltpu.VMEM((len(flat), GROUP_WIDTH // LANES, tb, LANES), F32)],
        compiler_params=_compiler_params(("parallel", "arbitrary")),
        name="layer_tail",
    )(*flat, gates, h3, xs, bc, z, dt3, *weights)


def kernel(x, norm1_w, w_in, conv_w, conv_b, dt_bias, a_log, d_skip, ssm_norm_w, w_attn_branch, w_ssm_branch,
           w_out, norm2_w, w_ffn_in, w_ffn_out, rel_bias, final_norm_w):
    b, s, d = x.shape
    depth = w_in.shape[0]
    d_inner = ssm_norm_w.shape[1]
    n_heads = dt_bias.shape[1]
    bc_w = 2 * N_SSM_GROUPS * D_STATE
    assert d_inner == N_SSM_GROUPS * HEADS_PER_SSM_GROUP * SSM_HEAD_DIM and n_heads <= LANES
    assert s % (DILATED_GROUPS[-1][1] * ATTN_BLOCK) == 0 and s % SSD_CHUNK == 0

    o_z = 3 * ATTN_WIDTH
    o_xbc = o_z + d_inner
    o_dt = o_xbc + d_inner + bc_w
    o_gate = o_dt + n_heads
    assert o_dt % LANES == 0 and o_dt + LANES <= w_in.shape[2]
    w_in16 = w_in.astype(BF16)
    w_gates16 = w_in[:, :, o_gate:].astype(BF16)
    qkv_start = lambda part, g: part * ATTN_WIDTH + g * GROUP_WIDTH
    main_weights = [(w_in16, None, 0), (w_gates16, None, 0)]
    main_outputs = (((0, o_z, d_inner),), ((1, 0, 2 * d),), ((0, o_xbc, d_inner),), ((0, o_xbc + d_inner, bc_w),),
                    tuple((0, qkv_start(part, 0), GROUP_WIDTH) for part in range(3)))

    pad_h = ((0, 0), (0, 0), (0, LANES - n_heads))
    row3 = lambda a: a[:, None, :]
    tail_weights = (w_attn_branch.astype(BF16), w_ssm_branch.astype(BF16), w_out.astype(BF16), row3(norm2_w),
                    w_ffn_in.astype(BF16), w_ffn_out.astype(BF16), final_norm_w[None, None, :],
                    conv_w, row3(conv_b), jnp.pad(row3(dt_bias), pad_h), jnp.pad(row3(a_log), pad_h),
                    row3(jnp.repeat(d_skip, SSM_HEAD_DIM, axis=1)), row3(ssm_norm_w))
    norm1_w3 = row3(norm1_w)
    bias_tables = _attention_bias_tables(rel_bias)

    h = x
    for layer in range(depth):
        z, gates, xs, bc, qkv0, dt = _in_proj(h, norm1_w3, main_weights, main_outputs, (0, o_dt, LANES),
                                              layer=layer, dilation=1, tm=IN_PROJ_ROWS, chunk=IN_PROJ_CHUNK)
        attn_outs = []
        for g, (_, dil) in enumerate(DILATED_GROUPS):
            if dil == 1:
                qkv = qkv0
            else:
                weights = [(w_in16, GROUP_WIDTH, qkv_start(part, g) // GROUP_WIDTH) for part in range(3)]
                (qkv,) = _in_proj(h, norm1_w3, weights, (tuple((part, 0, GROUP_WIDTH) for part in range(3)),),
                                  layer=layer, dilation=dil, tm=IN_PROJ_DILATED_ROWS, chunk=IN_PROJ_CHUNK)
            attn_outs.append(_attention(qkv, bias_tables[g], col_q=0, col_k=1, col_v=2,
                                        tq=min(ATTN_ROWS, s // dil)))
        h = _tail(attn_outs, gates, xs, bc, z, dt, h, tail_weights, layer=layer, tb=TAIL_ROWS,
                  ffn_chunk=FFN_CHUNK, ssd_stages_per_matmul=SSD_STAGES_PER_MATMUL,
                  final_norm=(layer == depth - 1))
    return h
```

```python
import functools
import math

import jax
import jax.numpy as jnp
from jax import lax
from jax.experimental import pallas as pl
from jax.experimental.pallas import tpu as pltpu

HEAD_DIM = 64
DILATED_GROUPS = ((128, 1), (512, 4), (2048, 16))
N_DIL = len(DILATED_GROUPS)
HEADS_PER_GROUP = 8
GROUP_WIDTH = HEADS_PER_GROUP * HEAD_DIM
ATTN_WIDTH = N_DIL * GROUP_WIDTH
ATTN_BLOCK = 128
N_REL_BUCKETS = 32
REL_MAX_DISTANCE = 2048
SSM_HEAD_DIM = 64
N_SSM_GROUPS = 4
HEADS_PER_SSM_GROUP = 8
D_STATE = 128
CONV_WIDTH = 4
SSD_CHUNK = 128
EPS = 1e-6

LANES = 128
SUBLANES = 8
VMEM_LIMIT_BYTES = 60000 * 1024

IN_PROJ_ROWS = 512
IN_PROJ_DILATED_ROWS = 1024
IN_PROJ_CHUNK = 1024
ATTN_ROWS = 2048
TAIL_ROWS = 256
FFN_CHUNK = 256
SSD_STAGES_PER_MATMUL = 2

BF16 = jnp.bfloat16
F32 = jnp.float32
NT_DIMS = (((1,), (1,)), ((), ()))


def _compiler_params(semantics):
    return pltpu.CompilerParams(dimension_semantics=semantics, vmem_limit_bytes=VMEM_LIMIT_BYTES)


def _rmsnorm(x, w):
    return x * lax.rsqrt(jnp.mean(x * x, axis=-1, keepdims=True) + EPS) * w


def _sigmoid(x):
    return 1.0 / (1.0 + jnp.exp(-x))


PERM_BLOCK = 256


def _in_proj_kernel(x_ref, nw_ref, *rest, outputs, dt_piece, chunk):
    n_w = 1 + max(p[0] for _, pieces in outputs for p in pieces)
    w_refs = [r.at[0] for r in rest[:n_w]]
    out_refs = rest[n_w:n_w + len(outputs)]
    rest = rest[n_w + len(outputs):]
    if dt_piece is not None:
        dt_ref, rest = rest[0], rest[1:]
    dilations = sorted({dil for dil, _ in outputs})
    xn_refs = dict(zip(dilations, rest))
    nw_ref = nw_ref.at[0]
    tm = x_ref.shape[1]
    xn = _rmsnorm(x_ref[0], nw_ref[...]).astype(BF16)
    if dt_piece is not None:
        src, c0, width = dt_piece
        dt_ref[0] = jnp.dot(xn, w_refs[src][:, c0:c0 + width], preferred_element_type=F32)
    for dilation, xn_ref in xn_refs.items():
        if dilation == 1:
            xn_ref[...] = xn
            continue
        per = PERM_BLOCK // dilation
        i = lax.broadcasted_iota(jnp.int32, (PERM_BLOCK, PERM_BLOCK), 0)
        k = lax.broadcasted_iota(jnp.int32, (PERM_BLOCK, PERM_BLOCK), 1)
        src = (i & (per - 1)) * dilation + (i >> (per.bit_length() - 1))
        perm = jnp.where(k == src, 1.0, 0.0).astype(BF16)
        for u in range(tm // PERM_BLOCK):
            y = jnp.dot(perm, xn[u * PERM_BLOCK:(u + 1) * PERM_BLOCK], preferred_element_type=F32).astype(BF16)
            for r in range(dilation):
                dst = r * (tm // dilation) + u * per
                xn_ref[dst:dst + per, :] = y[r * per:(r + 1) * per]
    for out_ref, (dilation, pieces) in zip(out_refs, outputs):
        col = 0
        for src, start, width in pieces:
            for c0 in range(0, width, chunk):
                cw = min(chunk, width - c0)
                res = jnp.dot(xn_refs[dilation][...], w_refs[src][:, start + c0:start + c0 + cw],
                              preferred_element_type=F32)
                out_ref[0, :, :, col:col + cw] = res.astype(out_ref.dtype).reshape(dilation, tm // dilation, cw)
                col += cw


def _layer_spec(arr, layer, block=None, col_block=0):
    shape = (1,) + tuple(arr.shape[1:-1]) + (block or arr.shape[-1],)
    index = (layer,) + (0,) * (arr.ndim - 2) + (col_block,)
    return pl.BlockSpec(shape, lambda *_: index, pipeline_mode=pl.Buffered(1))


def _in_proj(h3, norm_w, weights, outputs, dt_piece=None, *, layer, tm, chunk, name):
    b, s, d = h3.shape
    in_specs = [pl.BlockSpec((1, tm, d), lambda bi, i: (bi, i, 0)), _layer_spec(norm_w, layer)]
    in_specs += [_layer_spec(arr, layer, blk, cb) for arr, blk, cb in weights]
    shapes = [(dil, sum(p[2] for p in pieces)) for dil, pieces in outputs]
    out_specs = [pl.BlockSpec((1, dil, tm // dil, n), lambda bi, i: (bi, 0, i, 0)) for dil, n in shapes]
    out_shape = [jax.ShapeDtypeStruct((b, dil, s // dil, n), BF16) for dil, n in shapes]
    if dt_piece is not None:
        out_specs.append(pl.BlockSpec((1, tm, LANES), lambda bi, i: (bi, i, 0)))
        out_shape.append(jax.ShapeDtypeStruct((b, s, LANES), F32))
    return pl.pallas_call(
        functools.partial(_in_proj_kernel, outputs=outputs, dt_piece=dt_piece, chunk=chunk),
        grid=(b, s // tm),
        in_specs=in_specs,
        out_specs=out_specs,
        out_shape=out_shape,
        scratch_shapes=[pltpu.VMEM((tm, d), BF16) for _ in {dil for dil, _ in outputs}],
        compiler_params=_compiler_params(("parallel", "parallel")),
        name=name,
    )(h3, norm_w, *[arr for arr, _, _ in weights])


def _attention_kernel(q_ref, kp_ref, kc_ref, vp_ref, vc_ref, bias_ref, o_ref, l_ref, kext_ref, vext_ref, *, tq):
    n = pl.program_id(2)
    kext_ref[0:ATTN_BLOCK] = kp_ref[0, 0]
    kext_ref[ATTN_BLOCK:] = kc_ref[0, 0]
    vext_ref[0:ATTN_BLOCK] = vp_ref[0, 0]
    vext_ref[ATTN_BLOCK:] = vc_ref[0, 0]
    low_half = lax.broadcasted_iota(jnp.int32, (1, LANES), 1) < HEAD_DIM

    for s in range(tq // ATTN_BLOCK):
        r0 = s * ATTN_BLOCK
        qs = q_ref[0, 0, pl.ds(r0, ATTN_BLOCK), :] * (HEAD_DIM ** -0.5)
        ks = kext_ref[pl.ds(r0, 2 * ATTN_BLOCK), :]
        vs = vext_ref[pl.ds(r0, 2 * ATTN_BLOCK), :]
        bidx = jnp.where(n == 0, 0, 1) if s == 0 else 1
        for p in range(GROUP_WIDTH // LANES):
            cols = slice(p * LANES, (p + 1) * LANES)
            qp, kp, vp = qs[:, cols], ks[:, cols], vs[:, cols]
            q2 = jnp.concatenate([jnp.where(low_half, qp, jnp.zeros_like(qp)),
                                  jnp.where(low_half, jnp.zeros_like(qp), qp)], axis=0)
            bias2 = jnp.concatenate([bias_ref[bidx, 2 * p], bias_ref[bidx, 2 * p + 1]], axis=0)
            sc = lax.dot_general(q2, kp, NT_DIMS, preferred_element_type=F32) + bias2
            m = jnp.max(sc, axis=1, keepdims=True)
            pe = jnp.exp(sc - m).astype(BF16)
            nd = jnp.dot(pe, jnp.concatenate([vp, jnp.ones_like(vp)], axis=1), preferred_element_type=F32)
            num, den = nd[:, :LANES], nd[:, LANES:]
            o2 = num / den
            l2 = m + jnp.log(den)
            outs = [o2[:ATTN_BLOCK], o2[ATTN_BLOCK:]]
            lses = [l2[:ATTN_BLOCK], l2[ATTN_BLOCK:]]
            o_ref[0, 0, pl.ds(r0, ATTN_BLOCK), cols] = jnp.where(low_half, outs[0], outs[1])
            l_ref[0, 0, pl.ds(r0, ATTN_BLOCK), cols] = jnp.where(low_half, lses[0], lses[1])


def _attention(qkv, bias_g, *, col_q, col_k, col_v, tq):
    b, dilation, seg, _ = qkv.shape
    blocks_per_tile = tq // ATTN_BLOCK

    def cur(col):
        return pl.BlockSpec((1, 1, tq, GROUP_WIDTH), lambda bi, r, n: (bi, r, n, col))

    def prev(col):
        return pl.BlockSpec((1, 1, ATTN_BLOCK, GROUP_WIDTH),
                            lambda bi, r, n: (bi, r, jnp.maximum(n * blocks_per_tile - 1, 0), col))

    out_spec = pl.BlockSpec((1, 1, tq, GROUP_WIDTH), lambda bi, r, n: (bi, r, n, 0))
    out_sds = jax.ShapeDtypeStruct((b, dilation, seg, GROUP_WIDTH), F32)
    return pl.pallas_call(
        functools.partial(_attention_kernel, tq=tq),
        grid=(b, dilation, seg // tq),
        in_specs=[cur(col_q), prev(col_k), cur(col_k), prev(col_v), cur(col_v),
                  pl.BlockSpec(bias_g.shape, lambda bi, r, n: (0, 0, 0, 0))],
        out_specs=[out_spec, out_spec],
        out_shape=[out_sds, out_sds],
        scratch_shapes=[pltpu.VMEM((tq + ATTN_BLOCK, GROUP_WIDTH), BF16),
                        pltpu.VMEM((tq + ATTN_BLOCK, GROUP_WIDTH), BF16)],
        compiler_params=_compiler_params(("parallel", "parallel", "arbitrary")),
        name=f"attention_d{dilation}",
    )(qkv, qkv, qkv, qkv, qkv, bias_g)


def _t5_causal_bucket(dist):
    max_exact = N_REL_BUCKETS // 2
    d_f = jnp.maximum(dist, 1).astype(F32)
    large = max_exact + (jnp.log(d_f / max_exact) / math.log(REL_MAX_DISTANCE / max_exact)
                         * (N_REL_BUCKETS - max_exact)).astype(jnp.int32)
    large = jnp.minimum(large, N_REL_BUCKETS - 1)
    return jnp.where(dist < max_exact, dist, large)


def _attention_bias_tables(rel_bias):
    q, q2 = ATTN_BLOCK, 2 * ATTN_BLOCK
    steps = jnp.arange(q2) - (q - 1)
    in_prev = (jnp.arange(q2) < q)[None, None, :]
    tables = []
    for g, (window, dil) in enumerate(DILATED_GROUPS):
        n_steps = window // dil
        assert n_steps <= q
        rel_g = rel_bias[:, g * HEADS_PER_GROUP:(g + 1) * HEADS_PER_GROUP].astype(F32)
        vals = rel_g[_t5_causal_bucket(jnp.clip(steps, 0, n_steps) * dil)]
        vec = jnp.where(((steps >= 0) & (steps <= n_steps))[:, None], vals, -jnp.inf).T
        skew = jnp.tile(vec, (1, q + 1))[:, :q * (q2 + 1)].reshape(HEADS_PER_GROUP, q, q2 + 1)[:, :, :q2]
        rest = skew[:, :, ::-1]
        first = jnp.where(in_prev, -jnp.inf, rest)
        tables.append(jnp.stack([first, rest]))
    return jnp.stack(tables)


LOG2E = math.log2(math.e)


def _conv_silu(slab_ref, c, w_ref, b_ref):
    q = slab_ref.shape[1] - SUBLANES
    cols = slice(c * LANES, (c + 1) * LANES)
    acc = b_ref[:, cols] + slab_ref[c, pl.ds(SUBLANES, q), :] * w_ref[CONV_WIDTH - 1:CONV_WIDTH, cols]
    for s in range(1, CONV_WIDTH):
        acc = acc + slab_ref[c, pl.ds(SUBLANES - s, q), :] * w_ref[CONV_WIDTH - 1 - s:CONV_WIDTH - s, cols]
    return acc * _sigmoid(acc)


def _ssd_decays(r0, dt_ref, dtb_ref, alog_ref):
    q = SSD_CHUNK
    x_dt = dt_ref[0, r0:r0 + q, :] + dtb_ref[...]
    dt = jnp.maximum(x_dt, 0.0) + jnp.log(1.0 + jnp.exp(-jnp.abs(x_dt)))
    d_a = dt * (-jnp.exp(alog_ref[...]))
    ri = lax.broadcasted_iota(jnp.int32, (q, q), 0)
    ci = lax.broadcasted_iota(jnp.int32, (q, q), 1)
    causal_t = ri <= ci
    tril = jnp.where(ri >= ci, 1.0, 0.0).astype(BF16)
    d1 = d_a.astype(BF16)
    r1 = d_a - d1.astype(F32)
    d2 = r1.astype(BF16)
    d3 = (r1 - d2.astype(F32)).astype(BF16)
    la = (jnp.dot(tril, d1, preferred_element_type=F32) + jnp.dot(tril, d2, preferred_element_type=F32)
          + jnp.dot(tril, d3, preferred_element_type=F32))
    la = la * LOG2E
    la_t = la.T
    dt_t = dt.T
    last = jnp.broadcast_to(la_t[:, q - 1:q], (q, q))
    w_t = jnp.exp2(last - la_t) * dt_t
    state_decay = jnp.exp2(last)
    return causal_t, la, la_t, dt_t, w_t, state_decay


def _ssd_chunk_stages(r0, decays, xs_ref, bc_ref, z_ref, cw_ref, cb_ref, dskip_ref, nw_ref,
                      ssm_ref, state_ref, slab_ref):
    causal_t, la, la_t, dt_t, w_t, state_decay = decays
    q = SSD_CHUNK
    rq = slice(r0, r0 + q)
    gw = HEADS_PER_SSM_GROUP * SSM_HEAD_DIM
    xs_slabs = xs_ref.shape[3] // LANES
    slabs_per_group = gw // LANES

    for c in range(xs_slabs):
        slab_ref[c, SUBLANES:, :] = xs_ref[0, 0, rq, c * LANES:(c + 1) * LANES].astype(F32)
    for c in range(bc_ref.shape[3] // LANES):
        slab_ref[xs_slabs + c, SUBLANES:, :] = bc_ref[0, 0, rq, c * LANES:(c + 1) * LANES].astype(F32)

    for g in range(N_SSM_GROUPS):
        bm = _conv_silu(slab_ref, xs_slabs + g, cw_ref, cb_ref)
        cm = _conv_silu(slab_ref, xs_slabs + N_SSM_GROUPS + g, cw_ref, cb_ref)
        bm16 = bm.astype(BF16)
        cb_t = lax.dot_general(bm16, cm.astype(BF16), NT_DIMS, preferred_element_type=F32)
        cm_t = cm.T
        xs_g = jnp.concatenate([_conv_silu(slab_ref, g * slabs_per_group + c, cw_ref, cb_ref)
                                for c in range(slabs_per_group)], axis=1)
        xs_t = xs_g.T
        yield
        y_t = []
        for h in range(HEADS_PER_SSM_GROUP):
            hh = g * HEADS_PER_SSM_GROUP + h
            rows = slice(hh * SSM_HEAD_DIM, (hh + 1) * SSM_HEAD_DIM)
            la_i = la_t[hh:hh + 1, :]
            la_j = jnp.broadcast_to(la[:, hh:hh + 1], (q, q))
            decay_t = jnp.exp2(jnp.where(causal_t, la_i - la_j, -jnp.inf))
            rhs = jnp.concatenate([(cb_t * decay_t).astype(BF16), (cm_t * jnp.exp2(la_i)).astype(BF16)], axis=0)
            x_h = xs_t[h * SSM_HEAD_DIM:(h + 1) * SSM_HEAD_DIM]
            state = state_ref[rows, :]
            lhs = jnp.concatenate([(x_h * dt_t[hh:hh + 1, :]).astype(BF16), state.astype(BF16)], axis=1)
            y_t.append(jnp.dot(lhs, rhs, preferred_element_type=F32))
            yield
        heads = range(g * HEADS_PER_SSM_GROUP, (g + 1) * HEADS_PER_SSM_GROUP)
        grows = slice(heads[0] * SSM_HEAD_DIM, (heads[-1] + 1) * SSM_HEAD_DIM)
        def scale_heads(x, t):
            hp = x.reshape(HEADS_PER_SSM_GROUP, SSM_HEAD_DIM, q) * t[heads[0]:heads[-1] + 1][:, None, :]
            return hp.reshape(x.shape)
        upd = jnp.dot(scale_heads(xs_t, w_t).astype(BF16), bm16, preferred_element_type=F32)
        state_ref[grows, :] = scale_heads(state_ref[grows, :], state_decay) + upd
        yield
        y = jnp.concatenate(y_t, axis=0).T
        cols = slice(g * gw, (g + 1) * gw)
        y = y + xs_g * dskip_ref[:, cols]
        z = z_ref[0, 0, rq, cols].astype(F32)
        yg = y * (z * _sigmoid(z))
        yg = yg * lax.rsqrt(jnp.mean(yg * yg, axis=-1, keepdims=True) + EPS)
        ssm_ref[rq, cols] = (yg * nw_ref[:, cols]).astype(ssm_ref.dtype)
        yield

    slab_ref[:, 0:SUBLANES, :] = slab_ref[:, q:q + SUBLANES, :]


def _merge_ffn_stages(attn_refs, ssm_ref, gate_ref, h_ref, wa_ref, ws_ref, wo_ref, nat_ref,
                      nw_ref, wgu_ref, wd_ref, fw_ref, out_ref, *, ffn_chunk, final_norm):
    tb = h_ref.shape[1]
    n_slabs = GROUP_WIDTH // LANES
    for k, src in enumerate(attn_refs):
        dil = src.shape[1]
        for r in range(dil if dil > 1 else 0):
            for c in range(n_slabs):
                nat_ref[k, c, pl.ds(r, tb // dil, stride=dil), :] = src[0, r, :, c * LANES:(c + 1) * LANES]

    def natural(k, c):
        src = attn_refs[k]
        return src[0, 0, :, c * LANES:(c + 1) * LANES] if src.shape[1] == 1 else nat_ref[k, c]

    slabs = []
    for c in range(n_slabs):
        o = [natural(2 * g, c) for g in range(N_DIL)]
        l = [natural(2 * g + 1, c) for g in range(N_DIL)]
        mx = functools.reduce(jnp.maximum, l)
        e = [jnp.exp(lg - mx) for lg in l]
        num = functools.reduce(jnp.add, [eg * og for eg, og in zip(e, o)])
        slabs.append((num / functools.reduce(jnp.add, e)).astype(BF16))
    attn = jnp.concatenate(slabs, axis=1)
    yield
    a = jnp.dot(attn, wa_ref[...], preferred_element_type=F32)
    yield
    s = jnp.dot(ssm_ref[...], ws_ref[...], preferred_element_type=F32)
    d = a.shape[1]
    gates = _sigmoid(gate_ref[0, 0].astype(F32))
    merged = (gates[:, :d] * a + gates[:, d:] * s).astype(BF16)
    yield
    acc = h_ref[0] + jnp.dot(merged, wo_ref[...], preferred_element_type=F32)
    xn = _rmsnorm(acc, nw_ref[...]).astype(BF16)
    yield
    d_ff = wd_ref.shape[0]
    chunks = [slice(c0, c0 + ffn_chunk) for c0 in range(0, d_ff, ffn_chunk)]
    gate_up = None
    for c, cols in enumerate(chunks + [None]):
        prev_cols, prev_gate_up = (chunks[c - 1], gate_up) if c > 0 else (None, None)
        if cols is not None:
            up_cols = slice(d_ff + cols.start, d_ff + cols.stop)
            gate = jnp.dot(xn, wgu_ref[:, cols], preferred_element_type=F32)
            yield
            gate_up = (gate, jnp.dot(xn, wgu_ref[:, up_cols], preferred_element_type=F32))
            yield
        if prev_gate_up is not None:
            act = (prev_gate_up[0] * _sigmoid(prev_gate_up[0]) * prev_gate_up[1]).astype(BF16)
            acc = acc + jnp.dot(act, wd_ref[prev_cols, :], preferred_element_type=F32)
            yield
    out_ref[0] = _rmsnorm(acc, fw_ref[...]) if final_norm else acc


N_TAIL_WEIGHTS = 13


def _tail_kernel(*refs, ffn_chunk, ssd_stages_per_matmul, final_norm):
    attn_refs = refs[:2 * N_DIL]
    gate_ref, h_ref, xs_ref, bc_ref, z_ref, dt_ref = refs[2 * N_DIL:2 * N_DIL + 6]
    weight_refs = refs[2 * N_DIL + 6:2 * N_DIL + 6 + N_TAIL_WEIGHTS]
    (wa_ref, ws_ref, wo_ref, n2_ref, wgu_ref, wd_ref, fw_ref,
     cw_ref, cb_ref, dtb_ref, alog_ref, dskip_ref, nw_ref) = [r.at[0] for r in weight_refs]
    out_ref, state_ref, slab_ref, ssm_ref, new_ref, nat_ref = refs[2 * N_DIL + 6 + N_TAIL_WEIGHTS:]

    @pl.when(pl.program_id(1) == 0)
    def _():
        state_ref[...] = jnp.zeros_like(state_ref)
        slab_ref[:, 0:SUBLANES, :] = jnp.zeros((slab_ref.shape[0], SUBLANES, LANES), F32)
        ssm_ref[...] = jnp.zeros_like(ssm_ref)

    def ssd_stages():
        for r0 in range(0, xs_ref.shape[2], SSD_CHUNK):
            decays = _ssd_decays(r0, dt_ref, dtb_ref, alog_ref)
            yield
            yield from _ssd_chunk_stages(r0, decays, xs_ref, bc_ref, z_ref, cw_ref, cb_ref, dskip_ref, nw_ref,
                                         new_ref, state_ref, slab_ref)

    matmuls = _merge_ffn_stages(attn_refs, ssm_ref, gate_ref, h_ref, wa_ref, ws_ref, wo_ref, nat_ref,
                                n2_ref, wgu_ref, wd_ref, fw_ref, out_ref, ffn_chunk=ffn_chunk, final_norm=final_norm)
    turns = [(matmuls, 1), (ssd_stages(), ssd_stages_per_matmul)]
    live = [gen for gen, _ in turns]
    while live:
        for gen, reps in turns:
            for _ in range(reps if gen in live else 0):
                if next(gen, StopIteration) is StopIteration:
                    live.remove(gen)
                    break
    ssm_ref[...] = new_ref[...]


def _tail(attn_outs, gates, xs, bc, z, dt3, h3, weights, *, layer, tb, ffn_chunk, ssd_stages_per_matmul,
          final_norm):
    b, s, d = h3.shape
    assert len(weights) == N_TAIL_WEIGHTS
    conv_ch = weights[7].shape[-1]
    d_inner = xs.shape[-1]
    n_blocks = s // tb

    def cur(k):
        return jnp.minimum(k, n_blocks - 1)

    def prev(k):
        return jnp.maximum(k - 1, 0)

    def residues(arr):
        dil = arr.shape[1]
        return pl.BlockSpec((1, dil, tb // dil, GROUP_WIDTH), lambda bi, k: (bi, 0, prev(k), 0))

    def rows_of(arr, block_index):
        return pl.BlockSpec((1, 1, tb, arr.shape[-1]), lambda bi, k: (bi, 0, block_index(k), 0))

    flat = [a for pair in attn_outs for a in pair]
    in_specs = [residues(a) for a in flat] + [
        rows_of(gates, prev),
        pl.BlockSpec((1, tb, d), lambda bi, k: (bi, prev(k), 0)),
        rows_of(xs, cur), rows_of(bc, cur), rows_of(z, cur),
        pl.BlockSpec((1, tb, LANES), lambda bi, k: (bi, cur(k), 0)),
    ] + [_layer_spec(w, min(layer, w.shape[0] - 1)) for w in weights]
    return pl.pallas_call(
        functools.partial(_tail_kernel, ffn_chunk=ffn_chunk, ssd_stages_per_matmul=ssd_stages_per_matmul,
                          final_norm=final_norm),
        grid=(b, n_blocks + 1),
        in_specs=in_specs,
        out_specs=pl.BlockSpec((1, tb, d), lambda bi, k: (bi, prev(k), 0)),
        out_shape=jax.ShapeDtypeStruct((b, s, d), F32),
        scratch_shapes=[pltpu.VMEM((N_SSM_GROUPS * HEADS_PER_SSM_GROUP * SSM_HEAD_DIM, D_STATE), F32),
                        pltpu.VMEM((conv_ch // LANES, SUBLANES + SSD_CHUNK, LANES), F32),
                        pltpu.VMEM((tb, d_inner), BF16),
                        pltpu.VMEM((tb, d_inner), BF16),
                        pltpu.VMEM((len(flat), GROUP_WIDTH // LANES, tb, LANES), F32)],
        compiler_params=_compiler_params(("parallel", "arbitrary")),
        name="layer_tail",
    )(*flat, gates, h3, xs, bc, z, dt3, *weights)


def kernel(x, norm1_w, w_in, conv_w, conv_b, dt_bias, a_log, d_skip, ssm_norm_w, w_attn_branch, w_ssm_branch,
           w_out, norm2_w, w_ffn_in, w_ffn_out, rel_bias, final_norm_w):
    b, s, d = x.shape
    depth = w_in.shape[0]
    d_inner = ssm_norm_w.shape[1]
    n_heads = dt_bias.shape[1]
    bc_w = 2 * N_SSM_GROUPS * D_STATE
    assert d_inner == N_SSM_GROUPS * HEADS_PER_SSM_GROUP * SSM_HEAD_DIM and n_heads <= LANES
    assert s % (DILATED_GROUPS[-1][1] * ATTN_BLOCK) == 0 and s % SSD_CHUNK == 0 and DILATED_GROUPS[0][1] == 1

    o_z = 3 * ATTN_WIDTH
    o_xbc = o_z + d_inner
    o_dt = o_xbc + d_inner + bc_w
    o_gate = o_dt + n_heads
    assert o_dt % LANES == 0 and o_dt + LANES <= w_in.shape[2]
    w_in16 = w_in.astype(BF16)
    w_gates16 = w_in[:, :, o_gate:].astype(BF16)
    qkv_start = lambda part, g: part * ATTN_WIDTH + g * GROUP_WIDTH
    main_weights = [(w_in16, None, 0), (w_gates16, None, 0)]
    main_outputs = ((1, ((0, o_z, d_inner),)), (1, ((1, 0, 2 * d),)), (1, ((0, o_xbc, d_inner),)),
                    (1, ((0, o_xbc + d_inner, bc_w),)),
                    (1, tuple((0, qkv_start(part, 0), GROUP_WIDTH) for part in range(3))))
    dilated = [(g, dil) for g, (_, dil) in enumerate(DILATED_GROUPS) if dil > 1]
    dilated_weights = [(w_in16, GROUP_WIDTH, qkv_start(part, g) // GROUP_WIDTH) for g, _ in dilated for part in range(3)]
    dilated_outputs = tuple((dil, tuple((3 * n + part, 0, GROUP_WIDTH) for part in range(3)))
                            for n, (_, dil) in enumerate(dilated))

    pad_h = ((0, 0), (0, 0), (0, LANES - n_heads))
    row3 = lambda a: a[:, None, :]
    tail_weights = (w_attn_branch.astype(BF16), w_ssm_branch.astype(BF16), w_out.astype(BF16), row3(norm2_w),
                    w_ffn_in.astype(BF16), w_ffn_out.astype(BF16), final_norm_w[None, None, :],
                    conv_w, row3(conv_b), jnp.pad(row3(dt_bias), pad_h), jnp.pad(row3(a_log), pad_h),
                    row3(jnp.repeat(d_skip, SSM_HEAD_DIM, axis=1)), row3(ssm_norm_w))
    norm1_w3 = row3(norm1_w)
    bias_tables = _attention_bias_tables(rel_bias)

    h = x
    for layer in range(depth):
        z, gates, xs, bc, qkv0, dt = _in_proj(h, norm1_w3, main_weights, main_outputs, (0, o_dt, LANES),
                                              layer=layer, tm=IN_PROJ_ROWS, chunk=IN_PROJ_CHUNK, name="in_proj")
        qkv_dilated = _in_proj(h, norm1_w3, dilated_weights, dilated_outputs, layer=layer,
                               tm=IN_PROJ_DILATED_ROWS, chunk=IN_PROJ_CHUNK, name="in_proj_dilated")
        qkvs = {0: qkv0, **{g: arr for (g, _), arr in zip(dilated, qkv_dilated)}}
        attn_outs = [_attention(qkvs[g], bias_tables[g], col_q=0, col_k=1, col_v=2, tq=min(ATTN_ROWS, s // dil))
                     for g, (_, dil) in enumerate(DILATED_GROUPS)]
        h = _tail(attn_outs, gates, xs, bc, z, dt, h, tail_weights, layer=layer, tb=TAIL_ROWS,
                  ffn_chunk=FFN_CHUNK, ssd_stages_per_matmul=SSD_STAGES_PER_MATMUL,
                  final_norm=(layer == depth - 1))
    return h
```

```python
import functools
import math

import jax
import jax.numpy as jnp
from jax import lax
from jax.experimental import pallas as pl
from jax.experimental.pallas import tpu as pltpu

HEAD_DIM = 64
DILATED_GROUPS = ((128, 1), (512, 4), (2048, 16))
N_DIL = len(DILATED_GROUPS)
HEADS_PER_GROUP = 8
GROUP_WIDTH = HEADS_PER_GROUP * HEAD_DIM
ATTN_WIDTH = N_DIL * GROUP_WIDTH
ATTN_BLOCK = 128
N_REL_BUCKETS = 32
REL_MAX_DISTANCE = 2048
SSM_HEAD_DIM = 64
N_SSM_GROUPS = 4
HEADS_PER_SSM_GROUP = 8
D_STATE = 128
CONV_WIDTH = 4
SSD_CHUNK = 128
EPS = 1e-6

LANES = 128
SUBLANES = 8
VMEM_LIMIT_BYTES = 60000 * 1024

IN_PROJ_ROWS = 512
IN_PROJ_DILATED_ROWS = 1024
IN_PROJ_CHUNK = 1024
ATTN_ROWS = 2048
TAIL_ROWS = 256
FFN_CHUNK = 256
SSD_STAGES_PER_MATMUL = 2

BF16 = jnp.bfloat16
F32 = jnp.float32
NT_DIMS = (((1,), (1,)), ((), ()))


def _compiler_params(semantics):
    return pltpu.CompilerParams(dimension_semantics=semantics, vmem_limit_bytes=VMEM_LIMIT_BYTES)


def _rmsnorm(x, w):
    return x * lax.rsqrt(jnp.mean(x * x, axis=-1, keepdims=True) + EPS) * w


def _sigmoid(x):
    return 1.0 / (1.0 + jnp.exp(-x))


PERM_BLOCK = 256


def _in_proj_kernel(x_ref, nw_ref, *rest, outputs, dt_piece, chunk):
    n_w = 1 + max(p[0] for _, pieces in outputs for p in pieces)
    w_refs = [r.at[0] for r in rest[:n_w]]
    out_refs = rest[n_w:n_w + len(outputs)]
    rest = rest[n_w + len(outputs):]
    if dt_piece is not None:
        dt_ref, rest = rest[0], rest[1:]
    dilations = sorted({dil for dil, _ in outputs})
    xn_refs = dict(zip(dilations, rest))
    nw_ref = nw_ref.at[0]
    tm = x_ref.shape[1]
    xn = _rmsnorm(x_ref[0], nw_ref[...]).astype(BF16)
    if dt_piece is not None:
        src, c0, width = dt_piece
        dt_ref[0] = jnp.dot(xn, w_refs[src][:, c0:c0 + width], preferred_element_type=F32)
    for dilation, xn_ref in xn_refs.items():
        if dilation == 1:
            xn_ref[...] = xn
            continue
        per = PERM_BLOCK // dilation
        i = lax.broadcasted_iota(jnp.int32, (PERM_BLOCK, PERM_BLOCK), 0)
        k = lax.broadcasted_iota(jnp.int32, (PERM_BLOCK, PERM_BLOCK), 1)
        src = (i & (per - 1)) * dilation + (i >> (per.bit_length() - 1))
        perm = jnp.where(k == src, 1.0, 0.0).astype(BF16)
        for u in range(tm // PERM_BLOCK):
            y = jnp.dot(perm, xn[u * PERM_BLOCK:(u + 1) * PERM_BLOCK], preferred_element_type=F32).astype(BF16)
            for r in range(dilation):
                dst = r * (tm // dilation) + u * per
                xn_ref[dst:dst + per, :] = y[r * per:(r + 1) * per]
    for out_ref, (dilation, pieces) in zip(out_refs, outputs):
        col = 0
        for src, start, width in pieces:
            for c0 in range(0, width, chunk):
                cw = min(chunk, width - c0)
                res = jnp.dot(xn_refs[dilation][...], w_refs[src][:, start + c0:start + c0 + cw],
                              preferred_element_type=F32)
                out_ref[0, :, :, col:col + cw] = res.astype(out_ref.dtype).reshape(dilation, tm // dilation, cw)
                col += cw


def _layer_spec(arr, layer, block=None, col_block=0):
    shape = (1,) + tuple(arr.shape[1:-1]) + (block or arr.shape[-1],)
    index = (layer,) + (0,) * (arr.ndim - 2) + (col_block,)
    return pl.BlockSpec(shape, lambda *_: index, pipeline_mode=pl.Buffered(1))


def _in_proj(h3, norm_w, weights, outputs, dt_piece=None, *, layer, tm, chunk, name):
    b, s, d = h3.shape
    in_specs = [pl.BlockSpec((1, tm, d), lambda bi, i: (bi, i, 0)), _layer_spec(norm_w, layer)]
    in_specs += [_layer_spec(arr, layer, blk, cb) for arr, blk, cb in weights]
    shapes = [(dil, sum(p[2] for p in pieces)) for dil, pieces in outputs]
    out_specs = [pl.BlockSpec((1, dil, tm // dil, n), lambda bi, i: (bi, 0, i, 0)) for dil, n in shapes]
    out_shape = [jax.ShapeDtypeStruct((b, dil, s // dil, n), BF16) for dil, n in shapes]
    if dt_piece is not None:
        out_specs.append(pl.BlockSpec((1, tm, LANES), lambda bi, i: (bi, i, 0)))
        out_shape.append(jax.ShapeDtypeStruct((b, s, LANES), F32))
    return pl.pallas_call(
        functools.partial(_in_proj_kernel, outputs=outputs, dt_piece=dt_piece, chunk=chunk),
        grid=(b, s // tm),
        in_specs=in_specs,
        out_specs=out_specs,
        out_shape=out_shape,
        scratch_shapes=[pltpu.VMEM((tm, d), BF16) for _ in {dil for dil, _ in outputs}],
        compiler_params=_compiler_params(("parallel", "parallel")),
        name=name,
    )(h3, norm_w, *[arr for arr, _, _ in weights])


def _attention_kernel(q_ref, kp_ref, kc_ref, vp_ref, vc_ref, bias_ref, o_ref, l_ref, kext_ref, vext_ref, *, tq):
    n = pl.program_id(2)
    kext_ref[:, 0:ATTN_BLOCK] = kp_ref[0]
    kext_ref[:, ATTN_BLOCK:] = kc_ref[0]
    vext_ref[:, 0:ATTN_BLOCK] = vp_ref[0]
    vext_ref[:, ATTN_BLOCK:] = vc_ref[0]
    low_half = lax.broadcasted_iota(jnp.int32, (1, LANES), 1) < HEAD_DIM

    for rr, s in ((rr, s) for rr in range(q_ref.shape[1]) for s in range(tq // ATTN_BLOCK)):
        r0 = s * ATTN_BLOCK
        qs = q_ref[0, rr, pl.ds(r0, ATTN_BLOCK), :] * (HEAD_DIM ** -0.5)
        ks = kext_ref[rr, pl.ds(r0, 2 * ATTN_BLOCK), :]
        vs = vext_ref[rr, pl.ds(r0, 2 * ATTN_BLOCK), :]
        bidx = jnp.where(n == 0, 0, 1) if s == 0 else 1
        for p in range(GROUP_WIDTH // LANES):
            cols = slice(p * LANES, (p + 1) * LANES)
            qp, kp, vp = qs[:, cols], ks[:, cols], vs[:, cols]
            q2 = jnp.concatenate([jnp.where(low_half, qp, jnp.zeros_like(qp)),
                                  jnp.where(low_half, jnp.zeros_like(qp), qp)], axis=0)
            bias2 = jnp.concatenate([bias_ref[bidx, 2 * p], bias_ref[bidx, 2 * p + 1]], axis=0)
            sc = lax.dot_general(q2, kp, NT_DIMS, preferred_element_type=F32) + bias2
            m = jnp.max(sc, axis=1, keepdims=True)
            pe = jnp.exp(sc - m).astype(BF16)
            nd = jnp.dot(pe, jnp.concatenate([vp, jnp.ones_like(vp)], axis=1), preferred_element_type=F32)
            num, den = nd[:, :LANES], nd[:, LANES:]
            o2 = num / den
            l2 = m + jnp.log(den)
            outs = [o2[:ATTN_BLOCK], o2[ATTN_BLOCK:]]
            lses = [l2[:ATTN_BLOCK], l2[ATTN_BLOCK:]]
            o_ref[0, rr, pl.ds(r0, ATTN_BLOCK), cols] = jnp.where(low_half, outs[0], outs[1])
            l_ref[0, rr, pl.ds(r0, ATTN_BLOCK), cols] = jnp.where(low_half, lses[0], lses[1])


def _attention(qkv, bias_g, *, col_q, col_k, col_v, rows):
    b, dilation, seg, _ = qkv.shape
    tq = min(rows, seg)
    res = min(dilation, rows // tq)
    blocks_per_tile = tq // ATTN_BLOCK

    def cur(col):
        return pl.BlockSpec((1, res, tq, GROUP_WIDTH), lambda bi, r, n: (bi, r, n, col))

    def prev(col):
        return pl.BlockSpec((1, res, ATTN_BLOCK, GROUP_WIDTH),
                            lambda bi, r, n: (bi, r, jnp.maximum(n * blocks_per_tile - 1, 0), col))

    out_spec = pl.BlockSpec((1, res, tq, GROUP_WIDTH), lambda bi, r, n: (bi, r, n, 0))
    out_sds = jax.ShapeDtypeStruct((b, dilation, seg, GROUP_WIDTH), F32)
    return pl.pallas_call(
        functools.partial(_attention_kernel, tq=tq),
        grid=(b, dilation // res, seg // tq),
        in_specs=[cur(col_q), prev(col_k), cur(col_k), prev(col_v), cur(col_v),
                  pl.BlockSpec(bias_g.shape, lambda bi, r, n: (0, 0, 0, 0))],
        out_specs=[out_spec, out_spec],
        out_shape=[out_sds, out_sds],
        scratch_shapes=[pltpu.VMEM((res, tq + ATTN_BLOCK, GROUP_WIDTH), BF16),
                        pltpu.VMEM((res, tq + ATTN_BLOCK, GROUP_WIDTH), BF16)],
        compiler_params=_compiler_params(("parallel", "parallel", "arbitrary")),
        name=f"attention_d{dilation}",
    )(qkv, qkv, qkv, qkv, qkv, bias_g)


def _t5_causal_bucket(dist):
    max_exact = N_REL_BUCKETS // 2
    d_f = jnp.maximum(dist, 1).astype(F32)
    large = max_exact + (jnp.log(d_f / max_exact) / math.log(REL_MAX_DISTANCE / max_exact)
                         * (N_REL_BUCKETS - max_exact)).astype(jnp.int32)
    large = jnp.minimum(large, N_REL_BUCKETS - 1)
    return jnp.where(dist < max_exact, dist, large)


def _attention_bias_tables(rel_bias):
    q, q2 = ATTN_BLOCK, 2 * ATTN_BLOCK
    steps = jnp.arange(q2) - (q - 1)
    in_prev = (jnp.arange(q2) < q)[None, None, :]
    tables = []
    for g, (window, dil) in enumerate(DILATED_GROUPS):
        n_steps = window // dil
        assert n_steps <= q
        rel_g = rel_bias[:, g * HEADS_PER_GROUP:(g + 1) * HEADS_PER_GROUP].astype(F32)
        vals = rel_g[_t5_causal_bucket(jnp.clip(steps, 0, n_steps) * dil)]
        vec = jnp.where(((steps >= 0) & (steps <= n_steps))[:, None], vals, -jnp.inf).T
        skew = jnp.tile(vec, (1, q + 1))[:, :q * (q2 + 1)].reshape(HEADS_PER_GROUP, q, q2 + 1)[:, :, :q2]
        rest = skew[:, :, ::-1]
        first = jnp.where(in_prev, -jnp.inf, rest)
        tables.append(jnp.stack([first, rest]))
    return jnp.stack(tables)


LOG2E = math.log2(math.e)


def _conv_silu(slab_ref, c, w_ref, b_ref):
    q = slab_ref.shape[1] - SUBLANES
    cols = slice(c * LANES, (c + 1) * LANES)
    acc = b_ref[:, cols] + slab_ref[c, pl.ds(SUBLANES, q), :] * w_ref[CONV_WIDTH - 1:CONV_WIDTH, cols]
    for s in range(1, CONV_WIDTH):
        acc = acc + slab_ref[c, pl.ds(SUBLANES - s, q), :] * w_ref[CONV_WIDTH - 1 - s:CONV_WIDTH - s, cols]
    return acc * _sigmoid(acc)


def _ssd_decays(r0, dt_ref, dtb_ref, alog_ref):
    q = SSD_CHUNK
    x_dt = dt_ref[0, r0:r0 + q, :] + dtb_ref[...]
    dt = jnp.maximum(x_dt, 0.0) + jnp.log(1.0 + jnp.exp(-jnp.abs(x_dt)))
    d_a = dt * (-jnp.exp(alog_ref[...]))
    ri = lax.broadcasted_iota(jnp.int32, (q, q), 0)
    ci = lax.broadcasted_iota(jnp.int32, (q, q), 1)
    causal_t = ri <= ci
    tril = jnp.where(ri >= ci, 1.0, 0.0).astype(BF16)
    d1 = d_a.astype(BF16)
    r1 = d_a - d1.astype(F32)
    d2 = r1.astype(BF16)
    d3 = (r1 - d2.astype(F32)).astype(BF16)
    la = (jnp.dot(tril, d1, preferred_element_type=F32) + jnp.dot(tril, d2, preferred_element_type=F32)
          + jnp.dot(tril, d3, preferred_element_type=F32))
    la = la * LOG2E
    la_t = la.T
    dt_t = dt.T
    last = jnp.broadcast_to(la_t[:, q - 1:q], (q, q))
    w_t = jnp.exp2(last - la_t) * dt_t
    state_decay = jnp.exp2(last)
    return causal_t, la, la_t, dt_t, w_t, state_decay


def _ssd_chunk_stages(r0, decays, xs_ref, bc_ref, z_ref, cw_ref, cb_ref, dskip_ref, nw_ref,
                      ssm_ref, state_ref, slab_ref):
    causal_t, la, la_t, dt_t, w_t, state_decay = decays
    q = SSD_CHUNK
    rq = slice(r0, r0 + q)
    gw = HEADS_PER_SSM_GROUP * SSM_HEAD_DIM
    xs_slabs = xs_ref.shape[3] // LANES
    slabs_per_group = gw // LANES

    for c in range(xs_slabs):
        slab_ref[c, SUBLANES:, :] = xs_ref[0, 0, rq, c * LANES:(c + 1) * LANES].astype(F32)
    for c in range(bc_ref.shape[3] // LANES):
        slab_ref[xs_slabs + c, SUBLANES:, :] = bc_ref[0, 0, rq, c * LANES:(c + 1) * LANES].astype(F32)

    for g in range(N_SSM_GROUPS):
        bm = _conv_silu(slab_ref, xs_slabs + g, cw_ref, cb_ref)
        cm = _conv_silu(slab_ref, xs_slabs + N_SSM_GROUPS + g, cw_ref, cb_ref)
        bm16 = bm.astype(BF16)
        cb_t = lax.dot_general(bm16, cm.astype(BF16), NT_DIMS, preferred_element_type=F32)
        cm_t = cm.T
        xs_g = jnp.concatenate([_conv_silu(slab_ref, g * slabs_per_group + c, cw_ref, cb_ref)
                                for c in range(slabs_per_group)], axis=1)
        xs_t = xs_g.T
        yield
        y_t = []
        for h in range(HEADS_PER_SSM_GROUP):
            hh = g * HEADS_PER_SSM_GROUP + h
            rows = slice(hh * SSM_HEAD_DIM, (hh + 1) * SSM_HEAD_DIM)
            la_i = la_t[hh:hh + 1, :]
            la_j = jnp.broadcast_to(la[:, hh:hh + 1], (q, q))
            decay_t = jnp.exp2(jnp.where(causal_t, la_i - la_j, -jnp.inf))
            rhs = jnp.concatenate([(cb_t * decay_t).astype(BF16), (cm_t * jnp.exp2(la_i)).astype(BF16)], axis=0)
            x_h = xs_t[h * SSM_HEAD_DIM:(h + 1) * SSM_HEAD_DIM]
            state = state_ref[rows, :]
            lhs = jnp.concatenate([(x_h * dt_t[hh:hh + 1, :]).astype(BF16), state.astype(BF16)], axis=1)
            y_t.append(jnp.dot(lhs, rhs, preferred_element_type=F32))
            yield
        heads = range(g * HEADS_PER_SSM_GROUP, (g + 1) * HEADS_PER_SSM_GROUP)
        grows = slice(heads[0] * SSM_HEAD_DIM, (heads[-1] + 1) * SSM_HEAD_DIM)
        def scale_heads(x, t):
            hp = x.reshape(HEADS_PER_SSM_GROUP, SSM_HEAD_DIM, q) * t[heads[0]:heads[-1] + 1][:, None, :]
            return hp.reshape(x.shape)
        upd = jnp.dot(scale_heads(xs_t, w_t).astype(BF16), bm16, preferred_element_type=F32)
        state_ref[grows, :] = scale_heads(state_ref[grows, :], state_decay) + upd
        yield
        y = jnp.concatenate(y_t, axis=0).T
        cols = slice(g * gw, (g + 1) * gw)
        y = y + xs_g * dskip_ref[:, cols]
        z = z_ref[0, 0, rq, cols].astype(F32)
        yg = y * (z * _sigmoid(z))
        yg = yg * lax.rsqrt(jnp.mean(yg * yg, axis=-1, keepdims=True) + EPS)
        ssm_ref[rq, cols] = (yg * nw_ref[:, cols]).astype(ssm_ref.dtype)
        yield

    slab_ref[:, 0:SUBLANES, :] = slab_ref[:, q:q + SUBLANES, :]


def _merge_ffn_stages(attn_refs, ssm_ref, gate_ref, h_ref, wa_ref, ws_ref, wo_ref, nat_ref,
                      nw_ref, wgu_ref, wd_ref, fw_ref, out_ref, *, ffn_chunk, final_norm):
    tb = h_ref.shape[1]
    n_slabs = GROUP_WIDTH // LANES
    for k, src in enumerate(attn_refs):
        dil = src.shape[1]
        for r in range(dil if dil > 1 else 0):
            for c in range(n_slabs):
                nat_ref[k, c, pl.ds(r, tb // dil, stride=dil), :] = src[0, r, :, c * LANES:(c + 1) * LANES]

    def natural(k, c):
        src = attn_refs[k]
        return src[0, 0, :, c * LANES:(c + 1) * LANES] if src.shape[1] == 1 else nat_ref[k, c]

    slabs = []
    for c in range(n_slabs):
        o = [natural(2 * g, c) for g in range(N_DIL)]
        l = [natural(2 * g + 1, c) for g in range(N_DIL)]
        mx = functools.reduce(jnp.maximum, l)
        e = [jnp.exp(lg - mx) for lg in l]
        num = functools.reduce(jnp.add, [eg * og for eg, og in zip(e, o)])
        slabs.append((num / functools.reduce(jnp.add, e)).astype(BF16))
    attn = jnp.concatenate(slabs, axis=1)
    yield
    a = jnp.dot(attn, wa_ref[...], preferred_element_type=F32)
    yield
    s = jnp.dot(ssm_ref[...], ws_ref[...], preferred_element_type=F32)
    d = a.shape[1]
    gates = _sigmoid(gate_ref[0, 0].astype(F32))
    merged = (gates[:, :d] * a + gates[:, d:] * s).astype(BF16)
    yield
    acc = h_ref[0] + jnp.dot(merged, wo_ref[...], preferred_element_type=F32)
    xn = _rmsnorm(acc, nw_ref[...]).astype(BF16)
    yield
    d_ff = wd_ref.shape[0]
    chunks = [slice(c0, c0 + ffn_chunk) for c0 in range(0, d_ff, ffn_chunk)]
    gate_up = None
    for c, cols in enumerate(chunks + [None]):
        prev_cols, prev_gate_up = (chunks[c - 1], gate_up) if c > 0 else (None, None)
        if cols is not None:
            up_cols = slice(d_ff + cols.start, d_ff + cols.stop)
            gate = jnp.dot(xn, wgu_ref[:, cols], preferred_element_type=F32)
            yield
            gate_up = (gate, jnp.dot(xn, wgu_ref[:, up_cols], preferred_element_type=F32))
            yield
        if prev_gate_up is not None:
            act = (prev_gate_up[0] * _sigmoid(prev_gate_up[0]) * prev_gate_up[1]).astype(BF16)
            acc = acc + jnp.dot(act, wd_ref[prev_cols, :], preferred_element_type=F32)
            yield
    out_ref[0] = _rmsnorm(acc, fw_ref[...]) if final_norm else acc


N_TAIL_WEIGHTS = 13


def _tail_kernel(*refs, blocks_per_seq, ffn_chunk, ssd_stages_per_matmul, final_norm):
    attn_refs = refs[:2 * N_DIL]
    gate_ref, h_ref, xs_ref, bc_ref, z_ref, dt_ref = refs[2 * N_DIL:2 * N_DIL + 6]
    weight_refs = refs[2 * N_DIL + 6:2 * N_DIL + 6 + N_TAIL_WEIGHTS]
    (wa_ref, ws_ref, wo_ref, n2_ref, wgu_ref, wd_ref, fw_ref,
     cw_ref, cb_ref, dtb_ref, alog_ref, dskip_ref, nw_ref) = [r.at[0] for r in weight_refs]
    out_ref, state_ref, slab_ref, ssm_ref, new_ref, nat_ref = refs[2 * N_DIL + 6 + N_TAIL_WEIGHTS:]

    @pl.when(lax.rem(pl.program_id(0), blocks_per_seq) == 0)
    def _():
        state_ref[...] = jnp.zeros_like(state_ref)
        slab_ref[:, 0:SUBLANES, :] = jnp.zeros((slab_ref.shape[0], SUBLANES, LANES), F32)

    @pl.when(pl.program_id(0) == 0)
    def _():
        ssm_ref[...] = jnp.zeros_like(ssm_ref)

    def ssd_stages():
        for r0 in range(0, xs_ref.shape[2], SSD_CHUNK):
            decays = _ssd_decays(r0, dt_ref, dtb_ref, alog_ref)
            yield
            yield from _ssd_chunk_stages(r0, decays, xs_ref, bc_ref, z_ref, cw_ref, cb_ref, dskip_ref, nw_ref,
                                         new_ref, state_ref, slab_ref)

    matmuls = _merge_ffn_stages(attn_refs, ssm_ref, gate_ref, h_ref, wa_ref, ws_ref, wo_ref, nat_ref,
                                n2_ref, wgu_ref, wd_ref, fw_ref, out_ref, ffn_chunk=ffn_chunk, final_norm=final_norm)
    turns = [(matmuls, 1), (ssd_stages(), ssd_stages_per_matmul)]
    live = [gen for gen, _ in turns]
    while live:
        for gen, reps in turns:
            for _ in range(reps if gen in live else 0):
                if next(gen, StopIteration) is StopIteration:
                    live.remove(gen)
                    break
    ssm_ref[...] = new_ref[...]


def _tail(attn_outs, gates, xs, bc, z, dt3, h3, weights, *, layer, tb, ffn_chunk, ssd_stages_per_matmul,
          final_norm):
    b, s, d = h3.shape
    assert len(weights) == N_TAIL_WEIGHTS
    conv_ch = weights[7].shape[-1]
    d_inner = xs.shape[-1]
    n_blocks = s // tb

    def cur(k):
        return jnp.divmod(jnp.minimum(k, b * n_blocks - 1), n_blocks)

    def prev(k):
        return jnp.divmod(jnp.maximum(k - 1, 0), n_blocks)

    def rows3(width, block_index):
        def index_map(k):
            bi, blk = block_index(k)
            return bi, blk, 0
        return pl.BlockSpec((1, tb, width), index_map)

    def rows4(arr, block_index):
        dil = arr.shape[1]

        def index_map(k):
            bi, blk = block_index(k)
            return bi, 0, blk, 0
        return pl.BlockSpec((1, dil, tb // dil, arr.shape[-1]), index_map)

    flat = [a for pair in attn_outs for a in pair]
    in_specs = [rows4(a, prev) for a in flat] + [
        rows4(gates, prev), rows3(d, prev), rows4(xs, cur), rows4(bc, cur), rows4(z, cur), rows3(LANES, cur),
    ] + [_layer_spec(w, min(layer, w.shape[0] - 1)) for w in weights]
    return pl.pallas_call(
        functools.partial(_tail_kernel, blocks_per_seq=n_blocks, ffn_chunk=ffn_chunk,
                          ssd_stages_per_matmul=ssd_stages_per_matmul, final_norm=final_norm),
        grid=(b * n_blocks + 1,),
        in_specs=in_specs,
        out_specs=rows3(d, prev),
        out_shape=jax.ShapeDtypeStruct((b, s, d), F32),
        scratch_shapes=[pltpu.VMEM((N_SSM_GROUPS * HEADS_PER_SSM_GROUP * SSM_HEAD_DIM, D_STATE), F32),
                        pltpu.VMEM((conv_ch // LANES, SUBLANES + SSD_CHUNK, LANES), F32),
                        pltpu.VMEM((tb, d_inner), BF16),
                        pltpu.VMEM((tb, d_inner), BF16),
                        pltpu.VMEM((len(flat), GROUP_WIDTH // LANES, tb, LANES), F32)],
        compiler_params=_compiler_params(("arbitrary",)),
        name="layer_tail",
    )(*flat, gates, h3, xs, bc, z, dt3, *weights)


def kernel(x, norm1_w, w_in, conv_w, conv_b, dt_bias, a_log, d_skip, ssm_norm_w, w_attn_branch, w_ssm_branch,
           w_out, norm2_w, w_ffn_in, w_ffn_out, rel_bias, final_norm_w):
    b, s, d = x.shape
    depth = w_in.shape[0]
    d_inner = ssm_norm_w.shape[1]
    n_heads = dt_bias.shape[1]
    bc_w = 2 * N_SSM_GROUPS * D_STATE
    assert d_inner == N_SSM_GROUPS * HEADS_PER_SSM_GROUP * SSM_HEAD_DIM and n_heads <= LANES
    assert s % (DILATED_GROUPS[-1][1] * ATTN_BLOCK) == 0 and s % SSD_CHUNK == 0 and DILATED_GROUPS[0][1] == 1

    o_z = 3 * ATTN_WIDTH
    o_xbc = o_z + d_inner
    o_dt = o_xbc + d_inner + bc_w
    o_gate = o_dt + n_heads
    assert o_dt % LANES == 0 and o_dt + LANES <= w_in.shape[2]
    w_in16 = w_in.astype(BF16)
    w_gates16 = w_in[:, :, o_gate:].astype(BF16)
    qkv_start = lambda part, g: part * ATTN_WIDTH + g * GROUP_WIDTH
    main_weights = [(w_in16, None, 0), (w_gates16, None, 0)]
    main_outputs = ((1, ((0, o_z, d_inner),)), (1, ((1, 0, 2 * d),)), (1, ((0, o_xbc, d_inner),)),
                    (1, ((0, o_xbc + d_inner, bc_w),)),
                    (1, tuple((0, qkv_start(part, 0), GROUP_WIDTH) for part in range(3))))
    dilated = [(g, dil) for g, (_, dil) in enumerate(DILATED_GROUPS) if dil > 1]
    dilated_weights = [(w_in16, GROUP_WIDTH, qkv_start(part, g) // GROUP_WIDTH) for g, _ in dilated for part in range(3)]
    dilated_outputs = tuple((dil, tuple((3 * n + part, 0, GROUP_WIDTH) for part in range(3)))
                            for n, (_, dil) in enumerate(dilated))

    pad_h = ((0, 0), (0, 0), (0, LANES - n_heads))
    row3 = lambda a: a[:, None, :]
    tail_weights = (w_attn_branch.astype(BF16), w_ssm_branch.astype(BF16), w_out.astype(BF16), row3(norm2_w),
                    w_ffn_in.astype(BF16), w_ffn_out.astype(BF16), final_norm_w[None, None, :],
                    conv_w, row3(conv_b), jnp.pad(row3(dt_bias), pad_h), jnp.pad(row3(a_log), pad_h),
                    row3(jnp.repeat(d_skip, SSM_HEAD_DIM, axis=1)), row3(ssm_norm_w))
    norm1_w3 = row3(norm1_w)
    bias_tables = _attention_bias_tables(rel_bias)

    h = x
    for layer in range(depth):
        z, gates, xs, bc, qkv0, dt = _in_proj(h, norm1_w3, main_weights, main_outputs, (0, o_dt, LANES),
                                              layer=layer, tm=IN_PROJ_ROWS, chunk=IN_PROJ_CHUNK, name="in_proj")
        qkv_dilated = _in_proj(h, norm1_w3, dilated_weights, dilated_outputs, layer=layer,
                               tm=IN_PROJ_DILATED_ROWS, chunk=IN_PROJ_CHUNK, name="in_proj_dilated")
        qkvs = {0: qkv0, **{g: arr for (g, _), arr in zip(dilated, qkv_dilated)}}
        attn_outs = [_attention(qkvs[g], bias_tables[g], col_q=0, col_k=1, col_v=2, rows=ATTN_ROWS)
                     for g in range(N_DIL)]
        h = _tail(attn_outs, gates, xs, bc, z, dt, h, tail_weights, layer=layer, tb=TAIL_ROWS,
                  ffn_chunk=FFN_CHUNK, ssd_stages_per_matmul=SSD_STAGES_PER_MATMUL,
                  final_norm=(layer == depth - 1))
    return h
```

```python
import functools
import math

import jax
import jax.numpy as jnp
from jax import lax
from jax.experimental import pallas as pl
from jax.experimental.pallas import tpu as pltpu

HEAD_DIM = 64
DILATED_GROUPS = ((128, 1), (512, 4), (2048, 16))
N_DIL = len(DILATED_GROUPS)
HEADS_PER_GROUP = 8
GROUP_WIDTH = HEADS_PER_GROUP * HEAD_DIM
ATTN_WIDTH = N_DIL * GROUP_WIDTH
ATTN_BLOCK = 128
N_REL_BUCKETS = 32
REL_MAX_DISTANCE = 2048
SSM_HEAD_DIM = 64
N_SSM_GROUPS = 4
HEADS_PER_SSM_GROUP = 8
D_STATE = 128
CONV_WIDTH = 4
SSD_CHUNK = 128
EPS = 1e-6

LANES = 128
SUBLANES = 8
VMEM_LIMIT_BYTES = 60000 * 1024

IN_PROJ_ROWS = 512
IN_PROJ_DILATED_ROWS = 2048
IN_PROJ_CHUNK = 1024
ATTN_ROWS = 2048
TAIL_ROWS = 256
FFN_CHUNK = 256
SSD_STAGES_PER_MATMUL = 2

BF16 = jnp.bfloat16
F32 = jnp.float32
NT_DIMS = (((1,), (1,)), ((), ()))


def _compiler_params(semantics):
    return pltpu.CompilerParams(dimension_semantics=semantics, vmem_limit_bytes=VMEM_LIMIT_BYTES)


def _rmsnorm(x, w):
    return x * lax.rsqrt(jnp.mean(x * x, axis=-1, keepdims=True) + EPS) * w


def _sigmoid(x):
    return 1.0 / (1.0 + jnp.exp(-x))


PERM_BLOCK = 256


def _in_proj_kernel(x_ref, nw_ref, *rest, outputs, dt_piece, chunk):
    n_w = 1 + max(p[0] for _, pieces in outputs for p in pieces)
    w_refs = [r.at[0] for r in rest[:n_w]]
    out_refs = rest[n_w:n_w + len(outputs)]
    rest = rest[n_w + len(outputs):]
    if dt_piece is not None:
        dt_ref, rest = rest[0], rest[1:]
    dilations = sorted({dil for dil, _ in outputs})
    xn_refs = dict(zip(dilations, rest))
    nw_ref = nw_ref.at[0]
    tm = x_ref.shape[1]
    xn = _rmsnorm(x_ref[0], nw_ref[...]).astype(BF16)
    if dt_piece is not None:
        src, c0, width = dt_piece
        dt_ref[0] = jnp.dot(xn, w_refs[src][:, c0:c0 + width], preferred_element_type=F32)
    for dilation, xn_ref in xn_refs.items():
        if dilation == 1:
            xn_ref[...] = xn
            continue
        per = PERM_BLOCK // dilation
        i = lax.broadcasted_iota(jnp.int32, (PERM_BLOCK, PERM_BLOCK), 0)
        k = lax.broadcasted_iota(jnp.int32, (PERM_BLOCK, PERM_BLOCK), 1)
        src = (i & (per - 1)) * dilation + (i >> (per.bit_length() - 1))
        perm = jnp.where(k == src, 1.0, 0.0).astype(BF16)
        for u in range(tm // PERM_BLOCK):
            y = jnp.dot(perm, xn[u * PERM_BLOCK:(u + 1) * PERM_BLOCK], preferred_element_type=F32).astype(BF16)
            for r in range(dilation):
                dst = r * (tm // dilation) + u * per
                xn_ref[dst:dst + per, :] = y[r * per:(r + 1) * per]
    for out_ref, (dilation, pieces) in zip(out_refs, outputs):
        col = 0
        for src, start, width in pieces:
            for c0 in range(0, width, chunk):
                cw = min(chunk, width - c0)
                res = jnp.dot(xn_refs[dilation][...], w_refs[src][:, start + c0:start + c0 + cw],
                              preferred_element_type=F32)
                out_ref[0, :, :, col:col + cw] = res.astype(out_ref.dtype).reshape(dilation, tm // dilation, cw)
                col += cw


def _layer_spec(arr, layer, block=None, col_block=0):
    shape = (1,) + tuple(arr.shape[1:-1]) + (block or arr.shape[-1],)
    index = (layer,) + (0,) * (arr.ndim - 2) + (col_block,)
    return pl.BlockSpec(shape, lambda *_: index, pipeline_mode=pl.Buffered(1))


def _in_proj(h3, norm_w, weights, outputs, dt_piece=None, *, layer, tm, chunk, name):
    b, s, d = h3.shape
    in_specs = [pl.BlockSpec((1, tm, d), lambda bi, i: (bi, i, 0)), _layer_spec(norm_w, layer)]
    in_specs += [_layer_spec(arr, layer, blk, cb) for arr, blk, cb in weights]
    shapes = [(dil, sum(p[2] for p in pieces)) for dil, pieces in outputs]
    out_specs = [pl.BlockSpec((1, dil, tm // dil, n), lambda bi, i: (bi, 0, i, 0)) for dil, n in shapes]
    out_shape = [jax.ShapeDtypeStruct((b, dil, s // dil, n), BF16) for dil, n in shapes]
    if dt_piece is not None:
        out_specs.append(pl.BlockSpec((1, tm, LANES), lambda bi, i: (bi, i, 0)))
        out_shape.append(jax.ShapeDtypeStruct((b, s, LANES), F32))
    return pl.pallas_call(
        functools.partial(_in_proj_kernel, outputs=outputs, dt_piece=dt_piece, chunk=chunk),
        grid=(b, s // tm),
        in_specs=in_specs,
        out_specs=out_specs,
        out_shape=out_shape,
        scratch_shapes=[pltpu.VMEM((tm, d), BF16) for _ in {dil for dil, _ in outputs}],
        compiler_params=_compiler_params(("parallel", "parallel")),
        name=name,
    )(h3, norm_w, *[arr for arr, _, _ in weights])


def _attention_kernel(q_ref, kp_ref, kc_ref, vp_ref, vc_ref, bias_ref, o_ref, l_ref, kext_ref, vext_ref, *, tq):
    n = pl.program_id(2)
    kext_ref[:, 0:ATTN_BLOCK] = kp_ref[0]
    kext_ref[:, ATTN_BLOCK:] = kc_ref[0]
    vext_ref[:, 0:ATTN_BLOCK] = vp_ref[0]
    vext_ref[:, ATTN_BLOCK:] = vc_ref[0]
    low_half = lax.broadcasted_iota(jnp.int32, (1, LANES), 1) < HEAD_DIM

    for rr, s in ((rr, s) for rr in range(q_ref.shape[1]) for s in range(tq // ATTN_BLOCK)):
        r0 = s * ATTN_BLOCK
        qs = q_ref[0, rr, pl.ds(r0, ATTN_BLOCK), :] * (HEAD_DIM ** -0.5)
        ks = kext_ref[rr, pl.ds(r0, 2 * ATTN_BLOCK), :]
        vs = vext_ref[rr, pl.ds(r0, 2 * ATTN_BLOCK), :]
        bidx = jnp.where(n == 0, 0, 1) if s == 0 else 1
        for p in range(GROUP_WIDTH // LANES):
            cols = slice(p * LANES, (p + 1) * LANES)
            qp, kp, vp = qs[:, cols], ks[:, cols], vs[:, cols]
            q2 = jnp.concatenate([jnp.where(low_half, qp, jnp.zeros_like(qp)),
                                  jnp.where(low_half, jnp.zeros_like(qp), qp)], axis=0)
            bias2 = jnp.concatenate([bias_ref[bidx, 2 * p], bias_ref[bidx, 2 * p + 1]], axis=0)
            sc = lax.dot_general(q2, kp, NT_DIMS, preferred_element_type=F32) + bias2
            m = jnp.max(sc, axis=1, keepdims=True)
            pe = jnp.exp(sc - m).astype(BF16)
            nd = jnp.dot(pe, jnp.concatenate([vp, jnp.ones_like(vp)], axis=1), preferred_element_type=F32)
            num, den = nd[:, :LANES], nd[:, LANES:]
            o2 = num / den
            l2 = m + jnp.log(den)
            outs = [o2[:ATTN_BLOCK], o2[ATTN_BLOCK:]]
            lses = [l2[:ATTN_BLOCK], l2[ATTN_BLOCK:]]
            o_ref[0, rr, pl.ds(r0, ATTN_BLOCK), cols] = jnp.where(low_half, outs[0], outs[1])
            l_ref[0, rr, pl.ds(r0, ATTN_BLOCK), cols] = jnp.where(low_half, lses[0], lses[1])


def _attention(qkv, bias_g, *, col_q, col_k, col_v, rows):
    b, dilation, seg, _ = qkv.shape
    tq = min(rows, seg)
    res = min(dilation, rows // tq)
    blocks_per_tile = tq // ATTN_BLOCK

    def cur(col):
        return pl.BlockSpec((1, res, tq, GROUP_WIDTH), lambda bi, r, n: (bi, r, n, col))

    def prev(col):
        return pl.BlockSpec((1, res, ATTN_BLOCK, GROUP_WIDTH),
                            lambda bi, r, n: (bi, r, jnp.maximum(n * blocks_per_tile - 1, 0), col))

    out_spec = pl.BlockSpec((1, res, tq, GROUP_WIDTH), lambda bi, r, n: (bi, r, n, 0))
    out_sds = jax.ShapeDtypeStruct((b, dilation, seg, GROUP_WIDTH), F32)
    return pl.pallas_call(
        functools.partial(_attention_kernel, tq=tq),
        grid=(b, dilation // res, seg // tq),
        in_specs=[cur(col_q), prev(col_k), cur(col_k), prev(col_v), cur(col_v),
                  pl.BlockSpec(bias_g.shape, lambda bi, r, n: (0, 0, 0, 0))],
        out_specs=[out_spec, out_spec],
        out_shape=[out_sds, out_sds],
        scratch_shapes=[pltpu.VMEM((res, tq + ATTN_BLOCK, GROUP_WIDTH), BF16),
                        pltpu.VMEM((res, tq + ATTN_BLOCK, GROUP_WIDTH), BF16)],
        compiler_params=_compiler_params(("parallel", "parallel", "arbitrary")),
        name=f"attention_d{dilation}",
    )(qkv, qkv, qkv, qkv, qkv, bias_g)


def _t5_causal_bucket(dist):
    max_exact = N_REL_BUCKETS // 2
    d_f = jnp.maximum(dist, 1).astype(F32)
    large = max_exact + (jnp.log(d_f / max_exact) / math.log(REL_MAX_DISTANCE / max_exact)
                         * (N_REL_BUCKETS - max_exact)).astype(jnp.int32)
    large = jnp.minimum(large, N_REL_BUCKETS - 1)
    return jnp.where(dist < max_exact, dist, large)


def _attention_bias_tables(rel_bias):
    q, q2 = ATTN_BLOCK, 2 * ATTN_BLOCK
    steps = jnp.arange(q2) - (q - 1)
    in_prev = (jnp.arange(q2) < q)[None, None, :]
    tables = []
    for g, (window, dil) in enumerate(DILATED_GROUPS):
        n_steps = window // dil
        assert n_steps <= q
        rel_g = rel_bias[:, g * HEADS_PER_GROUP:(g + 1) * HEADS_PER_GROUP].astype(F32)
        vals = rel_g[_t5_causal_bucket(jnp.clip(steps, 0, n_steps) * dil)]
        vec = jnp.where(((steps >= 0) & (steps <= n_steps))[:, None], vals, -jnp.inf).T
        skew = jnp.tile(vec, (1, q + 1))[:, :q * (q2 + 1)].reshape(HEADS_PER_GROUP, q, q2 + 1)[:, :, :q2]
        rest = skew[:, :, ::-1]
        first = jnp.where(in_prev, -jnp.inf, rest)
        tables.append(jnp.stack([first, rest]))
    return jnp.stack(tables)


LOG2E = math.log2(math.e)


def _conv_silu(slab_ref, c, w_ref, b_ref):
    q = slab_ref.shape[1] - SUBLANES
    cols = slice(c * LANES, (c + 1) * LANES)
    acc = b_ref[:, cols] + slab_ref[c, pl.ds(SUBLANES, q), :] * w_ref[CONV_WIDTH - 1:CONV_WIDTH, cols]
    for s in range(1, CONV_WIDTH):
        acc = acc + slab_ref[c, pl.ds(SUBLANES - s, q), :] * w_ref[CONV_WIDTH - 1 - s:CONV_WIDTH - s, cols]
    return acc * _sigmoid(acc)


def _ssd_decays(r0, dt_ref, dtb_ref, alog_ref):
    q = SSD_CHUNK
    x_dt = dt_ref[0, r0:r0 + q, :] + dtb_ref[...]
    dt = jnp.maximum(x_dt, 0.0) + jnp.log(1.0 + jnp.exp(-jnp.abs(x_dt)))
    d_a = dt * (-jnp.exp(alog_ref[...]))
    ri = lax.broadcasted_iota(jnp.int32, (q, q), 0)
    ci = lax.broadcasted_iota(jnp.int32, (q, q), 1)
    causal_t = ri <= ci
    tril = jnp.where(ri >= ci, 1.0, 0.0).astype(BF16)
    d1 = d_a.astype(BF16)
    r1 = d_a - d1.astype(F32)
    d2 = r1.astype(BF16)
    d3 = (r1 - d2.astype(F32)).astype(BF16)
    la = (jnp.dot(tril, d1, preferred_element_type=F32) + jnp.dot(tril, d2, preferred_element_type=F32)
          + jnp.dot(tril, d3, preferred_element_type=F32))
    la = la * LOG2E
    la_t = la.T
    dt_t = dt.T
    last = jnp.broadcast_to(la_t[:, q - 1:q], (q, q))
    w_t = jnp.exp2(last - la_t) * dt_t
    state_decay = jnp.exp2(last)
    return causal_t, la, la_t, dt_t, w_t, state_decay


def _ssd_chunk_stages(r0, decays, xs_ref, bc_ref, z_ref, cw_ref, cb_ref, dskip_ref, nw_ref,
                      ssm_ref, state_ref, slab_ref):
    causal_t, la, la_t, dt_t, w_t, state_decay = decays
    q = SSD_CHUNK
    rq = slice(r0, r0 + q)
    gw = HEADS_PER_SSM_GROUP * SSM_HEAD_DIM
    xs_slabs = xs_ref.shape[3] // LANES
    slabs_per_group = gw // LANES

    for c in range(xs_slabs):
        slab_ref[c, SUBLANES:, :] = xs_ref[0, 0, rq, c * LANES:(c + 1) * LANES].astype(F32)
    for c in range(bc_ref.shape[3] // LANES):
        slab_ref[xs_slabs + c, SUBLANES:, :] = bc_ref[0, 0, rq, c * LANES:(c + 1) * LANES].astype(F32)

    for g in range(N_SSM_GROUPS):
        bm = _conv_silu(slab_ref, xs_slabs + g, cw_ref, cb_ref)
        cm = _conv_silu(slab_ref, xs_slabs + N_SSM_GROUPS + g, cw_ref, cb_ref)
        bm16 = bm.astype(BF16)
        cb_t = lax.dot_general(bm16, cm.astype(BF16), NT_DIMS, preferred_element_type=F32)
        cm_t = cm.T
        xs_g = jnp.concatenate([_conv_silu(slab_ref, g * slabs_per_group + c, cw_ref, cb_ref)
                                for c in range(slabs_per_group)], axis=1)
        xs_t = xs_g.T
        yield
        y_t = []
        for h in range(HEADS_PER_SSM_GROUP):
            hh = g * HEADS_PER_SSM_GROUP + h
            rows = slice(hh * SSM_HEAD_DIM, (hh + 1) * SSM_HEAD_DIM)
            la_i = la_t[hh:hh + 1, :]
            la_j = jnp.broadcast_to(la[:, hh:hh + 1], (q, q))
            decay_t = jnp.exp2(jnp.where(causal_t, la_i - la_j, -jnp.inf))
            rhs = jnp.concatenate([(cb_t * decay_t).astype(BF16), (cm_t * jnp.exp2(la_i)).astype(BF16)], axis=0)
            x_h = xs_t[h * SSM_HEAD_DIM:(h + 1) * SSM_HEAD_DIM]
            state = state_ref[rows, :]
            lhs = jnp.concatenate([(x_h * dt_t[hh:hh + 1, :]).astype(BF16), state.astype(BF16)], axis=1)
            y_t.append(jnp.dot(lhs, rhs, preferred_element_type=F32))
            yield
        heads = range(g * HEADS_PER_SSM_GROUP, (g + 1) * HEADS_PER_SSM_GROUP)
        grows = slice(heads[0] * SSM_HEAD_DIM, (heads[-1] + 1) * SSM_HEAD_DIM)
        def scale_heads(x, t):
            hp = x.reshape(HEADS_PER_SSM_GROUP, SSM_HEAD_DIM, q) * t[heads[0]:heads[-1] + 1][:, None, :]
            return hp.reshape(x.shape)
        upd = jnp.dot(scale_heads(xs_t, w_t).astype(BF16), bm16, preferred_element_type=F32)
        state_ref[grows, :] = scale_heads(state_ref[grows, :], state_decay) + upd
        yield
        y = jnp.concatenate(y_t, axis=0).T
        cols = slice(g * gw, (g + 1) * gw)
        y = y + xs_g * dskip_ref[:, cols]
        z = z_ref[0, 0, rq, cols].astype(F32)
        yg = y * (z * _sigmoid(z))
        yg = yg * lax.rsqrt(jnp.mean(yg * yg, axis=-1, keepdims=True) + EPS)
        ssm_ref[rq, cols] = (yg * nw_ref[:, cols]).astype(ssm_ref.dtype)
        yield

    slab_ref[:, 0:SUBLANES, :] = slab_ref[:, q:q + SUBLANES, :]


def _merge_ffn_stages(attn_refs, ssm_ref, gate_ref, h_ref, wa_ref, ws_ref, wo_ref, nat_ref,
                      nw_ref, wgu_ref, wd_ref, fw_ref, out_ref, *, ffn_chunk, final_norm):
    tb = h_ref.shape[1]
    n_slabs = GROUP_WIDTH // LANES
    for k, src in enumerate(attn_refs):
        dil = src.shape[1]
        for r in range(dil if dil > 1 else 0):
            for c in range(n_slabs):
                nat_ref[k, c, pl.ds(r, tb // dil, stride=dil), :] = src[0, r, :, c * LANES:(c + 1) * LANES]

    def natural(k, c):
        src = attn_refs[k]
        return src[0, 0, :, c * LANES:(c + 1) * LANES] if src.shape[1] == 1 else nat_ref[k, c]

    slabs = []
    for c in range(n_slabs):
        o = [natural(2 * g, c) for g in range(N_DIL)]
        l = [natural(2 * g + 1, c) for g in range(N_DIL)]
        mx = functools.reduce(jnp.maximum, l)
        e = [jnp.exp(lg - mx) for lg in l]
        num = functools.reduce(jnp.add, [eg * og for eg, og in zip(e, o)])
        slabs.append((num / functools.reduce(jnp.add, e)).astype(BF16))
    attn = jnp.concatenate(slabs, axis=1)
    yield
    a = jnp.dot(attn, wa_ref[...], preferred_element_type=F32)
    yield
    s = jnp.dot(ssm_ref[...], ws_ref[...], preferred_element_type=F32)
    d = a.shape[1]
    gates = _sigmoid(gate_ref[0, 0].astype(F32))
    merged = (gates[:, :d] * a + gates[:, d:] * s).astype(BF16)
    yield
    acc = h_ref[0] + jnp.dot(merged, wo_ref[...], preferred_element_type=F32)
    xn = _rmsnorm(acc, nw_ref[...]).astype(BF16)
    yield
    d_ff = wd_ref.shape[0]
    chunks = [slice(c0, c0 + ffn_chunk) for c0 in range(0, d_ff, ffn_chunk)]
    gate_up = None
    for c, cols in enumerate(chunks + [None]):
        prev_cols, prev_gate_up = (chunks[c - 1], gate_up) if c > 0 else (None, None)
        if cols is not None:
            up_cols = slice(d_ff + cols.start, d_ff + cols.stop)
            gate = jnp.dot(xn, wgu_ref[:, cols], preferred_element_type=F32)
            yield
            gate_up = (gate, jnp.dot(xn, wgu_ref[:, up_cols], preferred_element_type=F32))
            yield
        if prev_gate_up is not None:
            act = (prev_gate_up[0] * _sigmoid(prev_gate_up[0]) * prev_gate_up[1]).astype(BF16)
            acc = acc + jnp.dot(act, wd_ref[prev_cols, :], preferred_element_type=F32)
            yield
    out_ref[0] = _rmsnorm(acc, fw_ref[...]) if final_norm else acc


N_TAIL_WEIGHTS = 13


def _tail_kernel(*refs, blocks_per_seq, ffn_chunk, ssd_stages_per_matmul, final_norm):
    attn_refs = refs[:2 * N_DIL]
    gate_ref, h_ref, xs_ref, bc_ref, z_ref, dt_ref = refs[2 * N_DIL:2 * N_DIL + 6]
    weight_refs = refs[2 * N_DIL + 6:2 * N_DIL + 6 + N_TAIL_WEIGHTS]
    (wa_ref, ws_ref, wo_ref, n2_ref, wgu_ref, wd_ref, fw_ref,
     cw_ref, cb_ref, dtb_ref, alog_ref, dskip_ref, nw_ref) = [r.at[0] for r in weight_refs]
    out_ref, state_ref, slab_ref, ssm_ref, new_ref, nat_ref = refs[2 * N_DIL + 6 + N_TAIL_WEIGHTS:]

    @pl.when(lax.rem(pl.program_id(0), blocks_per_seq) == 0)
    def _():
        state_ref[...] = jnp.zeros_like(state_ref)
        slab_ref[:, 0:SUBLANES, :] = jnp.zeros((slab_ref.shape[0], SUBLANES, LANES), F32)

    @pl.when(pl.program_id(0) == 0)
    def _():
        ssm_ref[...] = jnp.zeros_like(ssm_ref)

    def ssd_stages():
        for r0 in range(0, xs_ref.shape[2], SSD_CHUNK):
            decays = _ssd_decays(r0, dt_ref, dtb_ref, alog_ref)
            yield
            yield from _ssd_chunk_stages(r0, decays, xs_ref, bc_ref, z_ref, cw_ref, cb_ref, dskip_ref, nw_ref,
                                         new_ref, state_ref, slab_ref)

    matmuls = _merge_ffn_stages(attn_refs, ssm_ref, gate_ref, h_ref, wa_ref, ws_ref, wo_ref, nat_ref,
                                n2_ref, wgu_ref, wd_ref, fw_ref, out_ref, ffn_chunk=ffn_chunk, final_norm=final_norm)
    turns = [(matmuls, 1), (ssd_stages(), ssd_stages_per_matmul)]
    live = [gen for gen, _ in turns]
    while live:
        for gen, reps in turns:
            for _ in range(reps if gen in live else 0):
                if next(gen, StopIteration) is StopIteration:
                    live.remove(gen)
                    break
    ssm_ref[...] = new_ref[...]


def _tail(attn_outs, gates, xs, bc, z, dt3, h3, weights, *, layer, tb, ffn_chunk, ssd_stages_per_matmul,
          final_norm):
    b, s, d = h3.shape
    assert len(weights) == N_TAIL_WEIGHTS
    conv_ch = weights[7].shape[-1]
    d_inner = xs.shape[-1]
    n_blocks = s // tb

    def cur(k):
        return jnp.divmod(jnp.minimum(k, b * n_blocks - 1), n_blocks)

    def prev(k):
        return jnp.divmod(jnp.maximum(k - 1, 0), n_blocks)

    def rows3(width, block_index):
        def index_map(k):
            bi, blk = block_index(k)
            return bi, blk, 0
        return pl.BlockSpec((1, tb, width), index_map)

    def rows4(arr, block_index):
        dil = arr.shape[1]

        def index_map(k):
            bi, blk = block_index(k)
            return bi, 0, blk, 0
        return pl.BlockSpec((1, dil, tb // dil, arr.shape[-1]), index_map)

    flat = [a for pair in attn_outs for a in pair]
    in_specs = [rows4(a, prev) for a in flat] + [
        rows4(gates, prev), rows3(d, prev), rows4(xs, cur), rows4(bc, cur), rows4(z, cur), rows3(LANES, cur),
    ] + [_layer_spec(w, min(layer, w.shape[0] - 1)) for w in weights]
    return pl.pallas_call(
        functools.partial(_tail_kernel, blocks_per_seq=n_blocks, ffn_chunk=ffn_chunk,
                          ssd_stages_per_matmul=ssd_stages_per_matmul, final_norm=final_norm),
        grid=(b * n_blocks + 1,),
        in_specs=in_specs,
        out_specs=rows3(d, prev),
        out_shape=jax.ShapeDtypeStruct((b, s, d), F32),
        scratch_shapes=[pltpu.VMEM((N_SSM_GROUPS * HEADS_PER_SSM_GROUP * SSM_HEAD_DIM, D_STATE), F32),
                        pltpu.VMEM((conv_ch // LANES, SUBLANES + SSD_CHUNK, LANES), F32),
                        pltpu.VMEM((tb, d_inner), BF16),
                        pltpu.VMEM((tb, d_inner), BF16),
                        pltpu.VMEM((len(flat), GROUP_WIDTH // LANES, tb, LANES), F32)],
        compiler_params=_compiler_params(("arbitrary",)),
        name="layer_tail",
    )(*flat, gates, h3, xs, bc, z, dt3, *weights)


def kernel(x, norm1_w, w_in, conv_w, conv_b, dt_bias, a_log, d_skip, ssm_norm_w, w_attn_branch, w_ssm_branch,
           w_out, norm2_w, w_ffn_in, w_ffn_out, rel_bias, final_norm_w):
    b, s, d = x.shape
    depth = w_in.shape[0]
    d_inner = ssm_norm_w.shape[1]
    n_heads = dt_bias.shape[1]
    bc_w = 2 * N_SSM_GROUPS * D_STATE
    assert d_inner == N_SSM_GROUPS * HEADS_PER_SSM_GROUP * SSM_HEAD_DIM and n_heads <= LANES
    assert s % (DILATED_GROUPS[-1][1] * ATTN_BLOCK) == 0 and s % SSD_CHUNK == 0 and DILATED_GROUPS[0][1] == 1

    o_z = 3 * ATTN_WIDTH
    o_xbc = o_z + d_inner
    o_dt = o_xbc + d_inner + bc_w
    o_gate = o_dt + n_heads
    assert o_dt % LANES == 0 and o_dt + LANES <= w_in.shape[2]
    w_in16 = w_in.astype(BF16)
    w_gates16 = w_in[:, :, o_gate:].astype(BF16)
    qkv_start = lambda part, g: part * ATTN_WIDTH + g * GROUP_WIDTH
    main_weights = [(w_in16, None, 0), (w_gates16, None, 0)]
    main_outputs = ((1, ((0, o_z, d_inner),)), (1, ((1, 0, 2 * d),)), (1, ((0, o_xbc, d_inner),)),
                    (1, ((0, o_xbc + d_inner, bc_w),)),
                    (1, tuple((0, qkv_start(part, 0), GROUP_WIDTH) for part in range(3))))
    dilated = [(g, dil) for g, (_, dil) in enumerate(DILATED_GROUPS) if dil > 1]
    dilated_weights = [(w_in16, GROUP_WIDTH, qkv_start(part, g) // GROUP_WIDTH) for g, _ in dilated for part in range(3)]
    dilated_outputs = tuple((dil, tuple((3 * n + part, 0, GROUP_WIDTH) for part in range(3)))
                            for n, (_, dil) in enumerate(dilated))

    pad_h = ((0, 0), (0, 0), (0, LANES - n_heads))
    row3 = lambda a: a[:, None, :]
    tail_weights = (w_attn_branch.astype(BF16), w_ssm_branch.astype(BF16), w_out.astype(BF16), row3(norm2_w),
                    w_ffn_in.astype(BF16), w_ffn_out.astype(BF16), final_norm_w[None, None, :],
                    conv_w, row3(conv_b), jnp.pad(row3(dt_bias), pad_h), jnp.pad(row3(a_log), pad_h),
                    row3(jnp.repeat(d_skip, SSM_HEAD_DIM, axis=1)), row3(ssm_norm_w))
    norm1_w3 = row3(norm1_w)
    bias_tables = _attention_bias_tables(rel_bias)

    h = x
    for layer in range(depth):
        z, gates, xs, bc, qkv0, dt = _in_proj(h, norm1_w3, main_weights, main_outputs, (0, o_dt, LANES),
                                              layer=layer, tm=IN_PROJ_ROWS, chunk=IN_PROJ_CHUNK, name="in_proj")
        qkv_dilated = _in_proj(h, norm1_w3, dilated_weights, dilated_outputs, layer=layer,
                               tm=IN_PROJ_DILATED_ROWS, chunk=IN_PROJ_CHUNK, name="in_proj_dilated")
        qkvs = {0: qkv0, **{g: arr for (g, _), arr in zip(dilated, qkv_dilated)}}
        attn_outs = [_attention(qkvs[g], bias_tables[g], col_q=0, col_k=1, col_v=2, rows=ATTN_ROWS)
                     for g in range(N_DIL)]
        h = _tail(attn_outs, gates, xs, bc, z, dt, h, tail_weights, layer=layer, tb=TAIL_ROWS,
                  ffn_chunk=FFN_CHUNK, ssd_stages_per_matmul=SSD_STAGES_PER_MATMUL,
                  final_norm=(layer == depth - 1))
    return h
```

```python
import functools
import math

import jax
import jax.numpy as jnp
from jax import lax
from jax.experimental import pallas as pl
from jax.experimental.pallas import tpu as pltpu

HEAD_DIM = 64
DILATED_GROUPS = ((128, 1), (512, 4), (2048, 16))
N_DIL = len(DILATED_GROUPS)
HEADS_PER_GROUP = 8
GROUP_WIDTH = HEADS_PER_GROUP * HEAD_DIM
ATTN_WIDTH = N_DIL * GROUP_WIDTH
ATTN_BLOCK = 128
N_REL_BUCKETS = 32
REL_MAX_DISTANCE = 2048
SSM_HEAD_DIM = 64
N_SSM_GROUPS = 4
HEADS_PER_SSM_GROUP = 8
D_STATE = 128
CONV_WIDTH = 4
SSD_CHUNK = 128
EPS = 1e-6

LANES = 128
SUBLANES = 8
VMEM_LIMIT_BYTES = 60000 * 1024

IN_PROJ_ROWS = 512
IN_PROJ_DILATED_ROWS = 1024
IN_PROJ_CHUNK = 1024
ATTN_ROWS = 2048
TAIL_ROWS = 256
FFN_CHUNK = 256
SSD_STAGES_PER_MATMUL = 1
SSD_HEADS_PER_STAGE = 4

BF16 = jnp.bfloat16
F32 = jnp.float32
NT_DIMS = (((1,), (1,)), ((), ()))


def _compiler_params(semantics):
    return pltpu.CompilerParams(dimension_semantics=semantics, vmem_limit_bytes=VMEM_LIMIT_BYTES)


def _rmsnorm(x, w):
    return x * lax.rsqrt(jnp.mean(x * x, axis=-1, keepdims=True) + EPS) * w


def _sigmoid(x):
    return 1.0 / (1.0 + jnp.exp(-x))


PERM_BLOCK = 256


def _in_proj_kernel(x_ref, nw_ref, *rest, outputs, dt_piece, chunk):
    n_w = 1 + max(p[0] for _, pieces in outputs for p in pieces)
    w_refs = [r.at[0] for r in rest[:n_w]]
    out_refs = rest[n_w:n_w + len(outputs)]
    rest = rest[n_w + len(outputs):]
    if dt_piece is not None:
        dt_ref, rest = rest[0], rest[1:]
    dilations = sorted({dil for dil, _ in outputs})
    xn_refs = dict(zip(dilations, rest))
    nw_ref = nw_ref.at[0]
    tm = x_ref.shape[1]
    xn = _rmsnorm(x_ref[0], nw_ref[...]).astype(BF16)
    if dt_piece is not None:
        src, c0, width = dt_piece
        dt_ref[0] = jnp.dot(xn, w_refs[src][:, c0:c0 + width], preferred_element_type=F32)
    for dilation, xn_ref in xn_refs.items():
        if dilation == 1:
            xn_ref[...] = xn
            continue
        per = PERM_BLOCK // dilation
        i = lax.broadcasted_iota(jnp.int32, (PERM_BLOCK, PERM_BLOCK), 0)
        k = lax.broadcasted_iota(jnp.int32, (PERM_BLOCK, PERM_BLOCK), 1)
        src = (i & (per - 1)) * dilation + (i >> (per.bit_length() - 1))
        perm = jnp.where(k == src, 1.0, 0.0).astype(BF16)
        for u in range(tm // PERM_BLOCK):
            y = jnp.dot(perm, xn[u * PERM_BLOCK:(u + 1) * PERM_BLOCK], preferred_element_type=F32).astype(BF16)
            for r in range(dilation):
                dst = r * (tm // dilation) + u * per
                xn_ref[dst:dst + per, :] = y[r * per:(r + 1) * per]
    for out_ref, (dilation, pieces) in zip(out_refs, outputs):
        col = 0
        for src, start, width in pieces:
            for c0 in range(0, width, chunk):
                cw = min(chunk, width - c0)
                res = jnp.dot(xn_refs[dilation][...], w_refs[src][:, start + c0:start + c0 + cw],
                              preferred_element_type=F32)
                out_ref[0, :, :, col:col + cw] = res.astype(out_ref.dtype).reshape(dilation, tm // dilation, cw)
                col += cw


def _layer_spec(arr, layer, block=None, col_block=0):
    shape = (1,) + tuple(arr.shape[1:-1]) + (block or arr.shape[-1],)
    index = (layer,) + (0,) * (arr.ndim - 2) + (col_block,)
    return pl.BlockSpec(shape, lambda *_: index, pipeline_mode=pl.Buffered(1))


def _in_proj(h3, norm_w, weights, outputs, dt_piece=None, *, layer, tm, chunk, name):
    b, s, d = h3.shape
    in_specs = [pl.BlockSpec((1, tm, d), lambda bi, i: (bi, i, 0)), _layer_spec(norm_w, layer)]
    in_specs += [_layer_spec(arr, layer, blk, cb) for arr, blk, cb in weights]
    shapes = [(dil, sum(p[2] for p in pieces)) for dil, pieces in outputs]
    out_specs = [pl.BlockSpec((1, dil, tm // dil, n), lambda bi, i: (bi, 0, i, 0)) for dil, n in shapes]
    out_shape = [jax.ShapeDtypeStruct((b, dil, s // dil, n), BF16) for dil, n in shapes]
    if dt_piece is not None:
        out_specs.append(pl.BlockSpec((1, tm, LANES), lambda bi, i: (bi, i, 0)))
        out_shape.append(jax.ShapeDtypeStruct((b, s, LANES), F32))
    return pl.pallas_call(
        functools.partial(_in_proj_kernel, outputs=outputs, dt_piece=dt_piece, chunk=chunk),
        grid=(b, s // tm),
        in_specs=in_specs,
        out_specs=out_specs,
        out_shape=out_shape,
        scratch_shapes=[pltpu.VMEM((tm, d), BF16) for _ in {dil for dil, _ in outputs}],
        compiler_params=_compiler_params(("parallel", "parallel")),
        name=name,
    )(h3, norm_w, *[arr for arr, _, _ in weights])


def _attention_kernel(q_ref, kp_ref, kc_ref, vp_ref, vc_ref, bias_ref, o_ref, l_ref, kext_ref, vext_ref, *, tq):
    n = pl.program_id(2)
    kext_ref[:, 0:ATTN_BLOCK] = kp_ref[0]
    kext_ref[:, ATTN_BLOCK:] = kc_ref[0]
    vext_ref[:, 0:ATTN_BLOCK] = vp_ref[0]
    vext_ref[:, ATTN_BLOCK:] = vc_ref[0]
    low_half = lax.broadcasted_iota(jnp.int32, (1, LANES), 1) < HEAD_DIM

    for rr, s in ((rr, s) for rr in range(q_ref.shape[1]) for s in range(tq // ATTN_BLOCK)):
        r0 = s * ATTN_BLOCK
        qs = q_ref[0, rr, pl.ds(r0, ATTN_BLOCK), :] * (HEAD_DIM ** -0.5)
        ks = kext_ref[rr, pl.ds(r0, 2 * ATTN_BLOCK), :]
        vs = vext_ref[rr, pl.ds(r0, 2 * ATTN_BLOCK), :]
        bidx = jnp.where(n == 0, 0, 1) if s == 0 else 1
        for p in range(GROUP_WIDTH // LANES):
            cols = slice(p * LANES, (p + 1) * LANES)
            qp, kp, vp = qs[:, cols], ks[:, cols], vs[:, cols]
            q2 = jnp.concatenate([jnp.where(low_half, qp, jnp.zeros_like(qp)),
                                  jnp.where(low_half, jnp.zeros_like(qp), qp)], axis=0)
            bias2 = jnp.concatenate([bias_ref[bidx, 2 * p], bias_ref[bidx, 2 * p + 1]], axis=0)
            sc = lax.dot_general(q2, kp, NT_DIMS, preferred_element_type=F32) + bias2
            m = jnp.max(sc, axis=1, keepdims=True)
            pe = jnp.exp(sc - m).astype(BF16)
            nd = jnp.dot(pe, jnp.concatenate([vp, jnp.ones_like(vp)], axis=1), preferred_element_type=F32)
            num, den = nd[:, :LANES], nd[:, LANES:]
            o2 = num / den
            l2 = m + jnp.log(den)
            outs = [o2[:ATTN_BLOCK], o2[ATTN_BLOCK:]]
            lses = [l2[:ATTN_BLOCK], l2[ATTN_BLOCK:]]
            o_ref[0, rr, pl.ds(r0, ATTN_BLOCK), cols] = jnp.where(low_half, outs[0], outs[1])
            l_ref[0, rr, pl.ds(r0, ATTN_BLOCK), cols] = jnp.where(low_half, lses[0], lses[1])


def _attention(qkv, bias_g, *, col_q, col_k, col_v, rows):
    b, dilation, seg, _ = qkv.shape
    tq = min(rows, seg)
    res = min(dilation, rows // tq)
    blocks_per_tile = tq // ATTN_BLOCK

    def cur(col):
        return pl.BlockSpec((1, res, tq, GROUP_WIDTH), lambda bi, r, n: (bi, r, n, col))

    def prev(col):
        return pl.BlockSpec((1, res, ATTN_BLOCK, GROUP_WIDTH),
                            lambda bi, r, n: (bi, r, jnp.maximum(n * blocks_per_tile - 1, 0), col))

    out_spec = pl.BlockSpec((1, res, tq, GROUP_WIDTH), lambda bi, r, n: (bi, r, n, 0))
    out_sds = jax.ShapeDtypeStruct((b, dilation, seg, GROUP_WIDTH), F32)
    return pl.pallas_call(
        functools.partial(_attention_kernel, tq=tq),
        grid=(b, dilation // res, seg // tq),
        in_specs=[cur(col_q), prev(col_k), cur(col_k), prev(col_v), cur(col_v),
                  pl.BlockSpec(bias_g.shape, lambda bi, r, n: (0, 0, 0, 0))],
        out_specs=[out_spec, out_spec],
        out_shape=[out_sds, out_sds],
        scratch_shapes=[pltpu.VMEM((res, tq + ATTN_BLOCK, GROUP_WIDTH), BF16),
                        pltpu.VMEM((res, tq + ATTN_BLOCK, GROUP_WIDTH), BF16)],
        compiler_params=_compiler_params(("parallel", "parallel", "arbitrary")),
        name=f"attention_d{dilation}",
    )(qkv, qkv, qkv, qkv, qkv, bias_g)


def _t5_causal_bucket(dist):
    max_exact = N_REL_BUCKETS // 2
    d_f = jnp.maximum(dist, 1).astype(F32)
    large = max_exact + (jnp.log(d_f / max_exact) / math.log(REL_MAX_DISTANCE / max_exact)
                         * (N_REL_BUCKETS - max_exact)).astype(jnp.int32)
    large = jnp.minimum(large, N_REL_BUCKETS - 1)
    return jnp.where(dist < max_exact, dist, large)


def _attention_bias_tables(rel_bias):
    q, q2 = ATTN_BLOCK, 2 * ATTN_BLOCK
    steps = jnp.arange(q2) - (q - 1)
    in_prev = (jnp.arange(q2) < q)[None, None, :]
    tables = []
    for g, (window, dil) in enumerate(DILATED_GROUPS):
        n_steps = window // dil
        assert n_steps <= q
        rel_g = rel_bias[:, g * HEADS_PER_GROUP:(g + 1) * HEADS_PER_GROUP].astype(F32)
        vals = rel_g[_t5_causal_bucket(jnp.clip(steps, 0, n_steps) * dil)]
        vec = jnp.where(((steps >= 0) & (steps <= n_steps))[:, None], vals, -jnp.inf).T
        skew = jnp.tile(vec, (1, q + 1))[:, :q * (q2 + 1)].reshape(HEADS_PER_GROUP, q, q2 + 1)[:, :, :q2]
        rest = skew[:, :, ::-1]
        first = jnp.where(in_prev, -jnp.inf, rest)
        tables.append(jnp.stack([first, rest]))
    return jnp.stack(tables)


LOG2E = math.log2(math.e)


def _conv_silu(slab_ref, c, w_ref, b_ref):
    q = slab_ref.shape[1] - SUBLANES
    cols = slice(c * LANES, (c + 1) * LANES)
    acc = b_ref[:, cols] + slab_ref[c, pl.ds(SUBLANES, q), :] * w_ref[CONV_WIDTH - 1:CONV_WIDTH, cols]
    for s in range(1, CONV_WIDTH):
        acc = acc + slab_ref[c, pl.ds(SUBLANES - s, q), :] * w_ref[CONV_WIDTH - 1 - s:CONV_WIDTH - s, cols]
    return acc * _sigmoid(acc)


def _ssd_decays(r0, dt_ref, dtb_ref, alog_ref):
    q = SSD_CHUNK
    x_dt = dt_ref[0, r0:r0 + q, :] + dtb_ref[...]
    dt = jnp.maximum(x_dt, 0.0) + jnp.log(1.0 + jnp.exp(-jnp.abs(x_dt)))
    d_a = dt * (-jnp.exp(alog_ref[...]))
    ri = lax.broadcasted_iota(jnp.int32, (q, q), 0)
    ci = lax.broadcasted_iota(jnp.int32, (q, q), 1)
    causal_t = ri <= ci
    tril = jnp.where(ri >= ci, 1.0, 0.0).astype(BF16)
    d1 = d_a.astype(BF16)
    r1 = d_a - d1.astype(F32)
    d2 = r1.astype(BF16)
    d3 = (r1 - d2.astype(F32)).astype(BF16)
    la = (jnp.dot(tril, d1, preferred_element_type=F32) + jnp.dot(tril, d2, preferred_element_type=F32)
          + jnp.dot(tril, d3, preferred_element_type=F32))
    la = la * LOG2E
    la_t = la.T
    dt_t = dt.T
    last = jnp.broadcast_to(la_t[:, q - 1:q], (q, q))
    w_t = jnp.exp2(last - la_t) * dt_t
    state_decay = jnp.exp2(last)
    return causal_t, la, la_t, dt_t, w_t, state_decay


def _ssd_chunk_stages(r0, decays, xs_ref, bc_ref, z_ref, cw_ref, cb_ref, dskip_ref, nw_ref,
                      ssm_ref, state_ref, slab_ref):
    causal_t, la, la_t, dt_t, w_t, state_decay = decays
    q = SSD_CHUNK
    rq = slice(r0, r0 + q)
    gw = HEADS_PER_SSM_GROUP * SSM_HEAD_DIM
    xs_slabs = xs_ref.shape[3] // LANES
    slabs_per_group = gw // LANES

    for c in range(xs_slabs):
        slab_ref[c, SUBLANES:, :] = xs_ref[0, 0, rq, c * LANES:(c + 1) * LANES].astype(F32)
    for c in range(bc_ref.shape[3] // LANES):
        slab_ref[xs_slabs + c, SUBLANES:, :] = bc_ref[0, 0, rq, c * LANES:(c + 1) * LANES].astype(F32)

    for g in range(N_SSM_GROUPS):
        bm = _conv_silu(slab_ref, xs_slabs + g, cw_ref, cb_ref)
        cm = _conv_silu(slab_ref, xs_slabs + N_SSM_GROUPS + g, cw_ref, cb_ref)
        bm16 = bm.astype(BF16)
        cb_t = lax.dot_general(bm16, cm.astype(BF16), NT_DIMS, preferred_element_type=F32)
        cm_t = cm.T
        xs_g = jnp.concatenate([_conv_silu(slab_ref, g * slabs_per_group + c, cw_ref, cb_ref)
                                for c in range(slabs_per_group)], axis=1)
        xs_t = xs_g.T
        yield
        y_t = []
        for h in range(HEADS_PER_SSM_GROUP):
            hh = g * HEADS_PER_SSM_GROUP + h
            rows = slice(hh * SSM_HEAD_DIM, (hh + 1) * SSM_HEAD_DIM)
            la_i = la_t[hh:hh + 1, :]
            la_j = jnp.broadcast_to(la[:, hh:hh + 1], (q, q))
            decay_t = jnp.exp2(jnp.where(causal_t, la_i - la_j, -jnp.inf))
            rhs = jnp.concatenate([(cb_t * decay_t).astype(BF16), (cm_t * jnp.exp2(la_i)).astype(BF16)], axis=0)
            x_h = xs_t[h * SSM_HEAD_DIM:(h + 1) * SSM_HEAD_DIM]
            state = state_ref[rows, :]
            lhs = jnp.concatenate([(x_h * dt_t[hh:hh + 1, :]).astype(BF16), state.astype(BF16)], axis=1)
            y_t.append(jnp.dot(lhs, rhs, preferred_element_type=F32))
            if (h + 1) % SSD_HEADS_PER_STAGE == 0:
                yield
        heads = range(g * HEADS_PER_SSM_GROUP, (g + 1) * HEADS_PER_SSM_GROUP)
        grows = slice(heads[0] * SSM_HEAD_DIM, (heads[-1] + 1) * SSM_HEAD_DIM)
        def scale_heads(x, t):
            hp = x.reshape(HEADS_PER_SSM_GROUP, SSM_HEAD_DIM, q) * t[heads[0]:heads[-1] + 1][:, None, :]
            return hp.reshape(x.shape)
        upd = jnp.dot(scale_heads(xs_t, w_t).astype(BF16), bm16, preferred_element_type=F32)
        state_ref[grows, :] = scale_heads(state_ref[grows, :], state_decay) + upd
        yield
        y = jnp.concatenate(y_t, axis=0).T
        cols = slice(g * gw, (g + 1) * gw)
        y = y + xs_g * dskip_ref[:, cols]
        z = z_ref[0, 0, rq, cols].astype(F32)
        yg = y * (z * _sigmoid(z))
        yg = yg * lax.rsqrt(jnp.mean(yg * yg, axis=-1, keepdims=True) + EPS)
        ssm_ref[rq, cols] = (yg * nw_ref[:, cols]).astype(ssm_ref.dtype)
        yield

    slab_ref[:, 0:SUBLANES, :] = slab_ref[:, q:q + SUBLANES, :]


def _merge_ffn_stages(attn_refs, ssm_ref, gate_ref, h_ref, wa_ref, ws_ref, wo_ref, nat_ref,
                      nw_ref, wgu_ref, wd_ref, fw_ref, out_ref, *, ffn_chunk, final_norm):
    tb = h_ref.shape[1]
    n_slabs = GROUP_WIDTH // LANES
    for k, src in enumerate(attn_refs):
        dil = src.shape[1]
        for r in range(dil if dil > 1 else 0):
            for c in range(n_slabs):
                nat_ref[k, c, pl.ds(r, tb // dil, stride=dil), :] = src[0, r, :, c * LANES:(c + 1) * LANES]

    def natural(k, c):
        src = attn_refs[k]
        return src[0, 0, :, c * LANES:(c + 1) * LANES] if src.shape[1] == 1 else nat_ref[k, c]

    slabs = []
    for c in range(n_slabs):
        o = [natural(2 * g, c) for g in range(N_DIL)]
        l = [natural(2 * g + 1, c) for g in range(N_DIL)]
        mx = functools.reduce(jnp.maximum, l)
        e = [jnp.exp(lg - mx) for lg in l]
        num = functools.reduce(jnp.add, [eg * og for eg, og in zip(e, o)])
        slabs.append((num / functools.reduce(jnp.add, e)).astype(BF16))
    attn = jnp.concatenate(slabs, axis=1)
    yield
    a = jnp.dot(attn, wa_ref[...], preferred_element_type=F32)
    yield
    s = jnp.dot(ssm_ref[...], ws_ref[...], preferred_element_type=F32)
    d = a.shape[1]
    gates = _sigmoid(gate_ref[0, 0].astype(F32))
    merged = (gates[:, :d] * a + gates[:, d:] * s).astype(BF16)
    yield
    acc = h_ref[0] + jnp.dot(merged, wo_ref[...], preferred_element_type=F32)
    xn = _rmsnorm(acc, nw_ref[...]).astype(BF16)
    yield
    d_ff = wd_ref.shape[0]
    chunks = [slice(c0, c0 + ffn_chunk) for c0 in range(0, d_ff, ffn_chunk)]
    gate_up = None
    for c, cols in enumerate(chunks + [None]):
        prev_cols, prev_gate_up = (chunks[c - 1], gate_up) if c > 0 else (None, None)
        if cols is not None:
            up_cols = slice(d_ff + cols.start, d_ff + cols.stop)
            gate = jnp.dot(xn, wgu_ref[:, cols], preferred_element_type=F32)
            yield
            gate_up = (gate, jnp.dot(xn, wgu_ref[:, up_cols], preferred_element_type=F32))
            yield
        if prev_gate_up is not None:
            act = (prev_gate_up[0] * _sigmoid(prev_gate_up[0]) * prev_gate_up[1]).astype(BF16)
            acc = acc + jnp.dot(act, wd_ref[prev_cols, :], preferred_element_type=F32)
            yield
    out_ref[0] = _rmsnorm(acc, fw_ref[...]) if final_norm else acc


N_TAIL_WEIGHTS = 13


def _tail_kernel(*refs, blocks_per_seq, ffn_chunk, ssd_stages_per_matmul, final_norm):
    attn_refs = refs[:2 * N_DIL]
    gate_ref, h_ref, xs_ref, bc_ref, z_ref, dt_ref = refs[2 * N_DIL:2 * N_DIL + 6]
    weight_refs = refs[2 * N_DIL + 6:2 * N_DIL + 6 + N_TAIL_WEIGHTS]
    (wa_ref, ws_ref, wo_ref, n2_ref, wgu_ref, wd_ref, fw_ref,
     cw_ref, cb_ref, dtb_ref, alog_ref, dskip_ref, nw_ref) = [r.at[0] for r in weight_refs]
    out_ref, state_ref, slab_ref, ssm_ref, new_ref, nat_ref = refs[2 * N_DIL + 6 + N_TAIL_WEIGHTS:]

    @pl.when(lax.rem(pl.program_id(0), blocks_per_seq) == 0)
    def _():
        state_ref[...] = jnp.zeros_like(state_ref)
        slab_ref[:, 0:SUBLANES, :] = jnp.zeros((slab_ref.shape[0], SUBLANES, LANES), F32)

    @pl.when(pl.program_id(0) == 0)
    def _():
        ssm_ref[...] = jnp.zeros_like(ssm_ref)

    def ssd_stages():
        for r0 in range(0, xs_ref.shape[2], SSD_CHUNK):
            decays = _ssd_decays(r0, dt_ref, dtb_ref, alog_ref)
            yield
            yield from _ssd_chunk_stages(r0, decays, xs_ref, bc_ref, z_ref, cw_ref, cb_ref, dskip_ref, nw_ref,
                                         new_ref, state_ref, slab_ref)

    matmuls = _merge_ffn_stages(attn_refs, ssm_ref, gate_ref, h_ref, wa_ref, ws_ref, wo_ref, nat_ref,
                                n2_ref, wgu_ref, wd_ref, fw_ref, out_ref, ffn_chunk=ffn_chunk, final_norm=final_norm)
    turns = [(matmuls, 1), (ssd_stages(), ssd_stages_per_matmul)]
    live = [gen for gen, _ in turns]
    while live:
        for gen, reps in turns:
            for _ in range(reps if gen in live else 0):
                if next(gen, StopIteration) is StopIteration:
                    live.remove(gen)
                    break
    ssm_ref[...] = new_ref[...]


def _tail(attn_outs, gates, xs, bc, z, dt3, h3, weights, *, layer, tb, ffn_chunk, ssd_stages_per_matmul,
          final_norm):
    b, s, d = h3.shape
    assert len(weights) == N_TAIL_WEIGHTS
    conv_ch = weights[7].shape[-1]
    d_inner = xs.shape[-1]
    n_blocks = s // tb

    def cur(k):
        return jnp.divmod(jnp.minimum(k, b * n_blocks - 1), n_blocks)

    def prev(k):
        return jnp.divmod(jnp.maximum(k - 1, 0), n_blocks)

    def rows3(width, block_index):
        def index_map(k):
            bi, blk = block_index(k)
            return bi, blk, 0
        return pl.BlockSpec((1, tb, width), index_map)

    def rows4(arr, block_index):
        dil = arr.shape[1]

        def index_map(k):
            bi, blk = block_index(k)
            return bi, 0, blk, 0
        return pl.BlockSpec((1, dil, tb // dil, arr.shape[-1]), index_map)

    flat = [a for pair in attn_outs for a in pair]
    in_specs = [rows4(a, prev) for a in flat] + [
        rows4(gates, prev), rows3(d, prev), rows4(xs, cur), rows4(bc, cur), rows4(z, cur), rows3(LANES, cur),
    ] + [_layer_spec(w, min(layer, w.shape[0] - 1)) for w in weights]
    return pl.pallas_call(
        functools.partial(_tail_kernel, blocks_per_seq=n_blocks, ffn_chunk=ffn_chunk,
                          ssd_stages_per_matmul=ssd_stages_per_matmul, final_norm=final_norm),
        grid=(b * n_blocks + 1,),
        in_specs=in_specs,
        out_specs=rows3(d, prev),
        out_shape=jax.ShapeDtypeStruct((b, s, d), F32),
        scratch_shapes=[pltpu.VMEM((N_SSM_GROUPS * HEADS_PER_SSM_GROUP * SSM_HEAD_DIM, D_STATE), F32),
                        pltpu.VMEM((conv_ch // LANES, SUBLANES + SSD_CHUNK, LANES), F32),
                        pltpu.VMEM((tb, d_inner), BF16),
                        pltpu.VMEM((tb, d_inner), BF16),
                        pltpu.VMEM((len(flat), GROUP_WIDTH // LANES, tb, LANES), F32)],
        compiler_params=_compiler_params(("arbitrary",)),
        name="layer_tail",
    )(*flat, gates, h3, xs, bc, z, dt3, *weights)


def kernel(x, norm1_w, w_in, conv_w, conv_b, dt_bias, a_log, d_skip, ssm_norm_w, w_attn_branch, w_ssm_branch,
           w_out, norm2_w, w_ffn_in, w_ffn_out, rel_bias, final_norm_w):
    b, s, d = x.shape
    depth = w_in.shape[0]
    d_inner = ssm_norm_w.shape[1]
    n_heads = dt_bias.shape[1]
    bc_w = 2 * N_SSM_GROUPS * D_STATE
    assert d_inner == N_SSM_GROUPS * HEADS_PER_SSM_GROUP * SSM_HEAD_DIM and n_heads <= LANES
    assert s % (DILATED_GROUPS[-1][1] * ATTN_BLOCK) == 0 and s % SSD_CHUNK == 0 and DILATED_GROUPS[0][1] == 1

    o_z = 3 * ATTN_WIDTH
    o_xbc = o_z + d_inner
    o_dt = o_xbc + d_inner + bc_w
    o_gate = o_dt + n_heads
    assert o_dt % LANES == 0 and o_dt + LANES <= w_in.shape[2]
    w_in16 = w_in.astype(BF16)
    w_gates16 = w_in[:, :, o_gate:].astype(BF16)
    qkv_start = lambda part, g: part * ATTN_WIDTH + g * GROUP_WIDTH
    main_weights = [(w_in16, None, 0), (w_gates16, None, 0)]
    main_outputs = ((1, ((0, o_z, d_inner),)), (1, ((1, 0, 2 * d),)), (1, ((0, o_xbc, d_inner),)),
                    (1, ((0, o_xbc + d_inner, bc_w),)),
                    (1, tuple((0, qkv_start(part, 0), GROUP_WIDTH) for part in range(3))))
    dilated = [(g, dil) for g, (_, dil) in enumerate(DILATED_GROUPS) if dil > 1]
    dilated_weights = [(w_in16, GROUP_WIDTH, qkv_start(part, g) // GROUP_WIDTH) for g, _ in dilated for part in range(3)]
    dilated_outputs = tuple((dil, tuple((3 * n + part, 0, GROUP_WIDTH) for part in range(3)))
                            for n, (_, dil) in enumerate(dilated))

    pad_h = ((0, 0), (0, 0), (0, LANES - n_heads))
    row3 = lambda a: a[:, None, :]
    tail_weights = (w_attn_branch.astype(BF16), w_ssm_branch.astype(BF16), w_out.astype(BF16), row3(norm2_w),
                    w_ffn_in.astype(BF16), w_ffn_out.astype(BF16), final_norm_w[None, None, :],
                    conv_w, row3(conv_b), jnp.pad(row3(dt_bias), pad_h), jnp.pad(row3(a_log), pad_h),
                    row3(jnp.repeat(d_skip, SSM_HEAD_DIM, axis=1)), row3(ssm_norm_w))
    norm1_w3 = row3(norm1_w)
    bias_tables = _attention_bias_tables(rel_bias)

    h = x
    for layer in range(depth):
        z, gates, xs, bc, qkv0, dt = _in_proj(h, norm1_w3, main_weights, main_outputs, (0, o_dt, LANES),
                                              layer=layer, tm=IN_PROJ_ROWS, chunk=IN_PROJ_CHUNK, name="in_proj")
        qkv_dilated = _in_proj(h, norm1_w3, dilated_weights, dilated_outputs, layer=layer,
                               tm=IN_PROJ_DILATED_ROWS, chunk=IN_PROJ_CHUNK, name="in_proj_dilated")
        qkvs = {0: qkv0, **{g: arr for (g, _), arr in zip(dilated, qkv_dilated)}}
        attn_outs = [_attention(qkvs[g], bias_tables[g], col_q=0, col_k=1, col_v=2, rows=ATTN_ROWS)
                     for g in range(N_DIL)]
        h = _tail(attn_outs, gates, xs, bc, z, dt, h, tail_weights, layer=layer, tb=TAIL_ROWS,
                  ffn_chunk=FFN_CHUNK, ssd_stages_per_matmul=SSD_STAGES_PER_MATMUL,
                  final_norm=(layer == depth - 1))
    return h
```

```python
import functools
import math

import jax
import jax.numpy as jnp
from jax import lax
from jax.experimental import pallas as pl
from jax.experimental.pallas import tpu as pltpu

HEAD_DIM = 64
DILATED_GROUPS = ((128, 1), (512, 4), (2048, 16))
N_DIL = len(DILATED_GROUPS)
HEADS_PER_GROUP = 8
GROUP_WIDTH = HEADS_PER_GROUP * HEAD_DIM
ATTN_WIDTH = N_DIL * GROUP_WIDTH
ATTN_BLOCK = 128
N_REL_BUCKETS = 32
REL_MAX_DISTANCE = 2048
SSM_HEAD_DIM = 64
N_SSM_GROUPS = 4
HEADS_PER_SSM_GROUP = 8
D_STATE = 128
CONV_WIDTH = 4
SSD_CHUNK = 128
EPS = 1e-6

LANES = 128
SUBLANES = 8
VMEM_LIMIT_BYTES = 60000 * 1024

IN_PROJ_ROWS = 512
IN_PROJ_DILATED_ROWS = 1024
IN_PROJ_CHUNK = 1024
ATTN_ROWS = 2048
TAIL_ROWS = 256
FFN_CHUNK = 256
SSD_STAGES_PER_MATMUL = 1
SSD_HEADS_PER_STAGE = 5

BF16 = jnp.bfloat16
F32 = jnp.float32
NT_DIMS = (((1,), (1,)), ((), ()))


def _compiler_params(semantics):
    return pltpu.CompilerParams(dimension_semantics=semantics, vmem_limit_bytes=VMEM_LIMIT_BYTES)


def _rmsnorm(x, w):
    return x * lax.rsqrt(jnp.mean(x * x, axis=-1, keepdims=True) + EPS) * w


def _sigmoid(x):
    return 1.0 / (1.0 + jnp.exp(-x))


PERM_BLOCK = 256


def _in_proj_kernel(x_ref, nw_ref, *rest, outputs, dt_piece, chunk):
    n_w = 1 + max(p[0] for _, pieces in outputs for p in pieces)
    w_refs = [r.at[0] for r in rest[:n_w]]
    out_refs = rest[n_w:n_w + len(outputs)]
    rest = rest[n_w + len(outputs):]
    if dt_piece is not None:
        dt_ref, rest = rest[0], rest[1:]
    dilations = sorted({dil for dil, _ in outputs})
    xn_refs = dict(zip(dilations, rest))
    nw_ref = nw_ref.at[0]
    tm = x_ref.shape[1]
    xn = _rmsnorm(x_ref[0], nw_ref[...]).astype(BF16)
    if dt_piece is not None:
        src, c0, width = dt_piece
        dt_ref[0] = jnp.dot(xn, w_refs[src][:, c0:c0 + width], preferred_element_type=F32)
    for dilation, xn_ref in xn_refs.items():
        if dilation == 1:
            xn_ref[...] = xn
            continue
        per = PERM_BLOCK // dilation
        i = lax.broadcasted_iota(jnp.int32, (PERM_BLOCK, PERM_BLOCK), 0)
        k = lax.broadcasted_iota(jnp.int32, (PERM_BLOCK, PERM_BLOCK), 1)
        src = (i & (per - 1)) * dilation + (i >> (per.bit_length() - 1))
        perm = jnp.where(k == src, 1.0, 0.0).astype(BF16)
        for u in range(tm // PERM_BLOCK):
            y = jnp.dot(perm, xn[u * PERM_BLOCK:(u + 1) * PERM_BLOCK], preferred_element_type=F32).astype(BF16)
            for r in range(dilation):
                dst = r * (tm // dilation) + u * per
                xn_ref[dst:dst + per, :] = y[r * per:(r + 1) * per]
    for out_ref, (dilation, pieces) in zip(out_refs, outputs):
        col = 0
        for src, start, width in pieces:
            for c0 in range(0, width, chunk):
                cw = min(chunk, width - c0)
                res = jnp.dot(xn_refs[dilation][...], w_refs[src][:, start + c0:start + c0 + cw],
                              preferred_element_type=F32)
                out_ref[0, :, :, col:col + cw] = res.astype(out_ref.dtype).reshape(dilation, tm // dilation, cw)
                col += cw


def _layer_spec(arr, layer, block=None, col_block=0):
    shape = (1,) + tuple(arr.shape[1:-1]) + (block or arr.shape[-1],)
    index = (layer,) + (0,) * (arr.ndim - 2) + (col_block,)
    return pl.BlockSpec(shape, lambda *_: index, pipeline_mode=pl.Buffered(1))


def _in_proj(h3, norm_w, weights, outputs, dt_piece=None, *, layer, tm, chunk, name):
    b, s, d = h3.shape
    in_specs = [pl.BlockSpec((1, tm, d), lambda bi, i: (bi, i, 0)), _layer_spec(norm_w, layer)]
    in_specs += [_layer_spec(arr, layer, blk, cb) for arr, blk, cb in weights]
    shapes = [(dil, sum(p[2] for p in pieces)) for dil, pieces in outputs]
    out_specs = [pl.BlockSpec((1, dil, tm // dil, n), lambda bi, i: (bi, 0, i, 0)) for dil, n in shapes]
    out_shape = [jax.ShapeDtypeStruct((b, dil, s // dil, n), BF16) for dil, n in shapes]
    if dt_piece is not None:
        out_specs.append(pl.BlockSpec((1, tm, LANES), lambda bi, i: (bi, i, 0)))
        out_shape.append(jax.ShapeDtypeStruct((b, s, LANES), F32))
    return pl.pallas_call(
        functools.partial(_in_proj_kernel, outputs=outputs, dt_piece=dt_piece, chunk=chunk),
        grid=(b, s // tm),
        in_specs=in_specs,
        out_specs=out_specs,
        out_shape=out_shape,
        scratch_shapes=[pltpu.VMEM((tm, d), BF16) for _ in {dil for dil, _ in outputs}],
        compiler_params=_compiler_params(("parallel", "parallel")),
        name=name,
    )(h3, norm_w, *[arr for arr, _, _ in weights])


def _attention_kernel(q_ref, kp_ref, kc_ref, vp_ref, vc_ref, bias_ref, o_ref, l_ref, kext_ref, vext_ref, *, tq):
    n = pl.program_id(2)
    kext_ref[:, 0:ATTN_BLOCK] = kp_ref[0]
    kext_ref[:, ATTN_BLOCK:] = kc_ref[0]
    vext_ref[:, 0:ATTN_BLOCK] = vp_ref[0]
    vext_ref[:, ATTN_BLOCK:] = vc_ref[0]
    low_half = lax.broadcasted_iota(jnp.int32, (1, LANES), 1) < HEAD_DIM

    for rr, s in ((rr, s) for rr in range(q_ref.shape[1]) for s in range(tq // ATTN_BLOCK)):
        r0 = s * ATTN_BLOCK
        qs = q_ref[0, rr, pl.ds(r0, ATTN_BLOCK), :] * (HEAD_DIM ** -0.5)
        ks = kext_ref[rr, pl.ds(r0, 2 * ATTN_BLOCK), :]
        vs = vext_ref[rr, pl.ds(r0, 2 * ATTN_BLOCK), :]
        bidx = jnp.where(n == 0, 0, 1) if s == 0 else 1
        for p in range(GROUP_WIDTH // LANES):
            cols = slice(p * LANES, (p + 1) * LANES)
            qp, kp, vp = qs[:, cols], ks[:, cols], vs[:, cols]
            q2 = jnp.concatenate([jnp.where(low_half, qp, jnp.zeros_like(qp)),
                                  jnp.where(low_half, jnp.zeros_like(qp), qp)], axis=0)
            bias2 = jnp.concatenate([bias_ref[bidx, 2 * p], bias_ref[bidx, 2 * p + 1]], axis=0)
            sc = lax.dot_general(q2, kp, NT_DIMS, preferred_element_type=F32) + bias2
            m = jnp.max(sc, axis=1, keepdims=True)
            pe = jnp.exp(sc - m).astype(BF16)
            nd = jnp.dot(pe, jnp.concatenate([vp, jnp.ones_like(vp)], axis=1), preferred_element_type=F32)
            num, den = nd[:, :LANES], nd[:, LANES:]
            o2 = num / den
            l2 = m + jnp.log(den)
            outs = [o2[:ATTN_BLOCK], o2[ATTN_BLOCK:]]
            lses = [l2[:ATTN_BLOCK], l2[ATTN_BLOCK:]]
            o_ref[0, rr, pl.ds(r0, ATTN_BLOCK), cols] = jnp.where(low_half, outs[0], outs[1])
            l_ref[0, rr, pl.ds(r0, ATTN_BLOCK), cols] = jnp.where(low_half, lses[0], lses[1])


def _attention(qkv, bias_g, *, col_q, col_k, col_v, rows):
    b, dilation, seg, _ = qkv.shape
    tq = min(rows, seg)
    res = min(dilation, rows // tq)
    blocks_per_tile = tq // ATTN_BLOCK

    def cur(col):
        return pl.BlockSpec((1, res, tq, GROUP_WIDTH), lambda bi, r, n: (bi, r, n, col))

    def prev(col):
        return pl.BlockSpec((1, res, ATTN_BLOCK, GROUP_WIDTH),
                            lambda bi, r, n: (bi, r, jnp.maximum(n * blocks_per_tile - 1, 0), col))

    out_spec = pl.BlockSpec((1, res, tq, GROUP_WIDTH), lambda bi, r, n: (bi, r, n, 0))
    out_sds = jax.ShapeDtypeStruct((b, dilation, seg, GROUP_WIDTH), F32)
    return pl.pallas_call(
        functools.partial(_attention_kernel, tq=tq),
        grid=(b, dilation // res, seg // tq),
        in_specs=[cur(col_q), prev(col_k), cur(col_k), prev(col_v), cur(col_v),
                  pl.BlockSpec(bias_g.shape, lambda bi, r, n: (0, 0, 0, 0))],
        out_specs=[out_spec, out_spec],
        out_shape=[out_sds, out_sds],
        scratch_shapes=[pltpu.VMEM((res, tq + ATTN_BLOCK, GROUP_WIDTH), BF16),
                        pltpu.VMEM((res, tq + ATTN_BLOCK, GROUP_WIDTH), BF16)],
        compiler_params=_compiler_params(("parallel", "parallel", "arbitrary")),
        name=f"attention_d{dilation}",
    )(qkv, qkv, qkv, qkv, qkv, bias_g)


def _t5_causal_bucket(dist):
    max_exact = N_REL_BUCKETS // 2
    d_f = jnp.maximum(dist, 1).astype(F32)
    large = max_exact + (jnp.log(d_f / max_exact) / math.log(REL_MAX_DISTANCE / max_exact)
                         * (N_REL_BUCKETS - max_exact)).astype(jnp.int32)
    large = jnp.minimum(large, N_REL_BUCKETS - 1)
    return jnp.where(dist < max_exact, dist, large)


def _attention_bias_tables(rel_bias):
    q, q2 = ATTN_BLOCK, 2 * ATTN_BLOCK
    steps = jnp.arange(q2) - (q - 1)
    in_prev = (jnp.arange(q2) < q)[None, None, :]
    tables = []
    for g, (window, dil) in enumerate(DILATED_GROUPS):
        n_steps = window // dil
        assert n_steps <= q
        rel_g = rel_bias[:, g * HEADS_PER_GROUP:(g + 1) * HEADS_PER_GROUP].astype(F32)
        vals = rel_g[_t5_causal_bucket(jnp.clip(steps, 0, n_steps) * dil)]
        vec = jnp.where(((steps >= 0) & (steps <= n_steps))[:, None], vals, -jnp.inf).T
        skew = jnp.tile(vec, (1, q + 1))[:, :q * (q2 + 1)].reshape(HEADS_PER_GROUP, q, q2 + 1)[:, :, :q2]
        rest = skew[:, :, ::-1]
        first = jnp.where(in_prev, -jnp.inf, rest)
        tables.append(jnp.stack([first, rest]))
    return jnp.stack(tables)


LOG2E = math.log2(math.e)


def _conv_silu(slab_ref, c, w_ref, b_ref):
    q = slab_ref.shape[1] - SUBLANES
    cols = slice(c * LANES, (c + 1) * LANES)
    acc = b_ref[:, cols] + slab_ref[c, pl.ds(SUBLANES, q), :] * w_ref[CONV_WIDTH - 1:CONV_WIDTH, cols]
    for s in range(1, CONV_WIDTH):
        acc = acc + slab_ref[c, pl.ds(SUBLANES - s, q), :] * w_ref[CONV_WIDTH - 1 - s:CONV_WIDTH - s, cols]
    return acc * _sigmoid(acc)


def _ssd_decays(r0, dt_ref, dtb_ref, alog_ref):
    q = SSD_CHUNK
    x_dt = dt_ref[0, r0:r0 + q, :] + dtb_ref[...]
    dt = jnp.maximum(x_dt, 0.0) + jnp.log(1.0 + jnp.exp(-jnp.abs(x_dt)))
    d_a = dt * (-jnp.exp(alog_ref[...]))
    ri = lax.broadcasted_iota(jnp.int32, (q, q), 0)
    ci = lax.broadcasted_iota(jnp.int32, (q, q), 1)
    causal_t = ri <= ci
    tril = jnp.where(ri >= ci, 1.0, 0.0).astype(BF16)
    d1 = d_a.astype(BF16)
    r1 = d_a - d1.astype(F32)
    d2 = r1.astype(BF16)
    d3 = (r1 - d2.astype(F32)).astype(BF16)
    la = (jnp.dot(tril, d1, preferred_element_type=F32) + jnp.dot(tril, d2, preferred_element_type=F32)
          + jnp.dot(tril, d3, preferred_element_type=F32))
    la = la * LOG2E
    la_t = la.T
    dt_t = dt.T
    last = jnp.broadcast_to(la_t[:, q - 1:q], (q, q))
    w_t = jnp.exp2(last - la_t) * dt_t
    state_decay = jnp.exp2(last)
    return causal_t, la, la_t, dt_t, w_t, state_decay


def _ssd_chunk_stages(r0, decays, xs_ref, bc_ref, z_ref, cw_ref, cb_ref, dskip_ref, nw_ref,
                      ssm_ref, state_ref, slab_ref):
    causal_t, la, la_t, dt_t, w_t, state_decay = decays
    q = SSD_CHUNK
    rq = slice(r0, r0 + q)
    gw = HEADS_PER_SSM_GROUP * SSM_HEAD_DIM
    xs_slabs = xs_ref.shape[3] // LANES
    slabs_per_group = gw // LANES

    for c in range(xs_slabs):
        slab_ref[c, SUBLANES:, :] = xs_ref[0, 0, rq, c * LANES:(c + 1) * LANES].astype(F32)
    for c in range(bc_ref.shape[3] // LANES):
        slab_ref[xs_slabs + c, SUBLANES:, :] = bc_ref[0, 0, rq, c * LANES:(c + 1) * LANES].astype(F32)

    for g in range(N_SSM_GROUPS):
        bm = _conv_silu(slab_ref, xs_slabs + g, cw_ref, cb_ref)
        cm = _conv_silu(slab_ref, xs_slabs + N_SSM_GROUPS + g, cw_ref, cb_ref)
        bm16 = bm.astype(BF16)
        cb_t = lax.dot_general(bm16, cm.astype(BF16), NT_DIMS, preferred_element_type=F32)
        cm_t = cm.T
        xs_g = jnp.concatenate([_conv_silu(slab_ref, g * slabs_per_group + c, cw_ref, cb_ref)
                                for c in range(slabs_per_group)], axis=1)
        xs_t = xs_g.T
        yield
        y_t = []
        for h in range(HEADS_PER_SSM_GROUP):
            hh = g * HEADS_PER_SSM_GROUP + h
            rows = slice(hh * SSM_HEAD_DIM, (hh + 1) * SSM_HEAD_DIM)
            la_i = la_t[hh:hh + 1, :]
            la_j = jnp.broadcast_to(la[:, hh:hh + 1], (q, q))
            decay_t = jnp.exp2(jnp.where(causal_t, la_i - la_j, -jnp.inf))
            rhs = jnp.concatenate([(cb_t * decay_t).astype(BF16), (cm_t * jnp.exp2(la_i)).astype(BF16)], axis=0)
            x_h = xs_t[h * SSM_HEAD_DIM:(h + 1) * SSM_HEAD_DIM]
            state = state_ref[rows, :]
            lhs = jnp.concatenate([(x_h * dt_t[hh:hh + 1, :]).astype(BF16), state.astype(BF16)], axis=1)
            y_t.append(jnp.dot(lhs, rhs, preferred_element_type=F32))
            if (h + 1) % SSD_HEADS_PER_STAGE == 0:
                yield
        heads = range(g * HEADS_PER_SSM_GROUP, (g + 1) * HEADS_PER_SSM_GROUP)
        grows = slice(heads[0] * SSM_HEAD_DIM, (heads[-1] + 1) * SSM_HEAD_DIM)
        def scale_heads(x, t):
            hp = x.reshape(HEADS_PER_SSM_GROUP, SSM_HEAD_DIM, q) * t[heads[0]:heads[-1] + 1][:, None, :]
            return hp.reshape(x.shape)
        upd = jnp.dot(scale_heads(xs_t, w_t).astype(BF16), bm16, preferred_element_type=F32)
        state_ref[grows, :] = scale_heads(state_ref[grows, :], state_decay) + upd
        yield
        y = jnp.concatenate(y_t, axis=0).T
        cols = slice(g * gw, (g + 1) * gw)
        y = y + xs_g * dskip_ref[:, cols]
        z = z_ref[0, 0, rq, cols].astype(F32)
        yg = y * (z * _sigmoid(z))
        yg = yg * lax.rsqrt(jnp.mean(yg * yg, axis=-1, keepdims=True) + EPS)
        ssm_ref[rq, cols] = (yg * nw_ref[:, cols]).astype(ssm_ref.dtype)
        yield

    slab_ref[:, 0:SUBLANES, :] = slab_ref[:, q:q + SUBLANES, :]


def _merge_ffn_stages(attn_refs, ssm_ref, gate_ref, h_ref, wa_ref, ws_ref, wo_ref, nat_ref,
                      nw_ref, wgu_ref, wd_ref, fw_ref, out_ref, *, ffn_chunk, final_norm):
    tb = h_ref.shape[1]
    n_slabs = GROUP_WIDTH // LANES
    for k, src in enumerate(attn_refs):
        dil = src.shape[1]
        for r in range(dil if dil > 1 else 0):
            for c in range(n_slabs):
                nat_ref[k, c, pl.ds(r, tb // dil, stride=dil), :] = src[0, r, :, c * LANES:(c + 1) * LANES]

    def natural(k, c):
        src = attn_refs[k]
        return src[0, 0, :, c * LANES:(c + 1) * LANES] if src.shape[1] == 1 else nat_ref[k, c]

    slabs = []
    for c in range(n_slabs):
        o = [natural(2 * g, c) for g in range(N_DIL)]
        l = [natural(2 * g + 1, c) for g in range(N_DIL)]
        mx = functools.reduce(jnp.maximum, l)
        e = [jnp.exp(lg - mx) for lg in l]
        num = functools.reduce(jnp.add, [eg * og for eg, og in zip(e, o)])
        slabs.append((num / functools.reduce(jnp.add, e)).astype(BF16))
    attn = jnp.concatenate(slabs, axis=1)
    yield
    a = jnp.dot(attn, wa_ref[...], preferred_element_type=F32)
    yield
    s = jnp.dot(ssm_ref[...], ws_ref[...], preferred_element_type=F32)
    d = a.shape[1]
    gates = _sigmoid(gate_ref[0, 0].astype(F32))
    merged = (gates[:, :d] * a + gates[:, d:] * s).astype(BF16)
    yield
    acc = h_ref[0] + jnp.dot(merged, wo_ref[...], preferred_element_type=F32)
    xn = _rmsnorm(acc, nw_ref[...]).astype(BF16)
    yield
    d_ff = wd_ref.shape[0]
    chunks = [slice(c0, c0 + ffn_chunk) for c0 in range(0, d_ff, ffn_chunk)]
    gate_up = None
    for c, cols in enumerate(chunks + [None]):
        prev_cols, prev_gate_up = (chunks[c - 1], gate_up) if c > 0 else (None, None)
        if cols is not None:
            up_cols = slice(d_ff + cols.start, d_ff + cols.stop)
            gate = jnp.dot(xn, wgu_ref[:, cols], preferred_element_type=F32)
            yield
            gate_up = (gate, jnp.dot(xn, wgu_ref[:, up_cols], preferred_element_type=F32))
            yield
        if prev_gate_up is not None:
            act = (prev_gate_up[0] * _sigmoid(prev_gate_up[0]) * prev_gate_up[1]).astype(BF16)
            acc = acc + jnp.dot(act, wd_ref[prev_cols, :], preferred_element_type=F32)
            yield
    out_ref[0] = _rmsnorm(acc, fw_ref[...]) if final_norm else acc


N_TAIL_WEIGHTS = 13


def _tail_kernel(*refs, blocks_per_seq, ffn_chunk, ssd_stages_per_matmul, final_norm):
    attn_refs = refs[:2 * N_DIL]
    gate_ref, h_ref, xs_ref, bc_ref, z_ref, dt_ref = refs[2 * N_DIL:2 * N_DIL + 6]
    weight_refs = refs[2 * N_DIL + 6:2 * N_DIL + 6 + N_TAIL_WEIGHTS]
    (wa_ref, ws_ref, wo_ref, n2_ref, wgu_ref, wd_ref, fw_ref,
     cw_ref, cb_ref, dtb_ref, alog_ref, dskip_ref, nw_ref) = [r.at[0] for r in weight_refs]
    out_ref, state_ref, slab_ref, ssm_ref, new_ref, nat_ref = refs[2 * N_DIL + 6 + N_TAIL_WEIGHTS:]

    @pl.when(lax.rem(pl.program_id(0), blocks_per_seq) == 0)
    def _():
        state_ref[...] = jnp.zeros_like(state_ref)
        slab_ref[:, 0:SUBLANES, :] = jnp.zeros((slab_ref.shape[0], SUBLANES, LANES), F32)

    @pl.when(pl.program_id(0) == 0)
    def _():
        ssm_ref[...] = jnp.zeros_like(ssm_ref)

    def ssd_stages():
        for r0 in range(0, xs_ref.shape[2], SSD_CHUNK):
            decays = _ssd_decays(r0, dt_ref, dtb_ref, alog_ref)
            yield
            yield from _ssd_chunk_stages(r0, decays, xs_ref, bc_ref, z_ref, cw_ref, cb_ref, dskip_ref, nw_ref,
                                         new_ref, state_ref, slab_ref)

    matmuls = _merge_ffn_stages(attn_refs, ssm_ref, gate_ref, h_ref, wa_ref, ws_ref, wo_ref, nat_ref,
                                n2_ref, wgu_ref, wd_ref, fw_ref, out_ref, ffn_chunk=ffn_chunk, final_norm=final_norm)
    turns = [(matmuls, 1), (ssd_stages(), ssd_stages_per_matmul)]
    live = [gen for gen, _ in turns]
    while live:
        for gen, reps in turns:
            for _ in range(reps if gen in live else 0):
                if next(gen, StopIteration) is StopIteration:
                    live.remove(gen)
                    break
    ssm_ref[...] = new_ref[...]


def _tail(attn_outs, gates, xs, bc, z, dt3, h3, weights, *, layer, tb, ffn_chunk, ssd_stages_per_matmul,
          final_norm):
    b, s, d = h3.shape
    assert len(weights) == N_TAIL_WEIGHTS
    conv_ch = weights[7].shape[-1]
    d_inner = xs.shape[-1]
    n_blocks = s // tb

    def cur(k):
        return jnp.divmod(jnp.minimum(k, b * n_blocks - 1), n_blocks)

    def prev(k):
        return jnp.divmod(jnp.maximum(k - 1, 0), n_blocks)

    def rows3(width, block_index):
        def index_map(k):
            bi, blk = block_index(k)
            return bi, blk, 0
        return pl.BlockSpec((1, tb, width), index_map)

    def rows4(arr, block_index):
        dil = arr.shape[1]

        def index_map(k):
            bi, blk = block_index(k)
            return bi, 0, blk, 0
        return pl.BlockSpec((1, dil, tb // dil, arr.shape[-1]), index_map)

    flat = [a for pair in attn_outs for a in pair]
    in_specs = [rows4(a, prev) for a in flat] + [
        rows4(gates, prev), rows3(d, prev), rows4(xs, cur), rows4(bc, cur), rows4(z, cur), rows3(LANES, cur),
    ] + [_layer_spec(w, min(layer, w.shape[0] - 1)) for w in weights]
    return pl.pallas_call(
        functools.partial(_tail_kernel, blocks_per_seq=n_blocks, ffn_chunk=ffn_chunk,
                          ssd_stages_per_matmul=ssd_stages_per_matmul, final_norm=final_norm),
        grid=(b * n_blocks + 1,),
        in_specs=in_specs,
        out_specs=rows3(d, prev),
        out_shape=jax.ShapeDtypeStruct((b, s, d), F32),
        scratch_shapes=[pltpu.VMEM((N_SSM_GROUPS * HEADS_PER_SSM_GROUP * SSM_HEAD_DIM, D_STATE), F32),
                        pltpu.VMEM((conv_ch // LANES, SUBLANES + SSD_CHUNK, LANES), F32),
                        pltpu.VMEM((tb, d_inner), BF16),
                        pltpu.VMEM((tb, d_inner), BF16),
                        pltpu.VMEM((len(flat), GROUP_WIDTH // LANES, tb, LANES), F32)],
        compiler_params=_compiler_params(("arbitrary",)),
        name="layer_tail",
    )(*flat, gates, h3, xs, bc, z, dt3, *weights)


def kernel(x, norm1_w, w_in, conv_w, conv_b, dt_bias, a_log, d_skip, ssm_norm_w, w_attn_branch, w_ssm_branch,
           w_out, norm2_w, w_ffn_in, w_ffn_out, rel_bias, final_norm_w):
    b, s, d = x.shape
    depth = w_in.shape[0]
    d_inner = ssm_norm_w.shape[1]
    n_heads = dt_bias.shape[1]
    bc_w = 2 * N_SSM_GROUPS * D_STATE
    assert d_inner == N_SSM_GROUPS * HEADS_PER_SSM_GROUP * SSM_HEAD_DIM and n_heads <= LANES
    assert s % (DILATED_GROUPS[-1][1] * ATTN_BLOCK) == 0 and s % SSD_CHUNK == 0 and DILATED_GROUPS[0][1] == 1

    o_z = 3 * ATTN_WIDTH
    o_xbc = o_z + d_inner
    o_dt = o_xbc + d_inner + bc_w
    o_gate = o_dt + n_heads
    assert o_dt % LANES == 0 and o_dt + LANES <= w_in.shape[2]
    w_in16 = w_in.astype(BF16)
    w_gates16 = w_in[:, :, o_gate:].astype(BF16)
    qkv_start = lambda part, g: part * ATTN_WIDTH + g * GROUP_WIDTH
    main_weights = [(w_in16, None, 0), (w_gates16, None, 0)]
    main_outputs = ((1, ((0, o_z, d_inner),)), (1, ((1, 0, 2 * d),)), (1, ((0, o_xbc, d_inner),)),
                    (1, ((0, o_xbc + d_inner, bc_w),)),
                    (1, tuple((0, qkv_start(part, 0), GROUP_WIDTH) for part in range(3))))
    dilated = [(g, dil) for g, (_, dil) in enumerate(DILATED_GROUPS) if dil > 1]
    dilated_weights = [(w_in16, GROUP_WIDTH, qkv_start(part, g) // GROUP_WIDTH) for g, _ in dilated for part in range(3)]
    dilated_outputs = tuple((dil, tuple((3 * n + part, 0, GROUP_WIDTH) for part in range(3)))
                            for n, (_, dil) in enumerate(dilated))

    pad_h = ((0, 0), (0, 0), (0, LANES - n_heads))
    row3 = lambda a: a[:, None, :]
    tail_weights = (w_attn_branch.astype(BF16), w_ssm_branch.astype(BF16), w_out.astype(BF16), row3(norm2_w),
                    w_ffn_in.astype(BF16), w_ffn_out.astype(BF16), final_norm_w[None, None, :],
                    conv_w, row3(conv_b), jnp.pad(row3(dt_bias), pad_h), jnp.pad(row3(a_log), pad_h),
                    row3(jnp.repeat(d_skip, SSM_HEAD_DIM, axis=1)), row3(ssm_norm_w))
    norm1_w3 = row3(norm1_w)
    bias_tables = _attention_bias_tables(rel_bias)

    h = x
    for layer in range(depth):
        z, gates, xs, bc, qkv0, dt = _in_proj(h, norm1_w3, main_weights, main_outputs, (0, o_dt, LANES),
                                              layer=layer, tm=IN_PROJ_ROWS, chunk=IN_PROJ_CHUNK, name="in_proj")
        qkv_dilated = _in_proj(h, norm1_w3, dilated_weights, dilated_outputs, layer=layer,
                               tm=IN_PROJ_DILATED_ROWS, chunk=IN_PROJ_CHUNK, name="in_proj_dilated")
        qkvs = {0: qkv0, **{g: arr for (g, _), arr in zip(dilated, qkv_dilated)}}
        attn_outs = [_attention(qkvs[g], bias_tables[g], col_q=0, col_k=1, col_v=2, rows=ATTN_ROWS)
                     for g in range(N_DIL)]
        h = _tail(attn_outs, gates, xs, bc, z, dt, h, tail_weights, layer=layer, tb=TAIL_ROWS,
                  ffn_chunk=FFN_CHUNK, ssd_stages_per_matmul=SSD_STAGES_PER_MATMUL,
                  final_norm=(layer == depth - 1))
    return h
```

```python
import functools
import math

import jax
import jax.numpy as jnp
from jax import lax
from jax.experimental import pallas as pl
from jax.experimental.pallas import tpu as pltpu

HEAD_DIM = 64
DILATED_GROUPS = ((128, 1), (512, 4), (2048, 16))
N_DIL = len(DILATED_GROUPS)
HEADS_PER_GROUP = 8
GROUP_WIDTH = HEADS_PER_GROUP * HEAD_DIM
ATTN_WIDTH = N_DIL * GROUP_WIDTH
ATTN_BLOCK = 128
N_REL_BUCKETS = 32
REL_MAX_DISTANCE = 2048
SSM_HEAD_DIM = 64
N_SSM_GROUPS = 4
HEADS_PER_SSM_GROUP = 8
D_STATE = 128
CONV_WIDTH = 4
SSD_CHUNK = 128
EPS = 1e-6

LANES = 128
SUBLANES = 8
VMEM_LIMIT_BYTES = 60000 * 1024

IN_PROJ_ROWS = 512
IN_PROJ_DILATED_ROWS = 1024
IN_PROJ_CHUNK = 1024
ATTN_ROWS = 2048
TAIL_ROWS = 256
FFN_CHUNK = 256
SSD_STAGES_PER_MATMUL = 1
SSD_HEADS_PER_STAGE = 6

BF16 = jnp.bfloat16
F32 = jnp.float32
NT_DIMS = (((1,), (1,)), ((), ()))


def _compiler_params(semantics):
    return pltpu.CompilerParams(dimension_semantics=semantics, vmem_limit_bytes=VMEM_LIMIT_BYTES)


def _rmsnorm(x, w):
    return x * lax.rsqrt(jnp.mean(x * x, axis=-1, keepdims=True) + EPS) * w


def _sigmoid(x):
    return 1.0 / (1.0 + jnp.exp(-x))


PERM_BLOCK = 256


def _in_proj_kernel(x_ref, nw_ref, *rest, outputs, dt_piece, chunk):
    n_w = 1 + max(p[0] for _, pieces in outputs for p in pieces)
    w_refs = [r.at[0] for r in rest[:n_w]]
    out_refs = rest[n_w:n_w + len(outputs)]
    rest = rest[n_w + len(outputs):]
    if dt_piece is not None:
        dt_ref, rest = rest[0], rest[1:]
    dilations = sorted({dil for dil, _ in outputs})
    xn_refs = dict(zip(dilations, rest))
    nw_ref = nw_ref.at[0]
    tm = x_ref.shape[1]
    xn = _rmsnorm(x_ref[0], nw_ref[...]).astype(BF16)
    if dt_piece is not None:
        src, c0, width = dt_piece
        dt_ref[0] = jnp.dot(xn, w_refs[src][:, c0:c0 + width], preferred_element_type=F32)
    for dilation, xn_ref in xn_refs.items():
        if dilation == 1:
            xn_ref[...] = xn
            continue
        per = PERM_BLOCK // dilation
        i = lax.broadcasted_iota(jnp.int32, (PERM_BLOCK, PERM_BLOCK), 0)
        k = lax.broadcasted_iota(jnp.int32, (PERM_BLOCK, PERM_BLOCK), 1)
        src = (i & (per - 1)) * dilation + (i >> (per.bit_length() - 1))
        perm = jnp.where(k == src, 1.0, 0.0).astype(BF16)
        for u in range(tm // PERM_BLOCK):
            y = jnp.dot(perm, xn[u * PERM_BLOCK:(u + 1) * PERM_BLOCK], preferred_element_type=F32).astype(BF16)
            for r in range(dilation):
                dst = r * (tm // dilation) + u * per
                xn_ref[dst:dst + per, :] = y[r * per:(r + 1) * per]
    for out_ref, (dilation, pieces) in zip(out_refs, outputs):
        col = 0
        for src, start, width in pieces:
            for c0 in range(0, width, chunk):
                cw = min(chunk, width - c0)
                res = jnp.dot(xn_refs[dilation][...], w_refs[src][:, start + c0:start + c0 + cw],
                              preferred_element_type=F32)
                out_ref[0, :, :, col:col + cw] = res.astype(out_ref.dtype).reshape(dilation, tm // dilation, cw)
                col += cw


def _layer_spec(arr, layer, block=None, col_block=0):
    shape = (1,) + tuple(arr.shape[1:-1]) + (block or arr.shape[-1],)
    index = (layer,) + (0,) * (arr.ndim - 2) + (col_block,)
    return pl.BlockSpec(shape, lambda *_: index, pipeline_mode=pl.Buffered(1))


def _in_proj(h3, norm_w, weights, outputs, dt_piece=None, *, layer, tm, chunk, name):
    b, s, d = h3.shape
    in_specs = [pl.BlockSpec((1, tm, d), lambda bi, i: (bi, i, 0)), _layer_spec(norm_w, layer)]
    in_specs += [_layer_spec(arr, layer, blk, cb) for arr, blk, cb in weights]
    shapes = [(dil, sum(p[2] for p in pieces)) for dil, pieces in outputs]
    out_specs = [pl.BlockSpec((1, dil, tm // dil, n), lambda bi, i: (bi, 0, i, 0)) for dil, n in shapes]
    out_shape = [jax.ShapeDtypeStruct((b, dil, s // dil, n), BF16) for dil, n in shapes]
    if dt_piece is not None:
        out_specs.append(pl.BlockSpec((1, tm, LANES), lambda bi, i: (bi, i, 0)))
        out_shape.append(jax.ShapeDtypeStruct((b, s, LANES), F32))
    return pl.pallas_call(
        functools.partial(_in_proj_kernel, outputs=outputs, dt_piece=dt_piece, chunk=chunk),
        grid=(b, s // tm),
        in_specs=in_specs,
        out_specs=out_specs,
        out_shape=out_shape,
        scratch_shapes=[pltpu.VMEM((tm, d), BF16) for _ in {dil for dil, _ in outputs}],
        compiler_params=_compiler_params(("parallel", "parallel")),
        name=name,
    )(h3, norm_w, *[arr for arr, _, _ in weights])


def _attention_kernel(q_ref, kp_ref, kc_ref, vp_ref, vc_ref, bias_ref, o_ref, l_ref, kext_ref, vext_ref, *, tq):
    n = pl.program_id(2)
    kext_ref[:, 0:ATTN_BLOCK] = kp_ref[0]
    kext_ref[:, ATTN_BLOCK:] = kc_ref[0]
    vext_ref[:, 0:ATTN_BLOCK] = vp_ref[0]
    vext_ref[:, ATTN_BLOCK:] = vc_ref[0]
    low_half = lax.broadcasted_iota(jnp.int32, (1, LANES), 1) < HEAD_DIM

    for rr, s in ((rr, s) for rr in range(q_ref.shape[1]) for s in range(tq // ATTN_BLOCK)):
        r0 = s * ATTN_BLOCK
        qs = q_ref[0, rr, pl.ds(r0, ATTN_BLOCK), :] * (HEAD_DIM ** -0.5)
        ks = kext_ref[rr, pl.ds(r0, 2 * ATTN_BLOCK), :]
        vs = vext_ref[rr, pl.ds(r0, 2 * ATTN_BLOCK), :]
        bidx = jnp.where(n == 0, 0, 1) if s == 0 else 1
        for p in range(GROUP_WIDTH // LANES):
            cols = slice(p * LANES, (p + 1) * LANES)
            qp, kp, vp = qs[:, cols], ks[:, cols], vs[:, cols]
            q2 = jnp.concatenate([jnp.where(low_half, qp, jnp.zeros_like(qp)),
                                  jnp.where(low_half, jnp.zeros_like(qp), qp)], axis=0)
            bias2 = jnp.concatenate([bias_ref[bidx, 2 * p], bias_ref[bidx, 2 * p + 1]], axis=0)
            sc = lax.dot_general(q2, kp, NT_DIMS, preferred_element_type=F32) + bias2
            m = jnp.max(sc, axis=1, keepdims=True)
            pe = jnp.exp(sc - m).astype(BF16)
            nd = jnp.dot(pe, jnp.concatenate([vp, jnp.ones_like(vp)], axis=1), preferred_element_type=F32)
            num, den = nd[:, :LANES], nd[:, LANES:]
            o2 = num / den
            l2 = m + jnp.log(den)
            outs = [o2[:ATTN_BLOCK], o2[ATTN_BLOCK:]]
            lses = [l2[:ATTN_BLOCK], l2[ATTN_BLOCK:]]
            o_ref[0, rr, pl.ds(r0, ATTN_BLOCK), cols] = jnp.where(low_half, outs[0], outs[1])
            l_ref[0, rr, pl.ds(r0, ATTN_BLOCK), cols] = jnp.where(low_half, lses[0], lses[1])


def _attention(qkv, bias_g, *, col_q, col_k, col_v, rows):
    b, dilation, seg, _ = qkv.shape
    tq = min(rows, seg)
    res = min(dilation, rows // tq)
    blocks_per_tile = tq // ATTN_BLOCK

    def cur(col):
        return pl.BlockSpec((1, res, tq, GROUP_WIDTH), lambda bi, r, n: (bi, r, n, col))

    def prev(col):
        return pl.BlockSpec((1, res, ATTN_BLOCK, GROUP_WIDTH),
                            lambda bi, r, n: (bi, r, jnp.maximum(n * blocks_per_tile - 1, 0), col))

    out_spec = pl.BlockSpec((1, res, tq, GROUP_WIDTH), lambda bi, r, n: (bi, r, n, 0))
    out_sds = jax.ShapeDtypeStruct((b, dilation, seg, GROUP_WIDTH), F32)
    return pl.pallas_call(
        functools.partial(_attention_kernel, tq=tq),
        grid=(b, dilation // res, seg // tq),
        in_specs=[cur(col_q), prev(col_k), cur(col_k), prev(col_v), cur(col_v),
                  pl.BlockSpec(bias_g.shape, lambda bi, r, n: (0, 0, 0, 0))],
        out_specs=[out_spec, out_spec],
        out_shape=[out_sds, out_sds],
        scratch_shapes=[pltpu.VMEM((res, tq + ATTN_BLOCK, GROUP_WIDTH), BF16),
                        pltpu.VMEM((res, tq + ATTN_BLOCK, GROUP_WIDTH), BF16)],
        compiler_params=_compiler_params(("parallel", "parallel", "arbitrary")),
        name=f"attention_d{dilation}",
    )(qkv, qkv, qkv, qkv, qkv, bias_g)


def _t5_causal_bucket(dist):
    max_exact = N_REL_BUCKETS // 2
    d_f = jnp.maximum(dist, 1).astype(F32)
    large = max_exact + (jnp.log(d_f / max_exact) / math.log(REL_MAX_DISTANCE / max_exact)
                         * (N_REL_BUCKETS - max_exact)).astype(jnp.int32)
    large = jnp.minimum(large, N_REL_BUCKETS - 1)
    return jnp.where(dist < max_exact, dist, large)


def _attention_bias_tables(rel_bias):
    q, q2 = ATTN_BLOCK, 2 * ATTN_BLOCK
    steps = jnp.arange(q2) - (q - 1)
    in_prev = (jnp.arange(q2) < q)[None, None, :]
    tables = []
    for g, (window, dil) in enumerate(DILATED_GROUPS):
        n_steps = window // dil
        assert n_steps <= q
        rel_g = rel_bias[:, g * HEADS_PER_GROUP:(g + 1) * HEADS_PER_GROUP].astype(F32)
        vals = rel_g[_t5_causal_bucket(jnp.clip(steps, 0, n_steps) * dil)]
        vec = jnp.where(((steps >= 0) & (steps <= n_steps))[:, None], vals, -jnp.inf).T
        skew = jnp.tile(vec, (1, q + 1))[:, :q * (q2 + 1)].reshape(HEADS_PER_GROUP, q, q2 + 1)[:, :, :q2]
        rest = skew[:, :, ::-1]
        first = jnp.where(in_prev, -jnp.inf, rest)
        tables.append(jnp.stack([first, rest]))
    return jnp.stack(tables)


LOG2E = math.log2(math.e)


def _conv_silu(slab_ref, c, w_ref, b_ref):
    q = slab_ref.shape[1] - SUBLANES
    cols = slice(c * LANES, (c + 1) * LANES)
    acc = b_ref[:, cols] + slab_ref[c, pl.ds(SUBLANES, q), :] * w_ref[CONV_WIDTH - 1:CONV_WIDTH, cols]
    for s in range(1, CONV_WIDTH):
        acc = acc + slab_ref[c, pl.ds(SUBLANES - s, q), :] * w_ref[CONV_WIDTH - 1 - s:CONV_WIDTH - s, cols]
    return acc * _sigmoid(acc)


def _ssd_decays(r0, dt_ref, dtb_ref, alog_ref):
    q = SSD_CHUNK
    x_dt = dt_ref[0, r0:r0 + q, :] + dtb_ref[...]
    dt = jnp.maximum(x_dt, 0.0) + jnp.log(1.0 + jnp.exp(-jnp.abs(x_dt)))
    d_a = dt * (-jnp.exp(alog_ref[...]))
    ri = lax.broadcasted_iota(jnp.int32, (q, q), 0)
    ci = lax.broadcasted_iota(jnp.int32, (q, q), 1)
    causal_t = ri <= ci
    tril = jnp.where(ri >= ci, 1.0, 0.0).astype(BF16)
    d1 = d_a.astype(BF16)
    r1 = d_a - d1.astype(F32)
    d2 = r1.astype(BF16)
    d3 = (r1 - d2.astype(F32)).astype(BF16)
    la = (jnp.dot(tril, d1, preferred_element_type=F32) + jnp.dot(tril, d2, preferred_element_type=F32)
          + jnp.dot(tril, d3, preferred_element_type=F32))
    la = la * LOG2E
    la_t = la.T
    dt_t = dt.T
    last = jnp.broadcast_to(la_t[:, q - 1:q], (q, q))
    w_t = jnp.exp2(last - la_t) * dt_t
    state_decay = jnp.exp2(last)
    return causal_t, la, la_t, dt_t, w_t, state_decay


def _ssd_chunk_stages(r0, decays, xs_ref, bc_ref, z_ref, cw_ref, cb_ref, dskip_ref, nw_ref,
                      ssm_ref, state_ref, slab_ref):
    causal_t, la, la_t, dt_t, w_t, state_decay = decays
    q = SSD_CHUNK
    rq = slice(r0, r0 + q)
    gw = HEADS_PER_SSM_GROUP * SSM_HEAD_DIM
    xs_slabs = xs_ref.shape[3] // LANES
    slabs_per_group = gw // LANES

    for c in range(xs_slabs):
        slab_ref[c, SUBLANES:, :] = xs_ref[0, 0, rq, c * LANES:(c + 1) * LANES].astype(F32)
    for c in range(bc_ref.shape[3] // LANES):
        slab_ref[xs_slabs + c, SUBLANES:, :] = bc_ref[0, 0, rq, c * LANES:(c + 1) * LANES].astype(F32)

    for g in range(N_SSM_GROUPS):
        bm = _conv_silu(slab_ref, xs_slabs + g, cw_ref, cb_ref)
        cm = _conv_silu(slab_ref, xs_slabs + N_SSM_GROUPS + g, cw_ref, cb_ref)
        bm16 = bm.astype(BF16)
        cb_t = lax.dot_general(bm16, cm.astype(BF16), NT_DIMS, preferred_element_type=F32)
        cm_t = cm.T
        xs_g = jnp.concatenate([_conv_silu(slab_ref, g * slabs_per_group + c, cw_ref, cb_ref)
                                for c in range(slabs_per_group)], axis=1)
        xs_t = xs_g.T
        yield
        y_t = []
        for h in range(HEADS_PER_SSM_GROUP):
            hh = g * HEADS_PER_SSM_GROUP + h
            rows = slice(hh * SSM_HEAD_DIM, (hh + 1) * SSM_HEAD_DIM)
            la_i = la_t[hh:hh + 1, :]
            la_j = jnp.broadcast_to(la[:, hh:hh + 1], (q, q))
            decay_t = jnp.exp2(jnp.where(causal_t, la_i - la_j, -jnp.inf))
            rhs = jnp.concatenate([(cb_t * decay_t).astype(BF16), (cm_t * jnp.exp2(la_i)).astype(BF16)], axis=0)
            x_h = xs_t[h * SSM_HEAD_DIM:(h + 1) * SSM_HEAD_DIM]
            state = state_ref[rows, :]
            lhs = jnp.concatenate([(x_h * dt_t[hh:hh + 1, :]).astype(BF16), state.astype(BF16)], axis=1)
            y_t.append(jnp.dot(lhs, rhs, preferred_element_type=F32))
            if (h + 1) % SSD_HEADS_PER_STAGE == 0:
                yield
        heads = range(g * HEADS_PER_SSM_GROUP, (g + 1) * HEADS_PER_SSM_GROUP)
        grows = slice(heads[0] * SSM_HEAD_DIM, (heads[-1] + 1) * SSM_HEAD_DIM)
        def scale_heads(x, t):
            hp = x.reshape(HEADS_PER_SSM_GROUP, SSM_HEAD_DIM, q) * t[heads[0]:heads[-1] + 1][:, None, :]
            return hp.reshape(x.shape)
        upd = jnp.dot(scale_heads(xs_t, w_t).astype(BF16), bm16, preferred_element_type=F32)
        state_ref[grows, :] = scale_heads(state_ref[grows, :], state_decay) + upd
        yield
        y = jnp.concatenate(y_t, axis=0).T
        cols = slice(g * gw, (g + 1) * gw)
        y = y + xs_g * dskip_ref[:, cols]
        z = z_ref[0, 0, rq, cols].astype(F32)
        yg = y * (z * _sigmoid(z))
        yg = yg * lax.rsqrt(jnp.mean(yg * yg, axis=-1, keepdims=True) + EPS)
        ssm_ref[rq, cols] = (yg * nw_ref[:, cols]).astype(ssm_ref.dtype)
        yield

    slab_ref[:, 0:SUBLANES, :] = slab_ref[:, q:q + SUBLANES, :]


def _merge_ffn_stages(attn_refs, ssm_ref, gate_ref, h_ref, wa_ref, ws_ref, wo_ref, nat_ref,
                      nw_ref, wgu_ref, wd_ref, fw_ref, out_ref, *, ffn_chunk, final_norm):
    tb = h_ref.shape[1]
    n_slabs = GROUP_WIDTH // LANES
    for k, src in enumerate(attn_refs):
        dil = src.shape[1]
        for r in range(dil if dil > 1 else 0):
            for c in range(n_slabs):
                nat_ref[k, c, pl.ds(r, tb // dil, stride=dil), :] = src[0, r, :, c * LANES:(c + 1) * LANES]

    def natural(k, c):
        src = attn_refs[k]
        return src[0, 0, :, c * LANES:(c + 1) * LANES] if src.shape[1] == 1 else nat_ref[k, c]

    slabs = []
    for c in range(n_slabs):
        o = [natural(2 * g, c) for g in range(N_DIL)]
        l = [natural(2 * g + 1, c) for g in range(N_DIL)]
        mx = functools.reduce(jnp.maximum, l)
        e = [jnp.exp(lg - mx) for lg in l]
        num = functools.reduce(jnp.add, [eg * og for eg, og in zip(e, o)])
        slabs.append((num / functools.reduce(jnp.add, e)).astype(BF16))
    attn = jnp.concatenate(slabs, axis=1)
    yield
    a = jnp.dot(attn, wa_ref[...], preferred_element_type=F32)
    yield
    s = jnp.dot(ssm_ref[...], ws_ref[...], preferred_element_type=F32)
    d = a.shape[1]
    gates = _sigmoid(gate_ref[0, 0].astype(F32))
    merged = (gates[:, :d] * a + gates[:, d:] * s).astype(BF16)
    yield
    acc = h_ref[0] + jnp.dot(merged, wo_ref[...], preferred_element_type=F32)
    xn = _rmsnorm(acc, nw_ref[...]).astype(BF16)
    yield
    d_ff = wd_ref.shape[0]
    chunks = [slice(c0, c0 + ffn_chunk) for c0 in range(0, d_ff, ffn_chunk)]
    gate_up = None
    for c, cols in enumerate(chunks + [None]):
        prev_cols, prev_gate_up = (chunks[c - 1], gate_up) if c > 0 else (None, None)
        if cols is not None:
            up_cols = slice(d_ff + cols.start, d_ff + cols.stop)
            gate = jnp.dot(xn, wgu_ref[:, cols], preferred_element_type=F32)
            yield
            gate_up = (gate, jnp.dot(xn, wgu_ref[:, up_cols], preferred_element_type=F32))
            yield
        if prev_gate_up is not None:
            act = (prev_gate_up[0] * _sigmoid(prev_gate_up[0]) * prev_gate_up[1]).astype(BF16)
            acc = acc + jnp.dot(act, wd_ref[prev_cols, :], preferred_element_type=F32)
            yield
    out_ref[0] = _rmsnorm(acc, fw_ref[...]) if final_norm else acc


N_TAIL_WEIGHTS = 13


def _tail_kernel(*refs, blocks_per_seq, ffn_chunk, ssd_stages_per_matmul, final_norm):
    attn_refs = refs[:2 * N_DIL]
    gate_ref, h_ref, xs_ref, bc_ref, z_ref, dt_ref = refs[2 * N_DIL:2 * N_DIL + 6]
    weight_refs = refs[2 * N_DIL + 6:2 * N_DIL + 6 + N_TAIL_WEIGHTS]
    (wa_ref, ws_ref, wo_ref, n2_ref, wgu_ref, wd_ref, fw_ref,
     cw_ref, cb_ref, dtb_ref, alog_ref, dskip_ref, nw_ref) = [r.at[0] for r in weight_refs]
    out_ref, state_ref, slab_ref, ssm_ref, new_ref, nat_ref = refs[2 * N_DIL + 6 + N_TAIL_WEIGHTS:]

    @pl.when(lax.rem(pl.program_id(0), blocks_per_seq) == 0)
    def _():
        state_ref[...] = jnp.zeros_like(state_ref)
        slab_ref[:, 0:SUBLANES, :] = jnp.zeros((slab_ref.shape[0], SUBLANES, LANES), F32)

    @pl.when(pl.program_id(0) == 0)
    def _():
        ssm_ref[...] = jnp.zeros_like(ssm_ref)

    def ssd_stages():
        for r0 in range(0, xs_ref.shape[2], SSD_CHUNK):
            decays = _ssd_decays(r0, dt_ref, dtb_ref, alog_ref)
            yield
            yield from _ssd_chunk_stages(r0, decays, xs_ref, bc_ref, z_ref, cw_ref, cb_ref, dskip_ref, nw_ref,
                                         new_ref, state_ref, slab_ref)

    matmuls = _merge_ffn_stages(attn_refs, ssm_ref, gate_ref, h_ref, wa_ref, ws_ref, wo_ref, nat_ref,
                                n2_ref, wgu_ref, wd_ref, fw_ref, out_ref, ffn_chunk=ffn_chunk, final_norm=final_norm)
    turns = [(matmuls, 1), (ssd_stages(), ssd_stages_per_matmul)]
    live = [gen for gen, _ in turns]
    while live:
        for gen, reps in turns:
            for _ in range(reps if gen in live else 0):
                if next(gen, StopIteration) is StopIteration:
                    live.remove(gen)
                    break
    ssm_ref[...] = new_ref[...]


def _tail(attn_outs, gates, xs, bc, z, dt3, h3, weights, *, layer, tb, ffn_chunk, ssd_stages_per_matmul,
          final_norm):
    b, s, d = h3.shape
    assert len(weights) == N_TAIL_WEIGHTS
    conv_ch = weights[7].shape[-1]
    d_inner = xs.shape[-1]
    n_blocks = s // tb

    def cur(k):
        return jnp.divmod(jnp.minimum(k, b * n_blocks - 1), n_blocks)

    def prev(k):
        return jnp.divmod(jnp.maximum(k - 1, 0), n_blocks)

    def rows3(width, block_index):
        def index_map(k):
            bi, blk = block_index(k)
            return bi, blk, 0
        return pl.BlockSpec((1, tb, width), index_map)

    def rows4(arr, block_index):
        dil = arr.shape[1]

        def index_map(k):
            bi, blk = block_index(k)
            return bi, 0, blk, 0
        return pl.BlockSpec((1, dil, tb // dil, arr.shape[-1]), index_map)

    flat = [a for pair in attn_outs for a in pair]
    in_specs = [rows4(a, prev) for a in flat] + [
        rows4(gates, prev), rows3(d, prev), rows4(xs, cur), rows4(bc, cur), rows4(z, cur), rows3(LANES, cur),
    ] + [_layer_spec(w, min(layer, w.shape[0] - 1)) for w in weights]
    return pl.pallas_call(
        functools.partial(_tail_kernel, blocks_per_seq=n_blocks, ffn_chunk=ffn_chunk,
                          ssd_stages_per_matmul=ssd_stages_per_matmul, final_norm=final_norm),
        grid=(b * n_blocks + 1,),
        in_specs=in_specs,
        out_specs=rows3(d, prev),
        out_shape=jax.ShapeDtypeStruct((b, s, d), F32),
        scratch_shapes=[pltpu.VMEM((N_SSM_GROUPS * HEADS_PER_SSM_GROUP * SSM_HEAD_DIM, D_STATE), F32),
                        pltpu.VMEM((conv_ch // LANES, SUBLANES + SSD_CHUNK, LANES), F32),
                        pltpu.VMEM((tb, d_inner), BF16),
                        pltpu.VMEM((tb, d_inner), BF16),
                        pltpu.VMEM((len(flat), GROUP_WIDTH // LANES, tb, LANES), F32)],
        compiler_params=_compiler_params(("arbitrary",)),
        name="layer_tail",
    )(*flat, gates, h3, xs, bc, z, dt3, *weights)


def kernel(x, norm1_w, w_in, conv_w, conv_b, dt_bias, a_log, d_skip, ssm_norm_w, w_attn_branch, w_ssm_branch,
           w_out, norm2_w, w_ffn_in, w_ffn_out, rel_bias, final_norm_w):
    b, s, d = x.shape
    depth = w_in.shape[0]
    d_inner = ssm_norm_w.shape[1]
    n_heads = dt_bias.shape[1]
    bc_w = 2 * N_SSM_GROUPS * D_STATE
    assert d_inner == N_SSM_GROUPS * HEADS_PER_SSM_GROUP * SSM_HEAD_DIM and n_heads <= LANES
    assert s % (DILATED_GROUPS[-1][1] * ATTN_BLOCK) == 0 and s % SSD_CHUNK == 0 and DILATED_GROUPS[0][1] == 1

    o_z = 3 * ATTN_WIDTH
    o_xbc = o_z + d_inner
    o_dt = o_xbc + d_inner + bc_w
    o_gate = o_dt + n_heads
    assert o_dt % LANES == 0 and o_dt + LANES <= w_in.shape[2]
    w_in16 = w_in.astype(BF16)
    w_gates16 = w_in[:, :, o_gate:].astype(BF16)
    qkv_start = lambda part, g: part * ATTN_WIDTH + g * GROUP_WIDTH
    main_weights = [(w_in16, None, 0), (w_gates16, None, 0)]
    main_outputs = ((1, ((0, o_z, d_inner),)), (1, ((1, 0, 2 * d),)), (1, ((0, o_xbc, d_inner),)),
                    (1, ((0, o_xbc + d_inner, bc_w),)),
                    (1, tuple((0, qkv_start(part, 0), GROUP_WIDTH) for part in range(3))))
    dilated = [(g, dil) for g, (_, dil) in enumerate(DILATED_GROUPS) if dil > 1]
    dilated_weights = [(w_in16, GROUP_WIDTH, qkv_start(part, g) // GROUP_WIDTH) for g, _ in dilated for part in range(3)]
    dilated_outputs = tuple((dil, tuple((3 * n + part, 0, GROUP_WIDTH) for part in range(3)))
                            for n, (_, dil) in enumerate(dilated))

    pad_h = ((0, 0), (0, 0), (0, LANES - n_heads))
    row3 = lambda a: a[:, None, :]
    tail_weights = (w_attn_branch.astype(BF16), w_ssm_branch.astype(BF16), w_out.astype(BF16), row3(norm2_w),
                    w_ffn_in.astype(BF16), w_ffn_out.astype(BF16), final_norm_w[None, None, :],
                    conv_w, row3(conv_b), jnp.pad(row3(dt_bias), pad_h), jnp.pad(row3(a_log), pad_h),
                    row3(jnp.repeat(d_skip, SSM_HEAD_DIM, axis=1)), row3(ssm_norm_w))
    norm1_w3 = row3(norm1_w)
    bias_tables = _attention_bias_tables(rel_bias)

    h = x
    for layer in range(depth):
        z, gates, xs, bc, qkv0, dt = _in_proj(h, norm1_w3, main_weights, main_outputs, (0, o_dt, LANES),
                                              layer=layer, tm=IN_PROJ_ROWS, chunk=IN_PROJ_CHUNK, name="in_proj")
        qkv_dilated = _in_proj(h, norm1_w3, dilated_weights, dilated_outputs, layer=layer,
                               tm=IN_PROJ_DILATED_ROWS, chunk=IN_PROJ_CHUNK, name="in_proj_dilated")
        qkvs = {0: qkv0, **{g: arr for (g, _), arr in zip(dilated, qkv_dilated)}}
        attn_outs = [_attention(qkvs[g], bias_tables[g], col_q=0, col_k=1, col_v=2, rows=ATTN_ROWS)
                     for g in range(N_DIL)]
        h = _tail(attn_outs, gates, xs, bc, z, dt, h, tail_weights, layer=layer, tb=TAIL_ROWS,
                  ffn_chunk=FFN_CHUNK, ssd_stages_per_matmul=SSD_STAGES_PER_MATMUL,
                  final_norm=(layer == depth - 1))
    return h
```

```python
import functools
import math

import jax
import jax.numpy as jnp
from jax import lax
from jax.experimental import pallas as pl
from jax.experimental.pallas import tpu as pltpu

HEAD_DIM = 64
DILATED_GROUPS = ((128, 1), (512, 4), (2048, 16))
N_DIL = len(DILATED_GROUPS)
HEADS_PER_GROUP = 8
GROUP_WIDTH = HEADS_PER_GROUP * HEAD_DIM
ATTN_WIDTH = N_DIL * GROUP_WIDTH
ATTN_BLOCK = 128
N_REL_BUCKETS = 32
REL_MAX_DISTANCE = 2048
SSM_HEAD_DIM = 64
N_SSM_GROUPS = 4
HEADS_PER_SSM_GROUP = 8
D_STATE = 128
CONV_WIDTH = 4
SSD_CHUNK = 128
EPS = 1e-6

LANES = 128
SUBLANES = 8
VMEM_LIMIT_BYTES = 60000 * 1024

IN_PROJ_ROWS = 512
IN_PROJ_DILATED_ROWS = 1024
IN_PROJ_CHUNK = 1024
ATTN_ROWS = 2048
TAIL_ROWS = 256
FFN_CHUNK = 256
SSD_STAGES_PER_MATMUL = 1
SSD_HEADS_PER_STAGE = 3

BF16 = jnp.bfloat16
F32 = jnp.float32
NT_DIMS = (((1,), (1,)), ((), ()))


def _compiler_params(semantics):
    return pltpu.CompilerParams(dimension_semantics=semantics, vmem_limit_bytes=VMEM_LIMIT_BYTES)


def _rmsnorm(x, w):
    return x * lax.rsqrt(jnp.mean(x * x, axis=-1, keepdims=True) + EPS) * w


def _sigmoid(x):
    return 1.0 / (1.0 + jnp.exp(-x))


PERM_BLOCK = 256


def _in_proj_kernel(x_ref, nw_ref, *rest, outputs, dt_piece, chunk):
    n_w = 1 + max(p[0] for _, pieces in outputs for p in pieces)
    w_refs = [r.at[0] for r in rest[:n_w]]
    out_refs = rest[n_w:n_w + len(outputs)]
    rest = rest[n_w + len(outputs):]
    if dt_piece is not None:
        dt_ref, rest = rest[0], rest[1:]
    dilations = sorted({dil for dil, _ in outputs})
    xn_refs = dict(zip(dilations, rest))
    nw_ref = nw_ref.at[0]
    tm = x_ref.shape[1]
    xn = _rmsnorm(x_ref[0], nw_ref[...]).astype(BF16)
    if dt_piece is not None:
        src, c0, width = dt_piece
        dt_ref[0] = jnp.dot(xn, w_refs[src][:, c0:c0 + width], preferred_element_type=F32)
    for dilation, xn_ref in xn_refs.items():
        if dilation == 1:
            xn_ref[...] = xn
            continue
        per = PERM_BLOCK // dilation
        i = lax.broadcasted_iota(jnp.int32, (PERM_BLOCK, PERM_BLOCK), 0)
        k = lax.broadcasted_iota(jnp.int32, (PERM_BLOCK, PERM_BLOCK), 1)
        src = (i & (per - 1)) * dilation + (i >> (per.bit_length() - 1))
        perm = jnp.where(k == src, 1.0, 0.0).astype(BF16)
        for u in range(tm // PERM_BLOCK):
            y = jnp.dot(perm, xn[u * PERM_BLOCK:(u + 1) * PERM_BLOCK], preferred_element_type=F32).astype(BF16)
            for r in range(dilation):
                dst = r * (tm // dilation) + u * per
                xn_ref[dst:dst + per, :] = y[r * per:(r + 1) * per]
    for out_ref, (dilation, pieces) in zip(out_refs, outputs):
        col = 0
        for src, start, width in pieces:
            for c0 in range(0, width, chunk):
                cw = min(chunk, width - c0)
                res = jnp.dot(xn_refs[dilation][...], w_refs[src][:, start + c0:start + c0 + cw],
                              preferred_element_type=F32)
                out_ref[0, :, :, col:col + cw] = res.astype(out_ref.dtype).reshape(dilation, tm // dilation, cw)
                col += cw


def _layer_spec(arr, layer, block=None, col_block=0):
    shape = (1,) + tuple(arr.shape[1:-1]) + (block or arr.shape[-1],)
    index = (layer,) + (0,) * (arr.ndim - 2) + (col_block,)
    return pl.BlockSpec(shape, lambda *_: index, pipeline_mode=pl.Buffered(1))


def _in_proj(h3, norm_w, weights, outputs, dt_piece=None, *, layer, tm, chunk, name):
    b, s, d = h3.shape
    in_specs = [pl.BlockSpec((1, tm, d), lambda bi, i: (bi, i, 0)), _layer_spec(norm_w, layer)]
    in_specs += [_layer_spec(arr, layer, blk, cb) for arr, blk, cb in weights]
    shapes = [(dil, sum(p[2] for p in pieces)) for dil, pieces in outputs]
    out_specs = [pl.BlockSpec((1, dil, tm // dil, n), lambda bi, i: (bi, 0, i, 0)) for dil, n in shapes]
    out_shape = [jax.ShapeDtypeStruct((b, dil, s // dil, n), BF16) for dil, n in shapes]
    if dt_piece is not None:
        out_specs.append(pl.BlockSpec((1, tm, LANES), lambda bi, i: (bi, i, 0)))
        out_shape.append(jax.ShapeDtypeStruct((b, s, LANES), F32))
    return pl.pallas_call(
        functools.partial(_in_proj_kernel, outputs=outputs, dt_piece=dt_piece, chunk=chunk),
        grid=(b, s // tm),
        in_specs=in_specs,
        out_specs=out_specs,
        out_shape=out_shape,
        scratch_shapes=[pltpu.VMEM((tm, d), BF16) for _ in {dil for dil, _ in outputs}],
        compiler_params=_compiler_params(("parallel", "parallel")),
        name=name,
    )(h3, norm_w, *[arr for arr, _, _ in weights])


def _attention_kernel(q_ref, kp_ref, kc_ref, vp_ref, vc_ref, bias_ref, o_ref, l_ref, kext_ref, vext_ref, *, tq):
    n = pl.program_id(2)
    kext_ref[:, 0:ATTN_BLOCK] = kp_ref[0]
    kext_ref[:, ATTN_BLOCK:] = kc_ref[0]
    vext_ref[:, 0:ATTN_BLOCK] = vp_ref[0]
    vext_ref[:, ATTN_BLOCK:] = vc_ref[0]
    low_half = lax.broadcasted_iota(jnp.int32, (1, LANES), 1) < HEAD_DIM

    for rr, s in ((rr, s) for rr in range(q_ref.shape[1]) for s in range(tq // ATTN_BLOCK)):
        r0 = s * ATTN_BLOCK
        qs = q_ref[0, rr, pl.ds(r0, ATTN_BLOCK), :] * (HEAD_DIM ** -0.5)
        ks = kext_ref[rr, pl.ds(r0, 2 * ATTN_BLOCK), :]
        vs = vext_ref[rr, pl.ds(r0, 2 * ATTN_BLOCK), :]
        bidx = jnp.where(n == 0, 0, 1) if s == 0 else 1
        for p in range(GROUP_WIDTH // LANES):
            cols = slice(p * LANES, (p + 1) * LANES)
            qp, kp, vp = qs[:, cols], ks[:, cols], vs[:, cols]
            q2 = jnp.concatenate([jnp.where(low_half, qp, jnp.zeros_like(qp)),
                                  jnp.where(low_half, jnp.zeros_like(qp), qp)], axis=0)
            bias2 = jnp.concatenate([bias_ref[bidx, 2 * p], bias_ref[bidx, 2 * p + 1]], axis=0)
            sc = lax.dot_general(q2, kp, NT_DIMS, preferred_element_type=F32) + bias2
            m = jnp.max(sc, axis=1, keepdims=True)
            pe = jnp.exp(sc - m).astype(BF16)
            nd = jnp.dot(pe, jnp.concatenate([vp, jnp.ones_like(vp)], axis=1), preferred_element_type=F32)
            num, den = nd[:, :LANES], nd[:, LANES:]
            o2 = num / den
            l2 = m + jnp.log(den)
            outs = [o2[:ATTN_BLOCK], o2[ATTN_BLOCK:]]
            lses = [l2[:ATTN_BLOCK], l2[ATTN_BLOCK:]]
            o_ref[0, rr, pl.ds(r0, ATTN_BLOCK), cols] = jnp.where(low_half, outs[0], outs[1])
            l_ref[0, rr, pl.ds(r0, ATTN_BLOCK), cols] = jnp.where(low_half, lses[0], lses[1])


def _attention(qkv, bias_g, *, col_q, col_k, col_v, rows):
    b, dilation, seg, _ = qkv.shape
    tq = min(rows, seg)
    res = min(dilation, rows // tq)
    blocks_per_tile = tq // ATTN_BLOCK

    def cur(col):
        return pl.BlockSpec((1, res, tq, GROUP_WIDTH), lambda bi, r, n: (bi, r, n, col))

    def prev(col):
        return pl.BlockSpec((1, res, ATTN_BLOCK, GROUP_WIDTH),
                            lambda bi, r, n: (bi, r, jnp.maximum(n * blocks_per_tile - 1, 0), col))

    out_spec = pl.BlockSpec((1, res, tq, GROUP_WIDTH), lambda bi, r, n: (bi, r, n, 0))
    out_sds = jax.ShapeDtypeStruct((b, dilation, seg, GROUP_WIDTH), F32)
    return pl.pallas_call(
        functools.partial(_attention_kernel, tq=tq),
        grid=(b, dilation // res, seg // tq),
        in_specs=[cur(col_q), prev(col_k), cur(col_k), prev(col_v), cur(col_v),
                  pl.BlockSpec(bias_g.shape, lambda bi, r, n: (0, 0, 0, 0))],
        out_specs=[out_spec, out_spec],
        out_shape=[out_sds, out_sds],
        scratch_shapes=[pltpu.VMEM((res, tq + ATTN_BLOCK, GROUP_WIDTH), BF16),
                        pltpu.VMEM((res, tq + ATTN_BLOCK, GROUP_WIDTH), BF16)],
        compiler_params=_compiler_params(("parallel", "parallel", "arbitrary")),
        name=f"attention_d{dilation}",
    )(qkv, qkv, qkv, qkv, qkv, bias_g)


def _t5_causal_bucket(dist):
    max_exact = N_REL_BUCKETS // 2
    d_f = jnp.maximum(dist, 1).astype(F32)
    large = max_exact + (jnp.log(d_f / max_exact) / math.log(REL_MAX_DISTANCE / max_exact)
                         * (N_REL_BUCKETS - max_exact)).astype(jnp.int32)
    large = jnp.minimum(large, N_REL_BUCKETS - 1)
    return jnp.where(dist < max_exact, dist, large)


def _attention_bias_tables(rel_bias):
    q, q2 = ATTN_BLOCK, 2 * ATTN_BLOCK
    steps = jnp.arange(q2) - (q - 1)
    in_prev = (jnp.arange(q2) < q)[None, None, :]
    tables = []
    for g, (window, dil) in enumerate(DILATED_GROUPS):
        n_steps = window // dil
        assert n_steps <= q
        rel_g = rel_bias[:, g * HEADS_PER_GROUP:(g + 1) * HEADS_PER_GROUP].astype(F32)
        vals = rel_g[_t5_causal_bucket(jnp.clip(steps, 0, n_steps) * dil)]
        vec = jnp.where(((steps >= 0) & (steps <= n_steps))[:, None], vals, -jnp.inf).T
        skew = jnp.tile(vec, (1, q + 1))[:, :q * (q2 + 1)].reshape(HEADS_PER_GROUP, q, q2 + 1)[:, :, :q2]
        rest = skew[:, :, ::-1]
        first = jnp.where(in_prev, -jnp.inf, rest)
        tables.append(jnp.stack([first, rest]))
    return jnp.stack(tables)


LOG2E = math.log2(math.e)


def _conv_silu(slab_ref, c, w_ref, b_ref):
    q = slab_ref.shape[1] - SUBLANES
    cols = slice(c * LANES, (c + 1) * LANES)
    acc = b_ref[:, cols] + slab_ref[c, pl.ds(SUBLANES, q), :] * w_ref[CONV_WIDTH - 1:CONV_WIDTH, cols]
    for s in range(1, CONV_WIDTH):
        acc = acc + slab_ref[c, pl.ds(SUBLANES - s, q), :] * w_ref[CONV_WIDTH - 1 - s:CONV_WIDTH - s, cols]
    return acc * _sigmoid(acc)


def _ssd_decays(r0, dt_ref, dtb_ref, alog_ref):
    q = SSD_CHUNK
    x_dt = dt_ref[0, r0:r0 + q, :] + dtb_ref[...]
    dt = jnp.maximum(x_dt, 0.0) + jnp.log(1.0 + jnp.exp(-jnp.abs(x_dt)))
    d_a = dt * (-jnp.exp(alog_ref[...]))
    ri = lax.broadcasted_iota(jnp.int32, (q, q), 0)
    ci = lax.broadcasted_iota(jnp.int32, (q, q), 1)
    causal_t = ri <= ci
    tril = jnp.where(ri >= ci, 1.0, 0.0).astype(BF16)
    d1 = d_a.astype(BF16)
    r1 = d_a - d1.astype(F32)
    d2 = r1.astype(BF16)
    d3 = (r1 - d2.astype(F32)).astype(BF16)
    la = (jnp.dot(tril, d1, preferred_element_type=F32) + jnp.dot(tril, d2, preferred_element_type=F32)
          + jnp.dot(tril, d3, preferred_element_type=F32))
    la = la * LOG2E
    la_t = la.T
    dt_t = dt.T
    last = jnp.broadcast_to(la_t[:, q - 1:q], (q, q))
    w_t = jnp.exp2(last - la_t) * dt_t
    state_decay = jnp.exp2(last)
    return causal_t, la, la_t, dt_t, w_t, state_decay


def _ssd_chunk_stages(r0, decays, xs_ref, bc_ref, z_ref, cw_ref, cb_ref, dskip_ref, nw_ref,
                      ssm_ref, state_ref, slab_ref):
    causal_t, la, la_t, dt_t, w_t, state_decay = decays
    q = SSD_CHUNK
    rq = slice(r0, r0 + q)
    gw = HEADS_PER_SSM_GROUP * SSM_HEAD_DIM
    xs_slabs = xs_ref.shape[3] // LANES
    slabs_per_group = gw // LANES

    for c in range(xs_slabs):
        slab_ref[c, SUBLANES:, :] = xs_ref[0, 0, rq, c * LANES:(c + 1) * LANES].astype(F32)
    for c in range(bc_ref.shape[3] // LANES):
        slab_ref[xs_slabs + c, SUBLANES:, :] = bc_ref[0, 0, rq, c * LANES:(c + 1) * LANES].astype(F32)

    for g in range(N_SSM_GROUPS):
        bm = _conv_silu(slab_ref, xs_slabs + g, cw_ref, cb_ref)
        cm = _conv_silu(slab_ref, xs_slabs + N_SSM_GROUPS + g, cw_ref, cb_ref)
        bm16 = bm.astype(BF16)
        cb_t = lax.dot_general(bm16, cm.astype(BF16), NT_DIMS, preferred_element_type=F32)
        cm_t = cm.T
        xs_g = jnp.concatenate([_conv_silu(slab_ref, g * slabs_per_group + c, cw_ref, cb_ref)
                                for c in range(slabs_per_group)], axis=1)
        xs_t = xs_g.T
        yield
        y_t = []
        for h in range(HEADS_PER_SSM_GROUP):
            hh = g * HEADS_PER_SSM_GROUP + h
            rows = slice(hh * SSM_HEAD_DIM, (hh + 1) * SSM_HEAD_DIM)
            la_i = la_t[hh:hh + 1, :]
            la_j = jnp.broadcast_to(la[:, hh:hh + 1], (q, q))
            decay_t = jnp.exp2(jnp.where(causal_t, la_i - la_j, -jnp.inf))
            rhs = jnp.concatenate([(cb_t * decay_t).astype(BF16), (cm_t * jnp.exp2(la_i)).astype(BF16)], axis=0)
            x_h = xs_t[h * SSM_HEAD_DIM:(h + 1) * SSM_HEAD_DIM]
            state = state_ref[rows, :]
            lhs = jnp.concatenate([(x_h * dt_t[hh:hh + 1, :]).astype(BF16), state.astype(BF16)], axis=1)
            y_t.append(jnp.dot(lhs, rhs, preferred_element_type=F32))
            if (h + 1) % SSD_HEADS_PER_STAGE == 0:
                yield
        heads = range(g * HEADS_PER_SSM_GROUP, (g + 1) * HEADS_PER_SSM_GROUP)
        grows = slice(heads[0] * SSM_HEAD_DIM, (heads[-1] + 1) * SSM_HEAD_DIM)
        def scale_heads(x, t):
            hp = x.reshape(HEADS_PER_SSM_GROUP, SSM_HEAD_DIM, q) * t[heads[0]:heads[-1] + 1][:, None, :]
            return hp.reshape(x.shape)
        upd = jnp.dot(scale_heads(xs_t, w_t).astype(BF16), bm16, preferred_element_type=F32)
        state_ref[grows, :] = scale_heads(state_ref[grows, :], state_decay) + upd
        yield
        y = jnp.concatenate(y_t, axis=0).T
        cols = slice(g * gw, (g + 1) * gw)
        y = y + xs_g * dskip_ref[:, cols]
        z = z_ref[0, 0, rq, cols].astype(F32)
        yg = y * (z * _sigmoid(z))
        yg = yg * lax.rsqrt(jnp.mean(yg * yg, axis=-1, keepdims=True) + EPS)
        ssm_ref[rq, cols] = (yg * nw_ref[:, cols]).astype(ssm_ref.dtype)
        yield

    slab_ref[:, 0:SUBLANES, :] = slab_ref[:, q:q + SUBLANES, :]


def _merge_ffn_stages(attn_refs, ssm_ref, gate_ref, h_ref, wa_ref, ws_ref, wo_ref, nat_ref,
                      nw_ref, wgu_ref, wd_ref, fw_ref, out_ref, *, ffn_chunk, final_norm):
    tb = h_ref.shape[1]
    n_slabs = GROUP_WIDTH // LANES
    for k, src in enumerate(attn_refs):
        dil = src.shape[1]
        for r in range(dil if dil > 1 else 0):
            for c in range(n_slabs):
                nat_ref[k, c, pl.ds(r, tb // dil, stride=dil), :] = src[0, r, :, c * LANES:(c + 1) * LANES]

    def natural(k, c):
        src = attn_refs[k]
        return src[0, 0, :, c * LANES:(c + 1) * LANES] if src.shape[1] == 1 else nat_ref[k, c]

    slabs = []
    for c in range(n_slabs):
        o = [natural(2 * g, c) for g in range(N_DIL)]
        l = [natural(2 * g + 1, c) for g in range(N_DIL)]
        mx = functools.reduce(jnp.maximum, l)
        e = [jnp.exp(lg - mx) for lg in l]
        num = functools.reduce(jnp.add, [eg * og for eg, og in zip(e, o)])
        slabs.append((num / functools.reduce(jnp.add, e)).astype(BF16))
    attn = jnp.concatenate(slabs, axis=1)
    yield
    a = jnp.dot(attn, wa_ref[...], preferred_element_type=F32)
    yield
    s = jnp.dot(ssm_ref[...], ws_ref[...], preferred_element_type=F32)
    d = a.shape[1]
    gates = _sigmoid(gate_ref[0, 0].astype(F32))
    merged = (gates[:, :d] * a + gates[:, d:] * s).astype(BF16)
    yield
    acc = h_ref[0] + jnp.dot(merged, wo_ref[...], preferred_element_type=F32)
    xn = _rmsnorm(acc, nw_ref[...]).astype(BF16)
    yield
    d_ff = wd_ref.shape[0]
    chunks = [slice(c0, c0 + ffn_chunk) for c0 in range(0, d_ff, ffn_chunk)]
    gate_up = None
    for c, cols in enumerate(chunks + [None]):
        prev_cols, prev_gate_up = (chunks[c - 1], gate_up) if c > 0 else (None, None)
        if cols is not None:
            up_cols = slice(d_ff + cols.start, d_ff + cols.stop)
            gate = jnp.dot(xn, wgu_ref[:, cols], preferred_element_type=F32)
            yield
            gate_up = (gate, jnp.dot(xn, wgu_ref[:, up_cols], preferred_element_type=F32))
            yield
        if prev_gate_up is not None:
            act = (prev_gate_up[0] * _sigmoid(prev_gate_up[0]) * prev_gate_up[1]).astype(BF16)
            acc = acc + jnp.dot(act, wd_ref[prev_cols, :], preferred_element_type=F32)
            yield
    out_ref[0] = _rmsnorm(acc, fw_ref[...]) if final_norm else acc


N_TAIL_WEIGHTS = 13


def _tail_kernel(*refs, blocks_per_seq, ffn_chunk, ssd_stages_per_matmul, final_norm):
    attn_refs = refs[:2 * N_DIL]
    gate_ref, h_ref, xs_ref, bc_ref, z_ref, dt_ref = refs[2 * N_DIL:2 * N_DIL + 6]
    weight_refs = refs[2 * N_DIL + 6:2 * N_DIL + 6 + N_TAIL_WEIGHTS]
    (wa_ref, ws_ref, wo_ref, n2_ref, wgu_ref, wd_ref, fw_ref,
     cw_ref, cb_ref, dtb_ref, alog_ref, dskip_ref, nw_ref) = [r.at[0] for r in weight_refs]
    out_ref, state_ref, slab_ref, ssm_ref, new_ref, nat_ref = refs[2 * N_DIL + 6 + N_TAIL_WEIGHTS:]

    @pl.when(lax.rem(pl.program_id(0), blocks_per_seq) == 0)
    def _():
        state_ref[...] = jnp.zeros_like(state_ref)
        slab_ref[:, 0:SUBLANES, :] = jnp.zeros((slab_ref.shape[0], SUBLANES, LANES), F32)

    @pl.when(pl.program_id(0) == 0)
    def _():
        ssm_ref[...] = jnp.zeros_like(ssm_ref)

    def ssd_stages():
        for r0 in range(0, xs_ref.shape[2], SSD_CHUNK):
            decays = _ssd_decays(r0, dt_ref, dtb_ref, alog_ref)
            yield
            yield from _ssd_chunk_stages(r0, decays, xs_ref, bc_ref, z_ref, cw_ref, cb_ref, dskip_ref, nw_ref,
                                         new_ref, state_ref, slab_ref)

    matmuls = _merge_ffn_stages(attn_refs, ssm_ref, gate_ref, h_ref, wa_ref, ws_ref, wo_ref, nat_ref,
                                n2_ref, wgu_ref, wd_ref, fw_ref, out_ref, ffn_chunk=ffn_chunk, final_norm=final_norm)
    turns = [(matmuls, 1), (ssd_stages(), ssd_stages_per_matmul)]
    live = [gen for gen, _ in turns]
    while live:
        for gen, reps in turns:
            for _ in range(reps if gen in live else 0):
                if next(gen, StopIteration) is StopIteration:
                    live.remove(gen)
                    break
    ssm_ref[...] = new_ref[...]


def _tail(attn_outs, gates, xs, bc, z, dt3, h3, weights, *, layer, tb, ffn_chunk, ssd_stages_per_matmul,
          final_norm):
    b, s, d = h3.shape
    assert len(weights) == N_TAIL_WEIGHTS
    conv_ch = weights[7].shape[-1]
    d_inner = xs.shape[-1]
    n_blocks = s // tb

    def cur(k):
        return jnp.divmod(jnp.minimum(k, b * n_blocks - 1), n_blocks)

    def prev(k):
        return jnp.divmod(jnp.maximum(k - 1, 0), n_blocks)

    def rows3(width, block_index):
        def index_map(k):
            bi, blk = block_index(k)
            return bi, blk, 0
        return pl.BlockSpec((1, tb, width), index_map)

    def rows4(arr, block_index):
        dil = arr.shape[1]

        def index_map(k):
            bi, blk = block_index(k)
            return bi, 0, blk, 0
        return pl.BlockSpec((1, dil, tb // dil, arr.shape[-1]), index_map)

    flat = [a for pair in attn_outs for a in pair]
    in_specs = [rows4(a, prev) for a in flat] + [
        rows4(gates, prev), rows3(d, prev), rows4(xs, cur), rows4(bc, cur), rows4(z, cur), rows3(LANES, cur),
    ] + [_layer_spec(w, min(layer, w.shape[0] - 1)) for w in weights]
    return pl.pallas_call(
        functools.partial(_tail_kernel, blocks_per_seq=n_blocks, ffn_chunk=ffn_chunk,
                          ssd_stages_per_matmul=ssd_stages_per_matmul, final_norm=final_norm),
        grid=(b * n_blocks + 1,),
        in_specs=in_specs,
        out_specs=rows3(d, prev),
        out_shape=jax.ShapeDtypeStruct((b, s, d), F32),
        scratch_shapes=[pltpu.VMEM((N_SSM_GROUPS * HEADS_PER_SSM_GROUP * SSM_HEAD_DIM, D_STATE), F32),
                        pltpu.VMEM((conv_ch // LANES, SUBLANES + SSD_CHUNK, LANES), F32),
                        pltpu.VMEM((tb, d_inner), BF16),
                        pltpu.VMEM((tb, d_inner), BF16),
                        pltpu.VMEM((len(flat), GROUP_WIDTH // LANES, tb, LANES), F32)],
        compiler_params=_compiler_params(("arbitrary",)),
        name="layer_tail",
    )(*flat, gates, h3, xs, bc, z, dt3, *weights)


def kernel(x, norm1_w, w_in, conv_w, conv_b, dt_bias, a_log, d_skip, ssm_norm_w, w_attn_branch, w_ssm_branch,
           w_out, norm2_w, w_ffn_in, w_ffn_out, rel_bias, final_norm_w):
    b, s, d = x.shape
    depth = w_in.shape[0]
    d_inner = ssm_norm_w.shape[1]
    n_heads = dt_bias.shape[1]
    bc_w = 2 * N_SSM_GROUPS * D_STATE
    assert d_inner == N_SSM_GROUPS * HEADS_PER_SSM_GROUP * SSM_HEAD_DIM and n_heads <= LANES
    assert s % (DILATED_GROUPS[-1][1] * ATTN_BLOCK) == 0 and s % SSD_CHUNK == 0 and DILATED_GROUPS[0][1] == 1

    o_z = 3 * ATTN_WIDTH
    o_xbc = o_z + d_inner
    o_dt = o_xbc + d_inner + bc_w
    o_gate = o_dt + n_heads
    assert o_dt % LANES == 0 and o_dt + LANES <= w_in.shape[2]
    w_in16 = w_in.astype(BF16)
    w_gates16 = w_in[:, :, o_gate:].astype(BF16)
    qkv_start = lambda part, g: part * ATTN_WIDTH + g * GROUP_WIDTH
    main_weights = [(w_in16, None, 0), (w_gates16, None, 0)]
    main_outputs = ((1, ((0, o_z, d_inner),)), (1, ((1, 0, 2 * d),)), (1, ((0, o_xbc, d_inner),)),
                    (1, ((0, o_xbc + d_inner, bc_w),)),
                    (1, tuple((0, qkv_start(part, 0), GROUP_WIDTH) for part in range(3))))
    dilated = [(g, dil) for g, (_, dil) in enumerate(DILATED_GROUPS) if dil > 1]
    dilated_weights = [(w_in16, GROUP_WIDTH, qkv_start(part, g) // GROUP_WIDTH) for g, _ in dilated for part in range(3)]
    dilated_outputs = tuple((dil, tuple((3 * n + part, 0, GROUP_WIDTH) for part in range(3)))
                            for n, (_, dil) in enumerate(dilated))

    pad_h = ((0, 0), (0, 0), (0, LANES - n_heads))
    row3 = lambda a: a[:, None, :]
    tail_weights = (w_attn_branch.astype(BF16), w_ssm_branch.astype(BF16), w_out.astype(BF16), row3(norm2_w),
                    w_ffn_in.astype(BF16), w_ffn_out.astype(BF16), final_norm_w[None, None, :],
                    conv_w, row3(conv_b), jnp.pad(row3(dt_bias), pad_h), jnp.pad(row3(a_log), pad_h),
                    row3(jnp.repeat(d_skip, SSM_HEAD_DIM, axis=1)), row3(ssm_norm_w))
    norm1_w3 = row3(norm1_w)
    bias_tables = _attention_bias_tables(rel_bias)

    h = x
    for layer in range(depth):
        z, gates, xs, bc, qkv0, dt = _in_proj(h, norm1_w3, main_weights, main_outputs, (0, o_dt, LANES),
                                              layer=layer, tm=IN_PROJ_ROWS, chunk=IN_PROJ_CHUNK, name="in_proj")
        qkv_dilated = _in_proj(h, norm1_w3, dilated_weights, dilated_outputs, layer=layer,
                               tm=IN_PROJ_DILATED_ROWS, chunk=IN_PROJ_CHUNK, name="in_proj_dilated")
        qkvs = {0: qkv0, **{g: arr for (g, _), arr in zip(dilated, qkv_dilated)}}
        attn_outs = [_attention(qkvs[g], bias_tables[g], col_q=0, col_k=1, col_v=2, rows=ATTN_ROWS)
                     for g in range(N_DIL)]
        h = _tail(attn_outs, gates, xs, bc, z, dt, h, tail_weights, layer=layer, tb=TAIL_ROWS,
                  ffn_chunk=FFN_CHUNK, ssd_stages_per_matmul=SSD_STAGES_PER_MATMUL,
                  final_norm=(layer == depth - 1))
    return h
```

```python
import functools
import math

import jax
import jax.numpy as jnp
from jax import lax
from jax.experimental import pallas as pl
from jax.experimental.pallas import tpu as pltpu

HEAD_DIM = 64
DILATED_GROUPS = ((128, 1), (512, 4), (2048, 16))
N_DIL = len(DILATED_GROUPS)
HEADS_PER_GROUP = 8
GROUP_WIDTH = HEADS_PER_GROUP * HEAD_DIM
ATTN_WIDTH = N_DIL * GROUP_WIDTH
ATTN_BLOCK = 128
N_REL_BUCKETS = 32
REL_MAX_DISTANCE = 2048
SSM_HEAD_DIM = 64
N_SSM_GROUPS = 4
HEADS_PER_SSM_GROUP = 8
D_STATE = 128
CONV_WIDTH = 4
SSD_CHUNK = 128
EPS = 1e-6

LANES = 128
SUBLANES = 8
VMEM_LIMIT_BYTES = 60000 * 1024

IN_PROJ_ROWS = 512
IN_PROJ_DILATED_ROWS = 1024
IN_PROJ_CHUNK = 1024
ATTN_ROWS = 2048
TAIL_ROWS = 256
FFN_CHUNK = 256
SSD_STAGES_PER_MATMUL = 1
SSD_HEADS_PER_STAGE = 5

BF16 = jnp.bfloat16
F32 = jnp.float32
NT_DIMS = (((1,), (1,)), ((), ()))


def _compiler_params(semantics):
    return pltpu.CompilerParams(dimension_semantics=semantics, vmem_limit_bytes=VMEM_LIMIT_BYTES)


def _rmsnorm(x, w):
    return x * lax.rsqrt(jnp.mean(x * x, axis=-1, keepdims=True) + EPS) * w


def _sigmoid(x):
    return 1.0 / (1.0 + jnp.exp(-x))


def _in_proj_kernel(x_ref, nw_ref, *rest, outputs, dt_piece, chunk):
    n_w = 1 + max(p[0] for _, pieces in outputs for p in pieces)
    w_refs = [r.at[0] for r in rest[:n_w]]
    out_refs = rest[n_w:n_w + len(outputs)]
    rest = rest[n_w + len(outputs):]
    if dt_piece is not None:
        dt_ref, rest = rest[0], rest[1:]
    dilations = sorted({dil for dil, _ in outputs})
    xn_refs = dict(zip(dilations, rest))
    nw_ref = nw_ref.at[0]
    tm = x_ref.shape[1]
    xn32 = _rmsnorm(x_ref[0], nw_ref[...])
    xn = xn32.astype(BF16)
    if dt_piece is not None:
        src, c0, width = dt_piece
        dt_ref[0] = jnp.dot(xn, w_refs[src][:, c0:c0 + width], preferred_element_type=F32)
    if dilations[-1] > 1:
        slab_ref = rest[len(dilations)]
        for c in range(slab_ref.shape[0]):
            slab_ref[c] = xn32[:, c * LANES:(c + 1) * LANES]
    for dilation, xn_ref in xn_refs.items():
        if dilation == 1:
            xn_ref[...] = xn
            continue
        per = tm // dilation
        for r in range(dilation):
            for c in range(slab_ref.shape[0]):
                rows = slab_ref[c, pl.ds(r, per, stride=dilation), :]
                xn_ref[r * per:(r + 1) * per, c * LANES:(c + 1) * LANES] = rows.astype(BF16)
    for out_ref, (dilation, pieces) in zip(out_refs, outputs):
        col = 0
        for src, start, width in pieces:
            for c0 in range(0, width, chunk):
                cw = min(chunk, width - c0)
                res = jnp.dot(xn_refs[dilation][...], w_refs[src][:, start + c0:start + c0 + cw],
                              preferred_element_type=F32)
                out_ref[0, :, :, col:col + cw] = res.astype(out_ref.dtype).reshape(dilation, tm // dilation, cw)
                col += cw


def _layer_spec(arr, layer, block=None, col_block=0):
    shape = (1,) + tuple(arr.shape[1:-1]) + (block or arr.shape[-1],)
    index = (layer,) + (0,) * (arr.ndim - 2) + (col_block,)
    return pl.BlockSpec(shape, lambda *_: index, pipeline_mode=pl.Buffered(1))


def _in_proj(h3, norm_w, weights, outputs, dt_piece=None, *, layer, tm, chunk, name):
    b, s, d = h3.shape
    in_specs = [pl.BlockSpec((1, tm, d), lambda bi, i: (bi, i, 0)), _layer_spec(norm_w, layer)]
    in_specs += [_layer_spec(arr, layer, blk, cb) for arr, blk, cb in weights]
    shapes = [(dil, sum(p[2] for p in pieces)) for dil, pieces in outputs]
    out_specs = [pl.BlockSpec((1, dil, tm // dil, n), lambda bi, i: (bi, 0, i, 0)) for dil, n in shapes]
    out_shape = [jax.ShapeDtypeStruct((b, dil, s // dil, n), BF16) for dil, n in shapes]
    if dt_piece is not None:
        out_specs.append(pl.BlockSpec((1, tm, LANES), lambda bi, i: (bi, i, 0)))
        out_shape.append(jax.ShapeDtypeStruct((b, s, LANES), F32))
    return pl.pallas_call(
        functools.partial(_in_proj_kernel, outputs=outputs, dt_piece=dt_piece, chunk=chunk),
        grid=(b, s // tm),
        in_specs=in_specs,
        out_specs=out_specs,
        out_shape=out_shape,
        scratch_shapes=[pltpu.VMEM((tm, d), BF16) for _ in {dil for dil, _ in outputs}]
        + ([pltpu.VMEM((d // LANES, tm, LANES), F32)] if any(dil > 1 for dil, _ in outputs) else []),
        compiler_params=_compiler_params(("parallel", "parallel")),
        name=name,
    )(h3, norm_w, *[arr for arr, _, _ in weights])


def _attention_kernel(q_ref, kp_ref, kc_ref, vp_ref, vc_ref, bias_ref, o_ref, l_ref, kext_ref, vext_ref, *, tq):
    n = pl.program_id(2)
    kext_ref[:, 0:ATTN_BLOCK] = kp_ref[0]
    kext_ref[:, ATTN_BLOCK:] = kc_ref[0]
    vext_ref[:, 0:ATTN_BLOCK] = vp_ref[0]
    vext_ref[:, ATTN_BLOCK:] = vc_ref[0]
    low_half = lax.broadcasted_iota(jnp.int32, (1, LANES), 1) < HEAD_DIM

    for rr, s in ((rr, s) for rr in range(q_ref.shape[1]) for s in range(tq // ATTN_BLOCK)):
        r0 = s * ATTN_BLOCK
        qs = q_ref[0, rr, pl.ds(r0, ATTN_BLOCK), :] * (HEAD_DIM ** -0.5)
        ks = kext_ref[rr, pl.ds(r0, 2 * ATTN_BLOCK), :]
        vs = vext_ref[rr, pl.ds(r0, 2 * ATTN_BLOCK), :]
        bidx = jnp.where(n == 0, 0, 1) if s == 0 else 1
        for p in range(GROUP_WIDTH // LANES):
            cols = slice(p * LANES, (p + 1) * LANES)
            qp, kp, vp = qs[:, cols], ks[:, cols], vs[:, cols]
            q2 = jnp.concatenate([jnp.where(low_half, qp, jnp.zeros_like(qp)),
                                  jnp.where(low_half, jnp.zeros_like(qp), qp)], axis=0)
            bias2 = jnp.concatenate([bias_ref[bidx, 2 * p], bias_ref[bidx, 2 * p + 1]], axis=0)
            sc = lax.dot_general(q2, kp, NT_DIMS, preferred_element_type=F32) + bias2
            m = jnp.max(sc, axis=1, keepdims=True)
            pe = jnp.exp(sc - m).astype(BF16)
            nd = jnp.dot(pe, jnp.concatenate([vp, jnp.ones_like(vp)], axis=1), preferred_element_type=F32)
            num, den = nd[:, :LANES], nd[:, LANES:]
            o2 = num / den
            l2 = m + jnp.log(den)
            outs = [o2[:ATTN_BLOCK], o2[ATTN_BLOCK:]]
            lses = [l2[:ATTN_BLOCK], l2[ATTN_BLOCK:]]
            o_ref[0, rr, pl.ds(r0, ATTN_BLOCK), cols] = jnp.where(low_half, outs[0], outs[1])
            l_ref[0, rr, pl.ds(r0, ATTN_BLOCK), cols] = jnp.where(low_half, lses[0], lses[1])


def _attention(qkv, bias_g, *, col_q, col_k, col_v, rows):
    b, dilation, seg, _ = qkv.shape
    tq = min(rows, seg)
    res = min(dilation, rows // tq)
    blocks_per_tile = tq // ATTN_BLOCK

    def cur(col):
        return pl.BlockSpec((1, res, tq, GROUP_WIDTH), lambda bi, r, n: (bi, r, n, col))

    def prev(col):
        return pl.BlockSpec((1, res, ATTN_BLOCK, GROUP_WIDTH),
                            lambda bi, r, n: (bi, r, jnp.maximum(n * blocks_per_tile - 1, 0), col))

    out_spec = pl.BlockSpec((1, res, tq, GROUP_WIDTH), lambda bi, r, n: (bi, r, n, 0))
    out_sds = jax.ShapeDtypeStruct((b, dilation, seg, GROUP_WIDTH), F32)
    return pl.pallas_call(
        functools.partial(_attention_kernel, tq=tq),
        grid=(b, dilation // res, seg // tq),
        in_specs=[cur(col_q), prev(col_k), cur(col_k), prev(col_v), cur(col_v),
                  pl.BlockSpec(bias_g.shape, lambda bi, r, n: (0, 0, 0, 0))],
        out_specs=[out_spec, out_spec],
        out_shape=[out_sds, out_sds],
        scratch_shapes=[pltpu.VMEM((res, tq + ATTN_BLOCK, GROUP_WIDTH), BF16),
                        pltpu.VMEM((res, tq + ATTN_BLOCK, GROUP_WIDTH), BF16)],
        compiler_params=_compiler_params(("parallel", "parallel", "arbitrary")),
        name=f"attention_d{dilation}",
    )(qkv, qkv, qkv, qkv, qkv, bias_g)


def _t5_causal_bucket(dist):
    max_exact = N_REL_BUCKETS // 2
    d_f = jnp.maximum(dist, 1).astype(F32)
    large = max_exact + (jnp.log(d_f / max_exact) / math.log(REL_MAX_DISTANCE / max_exact)
                         * (N_REL_BUCKETS - max_exact)).astype(jnp.int32)
    large = jnp.minimum(large, N_REL_BUCKETS - 1)
    return jnp.where(dist < max_exact, dist, large)


def _attention_bias_tables(rel_bias):
    q, q2 = ATTN_BLOCK, 2 * ATTN_BLOCK
    steps = jnp.arange(q2) - (q - 1)
    in_prev = (jnp.arange(q2) < q)[None, None, :]
    tables = []
    for g, (window, dil) in enumerate(DILATED_GROUPS):
        n_steps = window // dil
        assert n_steps <= q
        rel_g = rel_bias[:, g * HEADS_PER_GROUP:(g + 1) * HEADS_PER_GROUP].astype(F32)
        vals = rel_g[_t5_causal_bucket(jnp.clip(steps, 0, n_steps) * dil)]
        vec = jnp.where(((steps >= 0) & (steps <= n_steps))[:, None], vals, -jnp.inf).T
        skew = jnp.tile(vec, (1, q + 1))[:, :q * (q2 + 1)].reshape(HEADS_PER_GROUP, q, q2 + 1)[:, :, :q2]
        rest = skew[:, :, ::-1]
        first = jnp.where(in_prev, -jnp.inf, rest)
        tables.append(jnp.stack([first, rest]))
    return jnp.stack(tables)


LOG2E = math.log2(math.e)


def _conv_silu(slab_ref, c, w_ref, b_ref):
    q = slab_ref.shape[1] - SUBLANES
    cols = slice(c * LANES, (c + 1) * LANES)
    acc = b_ref[:, cols] + slab_ref[c, pl.ds(SUBLANES, q), :] * w_ref[CONV_WIDTH - 1:CONV_WIDTH, cols]
    for s in range(1, CONV_WIDTH):
        acc = acc + slab_ref[c, pl.ds(SUBLANES - s, q), :] * w_ref[CONV_WIDTH - 1 - s:CONV_WIDTH - s, cols]
    return acc * _sigmoid(acc)


def _ssd_decays(r0, dt_ref, dtb_ref, alog_ref):
    q = SSD_CHUNK
    x_dt = dt_ref[0, r0:r0 + q, :] + dtb_ref[...]
    dt = jnp.maximum(x_dt, 0.0) + jnp.log(1.0 + jnp.exp(-jnp.abs(x_dt)))
    d_a = dt * (-jnp.exp(alog_ref[...]))
    ri = lax.broadcasted_iota(jnp.int32, (q, q), 0)
    ci = lax.broadcasted_iota(jnp.int32, (q, q), 1)
    causal_t = ri <= ci
    tril = jnp.where(ri >= ci, 1.0, 0.0).astype(BF16)
    d1 = d_a.astype(BF16)
    r1 = d_a - d1.astype(F32)
    d2 = r1.astype(BF16)
    d3 = (r1 - d2.astype(F32)).astype(BF16)
    la = (jnp.dot(tril, d1, preferred_element_type=F32) + jnp.dot(tril, d2, preferred_element_type=F32)
          + jnp.dot(tril, d3, preferred_element_type=F32))
    la = la * LOG2E
    la_t = la.T
    dt_t = dt.T
    last = jnp.broadcast_to(la_t[:, q - 1:q], (q, q))
    w_t = jnp.exp2(last - la_t) * dt_t
    state_decay = jnp.exp2(last)
    return causal_t, la, la_t, dt_t, w_t, state_decay


def _ssd_chunk_stages(r0, decays, xs_ref, bc_ref, z_ref, cw_ref, cb_ref, dskip_ref, nw_ref,
                      ssm_ref, state_ref, slab_ref):
    causal_t, la, la_t, dt_t, w_t, state_decay = decays
    q = SSD_CHUNK
    rq = slice(r0, r0 + q)
    gw = HEADS_PER_SSM_GROUP * SSM_HEAD_DIM
    xs_slabs = xs_ref.shape[3] // LANES
    slabs_per_group = gw // LANES

    for c in range(xs_slabs):
        slab_ref[c, SUBLANES:, :] = xs_ref[0, 0, rq, c * LANES:(c + 1) * LANES].astype(F32)
    for c in range(bc_ref.shape[3] // LANES):
        slab_ref[xs_slabs + c, SUBLANES:, :] = bc_ref[0, 0, rq, c * LANES:(c + 1) * LANES].astype(F32)

    for g in range(N_SSM_GROUPS):
        bm = _conv_silu(slab_ref, xs_slabs + g, cw_ref, cb_ref)
        cm = _conv_silu(slab_ref, xs_slabs + N_SSM_GROUPS + g, cw_ref, cb_ref)
        bm16 = bm.astype(BF16)
        cb_t = lax.dot_general(bm16, cm.astype(BF16), NT_DIMS, preferred_element_type=F32)
        cm_t = cm.T
        xs_g = jnp.concatenate([_conv_silu(slab_ref, g * slabs_per_group + c, cw_ref, cb_ref)
                                for c in range(slabs_per_group)], axis=1)
        xs_t = xs_g.T
        yield
        y_t = []
        for h in range(HEADS_PER_SSM_GROUP):
            hh = g * HEADS_PER_SSM_GROUP + h
            rows = slice(hh * SSM_HEAD_DIM, (hh + 1) * SSM_HEAD_DIM)
            la_i = la_t[hh:hh + 1, :]
            la_j = jnp.broadcast_to(la[:, hh:hh + 1], (q, q))
            decay_t = jnp.exp2(jnp.where(causal_t, la_i - la_j, -jnp.inf))
            rhs = jnp.concatenate([(cb_t * decay_t).astype(BF16), (cm_t * jnp.exp2(la_i)).astype(BF16)], axis=0)
            x_h = xs_t[h * SSM_HEAD_DIM:(h + 1) * SSM_HEAD_DIM]
            state = state_ref[rows, :]
            lhs = jnp.concatenate([(x_h * dt_t[hh:hh + 1, :]).astype(BF16), state.astype(BF16)], axis=1)
            y_t.append(jnp.dot(lhs, rhs, preferred_element_type=F32))
            if (h + 1) % SSD_HEADS_PER_STAGE == 0:
                yield
        heads = range(g * HEADS_PER_SSM_GROUP, (g + 1) * HEADS_PER_SSM_GROUP)
        grows = slice(heads[0] * SSM_HEAD_DIM, (heads[-1] + 1) * SSM_HEAD_DIM)
        def scale_heads(x, t):
            hp = x.reshape(HEADS_PER_SSM_GROUP, SSM_HEAD_DIM, q) * t[heads[0]:heads[-1] + 1][:, None, :]
            return hp.reshape(x.shape)
        upd = jnp.dot(scale_heads(xs_t, w_t).astype(BF16), bm16, preferred_element_type=F32)
        state_ref[grows, :] = scale_heads(state_ref[grows, :], state_decay) + upd
        yield
        y = jnp.concatenate(y_t, axis=0).T
        cols = slice(g * gw, (g + 1) * gw)
        y = y + xs_g * dskip_ref[:, cols]
        z = z_ref[0, 0, rq, cols].astype(F32)
        yg = y * (z * _sigmoid(z))
        yg = yg * lax.rsqrt(jnp.mean(yg * yg, axis=-1, keepdims=True) + EPS)
        ssm_ref[rq, cols] = (yg * nw_ref[:, cols]).astype(ssm_ref.dtype)
        yield

    slab_ref[:, 0:SUBLANES, :] = slab_ref[:, q:q + SUBLANES, :]


def _merge_ffn_stages(attn_refs, ssm_ref, gate_ref, h_ref, wa_ref, ws_ref, wo_ref, nat_ref,
                      nw_ref, wgu_ref, wd_ref, fw_ref, out_ref, *, ffn_chunk, final_norm):
    tb = h_ref.shape[1]
    n_slabs = GROUP_WIDTH // LANES
    for k, src in enumerate(attn_refs):
        dil = src.shape[1]
        for r in range(dil if dil > 1 else 0):
            for c in range(n_slabs):
                nat_ref[k, c, pl.ds(r, tb // dil, stride=dil), :] = src[0, r, :, c * LANES:(c + 1) * LANES]

    def natural(k, c):
        src = attn_refs[k]
        return src[0, 0, :, c * LANES:(c + 1) * LANES] if src.shape[1] == 1 else nat_ref[k, c]

    slabs = []
    for c in range(n_slabs):
        o = [natural(2 * g, c) for g in range(N_DIL)]
        l = [natural(2 * g + 1, c) for g in range(N_DIL)]
        mx = functools.reduce(jnp.maximum, l)
        e = [jnp.exp(lg - mx) for lg in l]
        num = functools.reduce(jnp.add, [eg * og for eg, og in zip(e, o)])
        slabs.append((num / functools.reduce(jnp.add, e)).astype(BF16))
    attn = jnp.concatenate(slabs, axis=1)
    yield
    a = jnp.dot(attn, wa_ref[...], preferred_element_type=F32)
    yield
    s = jnp.dot(ssm_ref[...], ws_ref[...], preferred_element_type=F32)
    d = a.shape[1]
    gates = _sigmoid(gate_ref[0, 0].astype(F32))
    merged = (gates[:, :d] * a + gates[:, d:] * s).astype(BF16)
    yield
    acc = h_ref[0] + jnp.dot(merged, wo_ref[...], preferred_element_type=F32)
    xn = _rmsnorm(acc, nw_ref[...]).astype(BF16)
    yield
    d_ff = wd_ref.shape[0]
    chunks = [slice(c0, c0 + ffn_chunk) for c0 in range(0, d_ff, ffn_chunk)]
    gate_up = None
    for c, cols in enumerate(chunks + [None]):
        prev_cols, prev_gate_up = (chunks[c - 1], gate_up) if c > 0 else (None, None)
        if cols is not None:
            up_cols = slice(d_ff + cols.start, d_ff + cols.stop)
            gate = jnp.dot(xn, wgu_ref[:, cols], preferred_element_type=F32)
            yield
            gate_up = (gate, jnp.dot(xn, wgu_ref[:, up_cols], preferred_element_type=F32))
            yield
        if prev_gate_up is not None:
            act = (prev_gate_up[0] * _sigmoid(prev_gate_up[0]) * prev_gate_up[1]).astype(BF16)
            acc = acc + jnp.dot(act, wd_ref[prev_cols, :], preferred_element_type=F32)
            yield
    out_ref[0] = _rmsnorm(acc, fw_ref[...]) if final_norm else acc


N_TAIL_WEIGHTS = 13


def _tail_kernel(*refs, blocks_per_seq, ffn_chunk, ssd_stages_per_matmul, final_norm):
    attn_refs = refs[:2 * N_DIL]
    gate_ref, h_ref, xs_ref, bc_ref, z_ref, dt_ref = refs[2 * N_DIL:2 * N_DIL + 6]
    weight_refs = refs[2 * N_DIL + 6:2 * N_DIL + 6 + N_TAIL_WEIGHTS]
    (wa_ref, ws_ref, wo_ref, n2_ref, wgu_ref, wd_ref, fw_ref,
     cw_ref, cb_ref, dtb_ref, alog_ref, dskip_ref, nw_ref) = [r.at[0] for r in weight_refs]
    out_ref, state_ref, slab_ref, ssm_ref, new_ref, nat_ref = refs[2 * N_DIL + 6 + N_TAIL_WEIGHTS:]

    @pl.when(lax.rem(pl.program_id(0), blocks_per_seq) == 0)
    def _():
        state_ref[...] = jnp.zeros_like(state_ref)
        slab_ref[:, 0:SUBLANES, :] = jnp.zeros((slab_ref.shape[0], SUBLANES, LANES), F32)

    @pl.when(pl.program_id(0) == 0)
    def _():
        ssm_ref[...] = jnp.zeros_like(ssm_ref)

    def ssd_stages():
        for r0 in range(0, xs_ref.shape[2], SSD_CHUNK):
            decays = _ssd_decays(r0, dt_ref, dtb_ref, alog_ref)
            yield
            yield from _ssd_chunk_stages(r0, decays, xs_ref, bc_ref, z_ref, cw_ref, cb_ref, dskip_ref, nw_ref,
                                         new_ref, state_ref, slab_ref)

    matmuls = _merge_ffn_stages(attn_refs, ssm_ref, gate_ref, h_ref, wa_ref, ws_ref, wo_ref, nat_ref,
                                n2_ref, wgu_ref, wd_ref, fw_ref, out_ref, ffn_chunk=ffn_chunk, final_norm=final_norm)
    turns = [(matmuls, 1), (ssd_stages(), ssd_stages_per_matmul)]
    live = [gen for gen, _ in turns]
    while live:
        for gen, reps in turns:
            for _ in range(reps if gen in live else 0):
                if next(gen, StopIteration) is StopIteration:
                    live.remove(gen)
                    break
    ssm_ref[...] = new_ref[...]


def _tail(attn_outs, gates, xs, bc, z, dt3, h3, weights, *, layer, tb, ffn_chunk, ssd_stages_per_matmul,
          final_norm):
    b, s, d = h3.shape
    assert len(weights) == N_TAIL_WEIGHTS
    conv_ch = weights[7].shape[-1]
    d_inner = xs.shape[-1]
    n_blocks = s // tb

    def cur(k):
        return jnp.divmod(jnp.minimum(k, b * n_blocks - 1), n_blocks)

    def prev(k):
        return jnp.divmod(jnp.maximum(k - 1, 0), n_blocks)

    def rows3(width, block_index):
        def index_map(k):
            bi, blk = block_index(k)
            return bi, blk, 0
        return pl.BlockSpec((1, tb, width), index_map)

    def rows4(arr, block_index):
        dil = arr.shape[1]

        def index_map(k):
            bi, blk = block_index(k)
            return bi, 0, blk, 0
        return pl.BlockSpec((1, dil, tb // dil, arr.shape[-1]), index_map)

    flat = [a for pair in attn_outs for a in pair]
    in_specs = [rows4(a, prev) for a in flat] + [
        rows4(gates, prev), rows3(d, prev), rows4(xs, cur), rows4(bc, cur), rows4(z, cur), rows3(LANES, cur),
    ] + [_layer_spec(w, min(layer, w.shape[0] - 1)) for w in weights]
    return pl.pallas_call(
        functools.partial(_tail_kernel, blocks_per_seq=n_blocks, ffn_chunk=ffn_chunk,
                          ssd_stages_per_matmul=ssd_stages_per_matmul, final_norm=final_norm),
        grid=(b * n_blocks + 1,),
        in_specs=in_specs,
        out_specs=rows3(d, prev),
        out_shape=jax.ShapeDtypeStruct((b, s, d), F32),
        scratch_shapes=[pltpu.VMEM((N_SSM_GROUPS * HEADS_PER_SSM_GROUP * SSM_HEAD_DIM, D_STATE), F32),
                        pltpu.VMEM((conv_ch // LANES, SUBLANES + SSD_CHUNK, LANES), F32),
                        pltpu.VMEM((tb, d_inner), BF16),
                        pltpu.VMEM((tb, d_inner), BF16),
                        pltpu.VMEM((len(flat), GROUP_WIDTH // LANES, tb, LANES), F32)],
        compiler_params=_compiler_params(("arbitrary",)),
        name="layer_tail",
    )(*flat, gates, h3, xs, bc, z, dt3, *weights)


def kernel(x, norm1_w, w_in, conv_w, conv_b, dt_bias, a_log, d_skip, ssm_norm_w, w_attn_branch, w_ssm_branch,
           w_out, norm2_w, w_ffn_in, w_ffn_out, rel_bias, final_norm_w):
    b, s, d = x.shape
    depth = w_in.shape[0]
    d_inner = ssm_norm_w.shape[1]
    n_heads = dt_bias.shape[1]
    bc_w = 2 * N_SSM_GROUPS * D_STATE
    assert d_inner == N_SSM_GROUPS * HEADS_PER_SSM_GROUP * SSM_HEAD_DIM and n_heads <= LANES
    assert s % (DILATED_GROUPS[-1][1] * ATTN_BLOCK) == 0 and s % SSD_CHUNK == 0 and DILATED_GROUPS[0][1] == 1

    o_z = 3 * ATTN_WIDTH
    o_xbc = o_z + d_inner
    o_dt = o_xbc + d_inner + bc_w
    o_gate = o_dt + n_heads
    assert o_dt % LANES == 0 and o_dt + LANES <= w_in.shape[2]
    w_in16 = w_in.astype(BF16)
    w_gates16 = w_in[:, :, o_gate:].astype(BF16)
    qkv_start = lambda part, g: part * ATTN_WIDTH + g * GROUP_WIDTH
    main_weights = [(w_in16, None, 0), (w_gates16, None, 0)]
    main_outputs = ((1, ((0, o_z, d_inner),)), (1, ((1, 0, 2 * d),)), (1, ((0, o_xbc, d_inner),)),
                    (1, ((0, o_xbc + d_inner, bc_w),)),
                    (1, tuple((0, qkv_start(part, 0), GROUP_WIDTH) for part in range(3))))
    dilated = [(g, dil) for g, (_, dil) in enumerate(DILATED_GROUPS) if dil > 1]
    dilated_weights = [(w_in16, GROUP_WIDTH, qkv_start(part, g) // GROUP_WIDTH) for g, _ in dilated for part in range(3)]
    dilated_outputs = tuple((dil, tuple((3 * n + part, 0, GROUP_WIDTH) for part in range(3)))
                            for n, (_, dil) in enumerate(dilated))

    pad_h = ((0, 0), (0, 0), (0, LANES - n_heads))
    row3 = lambda a: a[:, None, :]
    tail_weights = (w_attn_branch.astype(BF16), w_ssm_branch.astype(BF16), w_out.astype(BF16), row3(norm2_w),
                    w_ffn_in.astype(BF16), w_ffn_out.astype(BF16), final_norm_w[None, None, :],
                    conv_w, row3(conv_b), jnp.pad(row3(dt_bias), pad_h), jnp.pad(row3(a_log), pad_h),
                    row3(jnp.repeat(d_skip, SSM_HEAD_DIM, axis=1)), row3(ssm_norm_w))
    norm1_w3 = row3(norm1_w)
    bias_tables = _attention_bias_tables(rel_bias)

    h = x
    for layer in range(depth):
        z, gates, xs, bc, qkv0, dt = _in_proj(h, norm1_w3, main_weights, main_outputs, (0, o_dt, LANES),
                                              layer=layer, tm=IN_PROJ_ROWS, chunk=IN_PROJ_CHUNK, name="in_proj")
        qkv_dilated = _in_proj(h, norm1_w3, dilated_weights, dilated_outputs, layer=layer,
                               tm=IN_PROJ_DILATED_ROWS, chunk=IN_PROJ_CHUNK, name="in_proj_dilated")
        qkvs = {0: qkv0, **{g: arr for (g, _), arr in zip(dilated, qkv_dilated)}}
        attn_outs = [_attention(qkvs[g], bias_tables[g], col_q=0, col_k=1, col_v=2, rows=ATTN_ROWS)
                     for g in range(N_DIL)]
        h = _tail(attn_outs, gates, xs, bc, z, dt, h, tail_weights, layer=layer, tb=TAIL_ROWS,
                  ffn_chunk=FFN_CHUNK, ssd_stages_per_matmul=SSD_STAGES_PER_MATMUL,
                  final_norm=(layer == depth - 1))
    return h
```
